```python
import math
import jax, jax.numpy as jnp
from jax import lax
import numpy as np

D_MODEL = 1024
BATCH = 32
SEQ = 256
DEPTH = 1
DEC_BATCH = 8
DEC_SEQ = 2048
PAST_LEN = 256

GRID_W = 64
N_HEADS = 8
QK_NOPE_DIM = 64
QK_ROPE_DIM = 32
V_HEAD_DIM = 64
QK_DIM = QK_NOPE_DIM + QK_ROPE_DIM
Q_LORA_RANK = 256
KV_LORA_RANK = 128
MLA_WIDTH = N_HEADS * V_HEAD_DIM
POOL_WIDTH = D_MODEL - MLA_WIDTH
POOL_WINDOWS = (2, 4, 8, 16)
N_POOL_GROUPS = len(POOL_WINDOWS)
POOL_GROUP = POOL_WIDTH // N_POOL_GROUPS
IN_COLS = Q_LORA_RANK + KV_LORA_RANK + QK_ROPE_DIM + POOL_WIDTH
D_FF = 4 * D_MODEL
Q_BLOCK = 128
ROPE_THETA = 10000.0
AXIS_ROPE = QK_ROPE_DIM // 2
N_FREQ = AXIS_ROPE // 2
EPS = 1e-6
ATTN_SCALE = 1.0 / math.sqrt(QK_DIM)

kernel_name = "hybrid_mla_pool_dit_step"


def rmsnorm(x, g):
    xf = x.astype(jnp.float32)
    y = xf * lax.rsqrt(jnp.mean(xf * xf, axis=-1, keepdims=True) + EPS)
    return (y * g.astype(jnp.float32)).astype(x.dtype)


def grid_rope_tables(n_tokens, dtype):
    rows = n_tokens // GRID_W
    row = jnp.repeat(jnp.arange(rows, dtype=jnp.float32), GRID_W)
    col = jnp.tile(jnp.arange(GRID_W, dtype=jnp.float32), rows)
    inv_freq = 1.0 / (ROPE_THETA ** (jnp.arange(N_FREQ, dtype=jnp.float32) * 2.0 / AXIS_ROPE))
    ang = jnp.stack([row[:, None] * inv_freq, col[:, None] * inv_freq], axis=1)
    return jnp.cos(ang).astype(dtype), jnp.sin(ang).astype(dtype)


def axial_rope(x, cos, sin):
    x2 = x.reshape(x.shape[:-1] + (2, 2, N_FREQ))
    a, b = x2[..., 0, :], x2[..., 1, :]
    out = jnp.stack([a * cos - b * sin, b * cos + a * sin], axis=-2)
    return out.reshape(x.shape)


def mla_attend(q_nope, q_rope, k_nope, k_rope, v):
    B, S = q_nope.shape[:2]
    nblk = S // Q_BLOCK
    qn = jnp.moveaxis(q_nope.reshape(B, nblk, Q_BLOCK, N_HEADS, QK_NOPE_DIM), 1, 0)
    qr = jnp.moveaxis(q_rope.reshape(B, nblk, Q_BLOCK, N_HEADS, QK_ROPE_DIM), 1, 0)

    def block(args):
        qn_b, qr_b = args
        s = (jnp.einsum('bqhd,bkhd->bhqk', qn_b, k_nope)
             + jnp.einsum('bqhr,bkr->bhqk', qr_b, k_rope)) * ATTN_SCALE
        p = jax.nn.softmax(s.astype(jnp.float32), axis=-1).astype(v.dtype)
        return jnp.einsum('bhqk,bkhd->bqhd', p, v)

    out = lax.map(block, (qn, qr))
    return jnp.moveaxis(out, 0, 1).reshape(B, S, MLA_WIDTH)


def pool_mix(u, w_pool, pool_scale):
    B, S, _ = u.shape
    uf = u.astype(jnp.float32)
    cs = jnp.concatenate([jnp.zeros((B, 1, POOL_WIDTH), jnp.float32), jnp.cumsum(uf, axis=1)], axis=1)
    t = jnp.arange(S)
    outs = []
    for g, w in enumerate(POOL_WINDOWS):
        lo = jnp.clip(t - w // 2, 0, S)
        hi = jnp.clip(t + w // 2, 0, S)
        seg = cs[..., g * POOL_GROUP:(g + 1) * POOL_GROUP]
        mean = (seg[:, hi] - seg[:, lo]) / (hi - lo).astype(jnp.float32)[:, None]
        outs.append(mean - uf[..., g * POOL_GROUP:(g + 1) * POOL_GROUP])
    pooled = jnp.stack(outs, axis=2).astype(u.dtype)
    mixed = jnp.einsum('bsgc,gcd->bsgd', pooled, w_pool).reshape(B, S, POOL_WIDTH)
    return mixed * pool_scale


def trunk_layer(x, mod, p, rope, ctx):
    (norm1, w_in, q_norm, w_qb, kv_norm, w_kvb, w_pool, pool_scale, w_o, norm2, w1, w2) = p
    shift1, scale1, gate1, shift2, scale2, gate2 = jnp.split(mod, 6, axis=-1)
    B, S, _ = x.shape
    h = rmsnorm(x, norm1) * (1 + scale1) + shift1
    proj = h @ w_in
    q_a, kv_a, k_r, u = jnp.split(
        proj, [Q_LORA_RANK, Q_LORA_RANK + KV_LORA_RANK, Q_LORA_RANK + KV_LORA_RANK + QK_ROPE_DIM], axis=-1)
    q = (rmsnorm(q_a, q_norm) @ w_qb).reshape(B, S, N_HEADS, QK_DIM)
    q_nope, q_rope = q[..., :QK_NOPE_DIM], q[..., QK_NOPE_DIM:]
    ckv = rmsnorm(kv_a, kv_norm)
    k_rope = k_r
    if rope is not None:
        cos, sin = rope
        q_rope = axial_rope(q_rope, cos[:, None], sin[:, None])
        k_rope = axial_rope(k_rope, cos, sin)
    keys_ckv, keys_kr = ckv, k_rope
    if ctx is not None:
        ctx_ckv, ctx_kr = ctx
        keys_ckv = jnp.concatenate([keys_ckv, ctx_ckv], axis=1)
        keys_kr = jnp.concatenate([keys_kr, ctx_kr], axis=1)
    K = keys_ckv.shape[1]
    kv = (keys_ckv @ w_kvb).reshape(B, K, N_HEADS, QK_NOPE_DIM + V_HEAD_DIM)
    k_nope, v = kv[..., :QK_NOPE_DIM], kv[..., QK_NOPE_DIM:]
    attn = mla_attend(q_nope, q_rope, k_nope, keys_kr, v)
    pooled = pool_mix(u, w_pool, pool_scale)
    x = x + gate1 * (jnp.concatenate([attn, pooled], axis=-1) @ w_o)
    h2 = rmsnorm(x, norm2) * (1 + scale2) + shift2
    x = x + gate2 * (jnp.square(jax.nn.relu(h2 @ w1)) @ w2)
    return x, ckv, k_rope


def setup_inputs(seed: int = 0) -> dict:
    key = jax.random.key(seed)
    ks = jax.random.split(key, 24)
    f = jnp.float32
    nrm = lambda k, shape, s: jax.random.normal(k, shape, f) * s
    return {
        'x_prompt': nrm(ks[0], (BATCH, SEQ, D_MODEL), 1.0),
        'x_sample': nrm(ks[1], (DEC_BATCH, DEC_SEQ, D_MODEL), 1.0),
        'cache_ckv': nrm(ks[2], (DEC_BATCH, DEPTH, PAST_LEN, KV_LORA_RANK), 1.0),
        'cache_krope': nrm(ks[3], (DEC_BATCH, DEPTH, PAST_LEN, QK_ROPE_DIM), 1.0),
        'c': nrm(ks[4], (DEC_BATCH, D_MODEL), 1.0),
        'c_ctx': nrm(ks[5], (D_MODEL,), 1.0),
        'w_ada': nrm(ks[6], (DEPTH, D_MODEL, 6 * D_MODEL), 0.5 * D_MODEL ** -0.5),
        'b_ada': nrm(ks[7], (DEPTH, 6 * D_MODEL), 0.1),
        'norm1': 1.0 + nrm(ks[8], (DEPTH, D_MODEL), 0.05),
        'w_in': nrm(ks[9], (DEPTH, D_MODEL, IN_COLS), D_MODEL ** -0.5),
        'q_norm': 1.0 + nrm(ks[10], (DEPTH, Q_LORA_RANK), 0.05),
        'w_qb': nrm(ks[11], (DEPTH, Q_LORA_RANK, N_HEADS * QK_DIM), Q_LORA_RANK ** -0.5),
        'kv_norm': 1.0 + nrm(ks[12], (DEPTH, KV_LORA_RANK), 0.05),
        'w_kvb': nrm(ks[13], (DEPTH, KV_LORA_RANK, N_HEADS * (QK_NOPE_DIM + V_HEAD_DIM)), KV_LORA_RANK ** -0.5),
        'w_pool': nrm(ks[14], (DEPTH, N_POOL_GROUPS, POOL_GROUP, POOL_GROUP), POOL_GROUP ** -0.5),
        'pool_scale': 1.0 + nrm(ks[15], (DEPTH, POOL_WIDTH), 0.05),
        'w_o': nrm(ks[16], (DEPTH, D_MODEL, D_MODEL), D_MODEL ** -0.5),
        'norm2': 1.0 + nrm(ks[17], (DEPTH, D_MODEL), 0.05),
        'w1': nrm(ks[18], (DEPTH, D_MODEL, D_FF), D_MODEL ** -0.5),
        'w2': nrm(ks[19], (DEPTH, D_FF, D_MODEL), D_FF ** -0.5),
        'final_norm': 1.0 + nrm(ks[20], (D_MODEL,), 0.05),
    }


def reference(x_prompt, x_sample, cache_ckv, cache_krope, c, c_ctx, w_ada, b_ada, norm1, w_in,
              q_norm, w_qb, kv_norm, w_kvb, w_pool, pool_scale, w_o, norm2, w1, w2, final_norm):
    n_lat = x_sample.shape[1]
    rope = grid_rope_tables(n_lat, x_sample.dtype)
    silu_c = jax.nn.silu(c)
    silu_ctx = jax.nn.silu(c_ctx)
    xc, xl = x_prompt, x_sample
    new_ckv, new_krope = [], []
    for l in range(DEPTH):
        p = (norm1[l], w_in[l], q_norm[l], w_qb[l], kv_norm[l], w_kvb[l], w_pool[l], pool_scale[l],
             w_o[l], norm2[l], w1[l], w2[l])
        mod_ctx = (silu_ctx @ w_ada[l] + b_ada[l])[None, None, :]
        mod_lat = (silu_c @ w_ada[l] + b_ada[l])[:, None, :]
        xc, ckv, krope = trunk_layer(xc, mod_ctx, p, None, None)
        new_ckv.append(ckv)
        new_krope.append(krope)
        xl, _, _ = trunk_layer(xl, mod_lat, p, rope, (cache_ckv[:, l], cache_krope[:, l]))
    y_prompt = rmsnorm(xc, final_norm)
    y_sample = rmsnorm(xl, final_norm)
    state_ckv = jnp.stack(new_ckv, axis=1)
    state_krope = jnp.stack(new_krope, axis=1)
    return (y_prompt, y_sample, state_ckv, state_krope)
```

```python
import functools
import math

import jax
import jax.numpy as jnp
from jax import lax
from jax.experimental import pallas as pl
from jax.experimental.pallas import tpu as pltpu

D_MODEL = 1024
N_HEADS = 8
QK_NOPE_DIM = 64
QK_ROPE_DIM = 32
V_HEAD_DIM = 64
QK_DIM = QK_NOPE_DIM + QK_ROPE_DIM
Q_LORA_RANK = 256
KV_LORA_RANK = 128
MLA_WIDTH = N_HEADS * V_HEAD_DIM
POOL_WIDTH = D_MODEL - MLA_WIDTH
POOL_WINDOWS = (2, 4, 8, 16)
POOL_GROUP = POOL_WIDTH // len(POOL_WINDOWS)
D_FF = 4 * D_MODEL
GRID_W = 64
ROPE_THETA = 10000.0
AXIS_ROPE = QK_ROPE_DIM // 2
N_FREQ = AXIS_ROPE // 2
EPS = 1e-6
ATTN_SCALE = 1.0 / math.sqrt(QK_DIM)

LANES = 128
SUBLANES = 8
HEAD_SLAB = LANES
QK_WIDTH = N_HEADS * HEAD_SLAB
MOD_ROWS = 16
POOL_HALO = max(POOL_WINDOWS) // 2
KR_OFF = Q_LORA_RANK + KV_LORA_RANK
U_OFF = KR_OFF + LANES
PROJ_COLS = U_OFF + POOL_WIDTH
VMEM_LIMIT = 56 * 1024 * 1024

F32 = jnp.float32
BF16 = jnp.bfloat16


def _dot(a, b):
    return jnp.dot(a, b, preferred_element_type=F32)


def _rms(x, g):
    return x * lax.rsqrt(jnp.mean(x * x, axis=-1, keepdims=True) + EPS) * g


def _mod_kernel(c_ref, w_ref, b_ref, o_ref):
    c = c_ref[...]
    s = (c * jax.nn.sigmoid(c)).astype(BF16)
    o_ref[...] = _dot(s, w_ref[...].astype(BF16)) + b_ref[...]


def _mod_call(c_all, w_ada, b_ada):
    tn = 1024
    n = w_ada.shape[1]
    return pl.pallas_call(
        _mod_kernel,
        grid=(n // tn,),
        in_specs=[
            pl.BlockSpec((MOD_ROWS, D_MODEL), lambda j: (0, 0)),
            pl.BlockSpec((D_MODEL, tn), lambda j: (0, j)),
            pl.BlockSpec((1, tn), lambda j: (0, j)),
        ],
        out_specs=pl.BlockSpec((MOD_ROWS, tn), lambda j: (0, j)),
        out_shape=jax.ShapeDtypeStruct((MOD_ROWS, n), F32),
        compiler_params=pltpu.CompilerParams(dimension_semantics=("arbitrary",)),
        name="mod",
    )(c_all, w_ada, b_ada)


def _rope_slab(slab, c, s_up, s_dn):
    return (slab * c + pltpu.roll(slab, N_FREQ, 1) * s_up
            + pltpu.roll(slab, LANES - N_FREQ, 1) * s_dn)


def _stage1_kernel(*refs, use_rope, emit_cache):
    it = iter(refs)
    x_ref, mod_ref, norm1_ref, w_in_ref, qn_ref, wq_ref, kvn_ref, wk_ref, wv_ref = (next(it) for _ in range(9))
    if use_rope:
        cq_ref, uq_ref, dq_ref, ck_ref, uk_ref, dk_ref = (next(it) for _ in range(6))
    q_ref, k_ref, v_ref, u_ref = (next(it) for _ in range(4))
    if emit_cache:
        ckv_ref, kr_ref = (next(it) for _ in range(2))

    x = x_ref[...]
    gain = norm1_ref[...] * (1.0 + mod_ref[0, 1:2, :])
    h = _rms(x, gain) + mod_ref[0, 0:1, :]
    proj = _dot(h.astype(BF16), w_in_ref[...])

    u_ref[...] = proj[:, U_OFF:]

    qa = _rms(proj[:, :Q_LORA_RANK], qn_ref[...]).astype(BF16)
    q = _dot(qa, wq_ref[...])
    for hd in range(N_HEADS):
        slab = q[:, hd * HEAD_SLAB:(hd + 1) * HEAD_SLAB]
        if use_rope:
            slab = _rope_slab(slab, cq_ref[...], uq_ref[...], dq_ref[...])
        else:
            slab = slab * ATTN_SCALE
        q_ref[:, hd * HEAD_SLAB:(hd + 1) * HEAD_SLAB] = slab.astype(BF16)

    ckv = _rms(proj[:, Q_LORA_RANK:KR_OFF], kvn_ref[...])
    kr = proj[:, KR_OFF:U_OFF]
    if emit_cache:
        ckv_ref[...] = ckv
        kr_ref[...] = kr[:, :QK_ROPE_DIM]
    if use_rope:
        kr = _rope_slab(kr, ck_ref[...], uk_ref[...], dk_ref[...])
    ckv16 = ckv.astype(BF16)
    kin = jnp.concatenate([ckv16, kr.astype(BF16)], axis=1)
    k_ref[...] = _dot(kin, wk_ref[...]).astype(BF16)
    v_ref[...] = _dot(ckv16, wv_ref[...]).astype(BF16)


def _const_spec(shape):
    nd = len(shape)
    return pl.BlockSpec(shape, lambda i: (0,) * nd)


def _stage1_call(x2d, mod3, mod_row_fn, weights, rope_tabs, seq, tm, emit_cache):
    n_tok = x2d.shape[0]
    use_rope = rope_tabs is not None
    norm1, w_in, q_norm, wq, kv_norm, wk, wv = weights
    tiles_per_seq = seq // tm
    in_specs = [
        pl.BlockSpec((tm, D_MODEL), lambda i: (i, 0)),
        pl.BlockSpec((1, 6, D_MODEL), lambda i: (mod_row_fn(i), 0, 0)),
        _const_spec(norm1.shape), _const_spec(w_in.shape), _const_spec(q_norm.shape),
        _const_spec(wq.shape), _const_spec(kv_norm.shape), _const_spec(wk.shape), _const_spec(wv.shape),
    ]
    args = [x2d, mod3, norm1, w_in, q_norm, wq, kv_norm, wk, wv]
    if use_rope:
        in_specs += [pl.BlockSpec((tm, LANES), lambda i: (i % tiles_per_seq, 0))] * 6
        args += list(rope_tabs)
    out_specs = [
        pl.BlockSpec((tm, QK_WIDTH), lambda i: (i, 0)),
        pl.BlockSpec((tm, QK_WIDTH), lambda i: (i, 0)),
        pl.BlockSpec((tm, MLA_WIDTH), lambda i: (i, 0)),
        pl.BlockSpec((tm, POOL_WIDTH), lambda i: (i, 0)),
    ]
    out_shape = [
        jax.ShapeDtypeStruct((n_tok, QK_WIDTH), BF16),
        jax.ShapeDtypeStruct((n_tok, QK_WIDTH), BF16),
        jax.ShapeDtypeStruct((n_tok, MLA_WIDTH), BF16),
        jax.ShapeDtypeStruct((n_tok, POOL_WIDTH), F32),
    ]
    if emit_cache:
        out_specs += [pl.BlockSpec((tm, KV_LORA_RANK), lambda i: (i, 0)),
                      pl.BlockSpec((tm, QK_ROPE_DIM), lambda i: (i, 0))]
        out_shape += [jax.ShapeDtypeStruct((n_tok, KV_LORA_RANK), F32),
                      jax.ShapeDtypeStruct((n_tok, QK_ROPE_DIM), F32)]
    return pl.pallas_call(
        functools.partial(_stage1_kernel, use_rope=use_rope, emit_cache=emit_cache),
        grid=(n_tok // tm,),
        in_specs=in_specs,
        out_specs=out_specs,
        out_shape=out_shape,
        compiler_params=pltpu.CompilerParams(dimension_semantics=("parallel",),
                                             vmem_limit_bytes=VMEM_LIMIT),
        name="stage1_rope" if use_rope else "stage1_ctx",
    )(*args)


def _kv_up_kernel(ckv_ref, kr_ref, wk_ref, wv_ref, k_ref, v_ref):
    ckv16 = ckv_ref[...].astype(BF16)
    kr = kr_ref[...].astype(BF16)
    pad = jnp.zeros((kr.shape[0], LANES - QK_ROPE_DIM), BF16)
    kin = jnp.concatenate([ckv16, kr, pad], axis=1)
    k_ref[...] = _dot(kin, wk_ref[...]).astype(BF16)
    v_ref[...] = _dot(ckv16, wv_ref[...]).astype(BF16)


def _kv_up_call(ckv2d, kr2d, wk, wv, tm):
    n_tok = ckv2d.shape[0]
    return pl.pallas_call(
        _kv_up_kernel,
        grid=(n_tok // tm,),
        in_specs=[
            pl.BlockSpec((tm, KV_LORA_RANK), lambda i: (i, 0)),
            pl.BlockSpec((tm, QK_ROPE_DIM), lambda i: (i, 0)),
            _const_spec(wk.shape), _const_spec(wv.shape),
        ],
        out_specs=[pl.BlockSpec((tm, QK_WIDTH), lambda i: (i, 0)),
                   pl.BlockSpec((tm, MLA_WIDTH), lambda i: (i, 0))],
        out_shape=[jax.ShapeDtypeStruct((n_tok, QK_WIDTH), BF16),
                   jax.ShapeDtypeStruct((n_tok, MLA_WIDTH), BF16)],
        compiler_params=pltpu.CompilerParams(dimension_semantics=("parallel",)),
        name="kv_up",
    )(ckv2d, kr2d, wk, wv)


def _qk(qh, kh):
    return lax.dot_general(qh, kh, (((1,), (1,)), ((), ())), preferred_element_type=F32)


def _attn_kernel(*refs, heads, with_cache):
    if with_cache:
        q_ref, k_ref, v_ref, kc_ref, vc_ref, o_ref = refs
    else:
        q_ref, k_ref, v_ref, o_ref = refs
    for hd in range(heads):
        qs = slice(hd * HEAD_SLAB, (hd + 1) * HEAD_SLAB)
        vs = slice(hd * V_HEAD_DIM, (hd + 1) * V_HEAD_DIM)
        qh = q_ref[0, :, qs]
        s = _qk(qh, k_ref[0, :, qs])
        m = jnp.max(s, axis=-1, keepdims=True)
        if with_cache:
            sc = _qk(qh, kc_ref[0, :, qs])
            m = jnp.maximum(m, jnp.max(sc, axis=-1, keepdims=True))
        p = jnp.exp(s - m)
        l = jnp.sum(p, axis=-1, keepdims=True)
        o = _dot(p.astype(BF16), v_ref[0, :, vs])
        if with_cache:
            pc = jnp.exp(sc - m)
            l = l + jnp.sum(pc, axis=-1, keepdims=True)
            o = o + _dot(pc.astype(BF16), vc_ref[0, :, vs])
        o_ref[0, :, vs] = (o * (1.0 / l)).astype(BF16)


def _attn_call(q3, k3, v3, cache, tq, heads):
    b, s, _ = q3.shape
    nk = k3.shape[1]
    groups = N_HEADS // heads
    qw, vw = heads * HEAD_SLAB, heads * V_HEAD_DIM
    in_specs = [
        pl.BlockSpec((1, tq, qw), lambda bi, g, i: (bi, i, g)),
        pl.BlockSpec((1, nk, qw), lambda bi, g, i: (bi, 0, g)),
        pl.BlockSpec((1, nk, vw), lambda bi, g, i: (bi, 0, g)),
    ]
    args = [q3, k3, v3]
    if cache is not None:
        kc3, vc3 = cache
        nc = kc3.shape[1]
        in_specs += [pl.BlockSpec((1, nc, qw), lambda bi, g, i: (bi, 0, g)),
                     pl.BlockSpec((1, nc, vw), lambda bi, g, i: (bi, 0, g))]
        args += [kc3, vc3]
    return pl.pallas_call(
        functools.partial(_attn_kernel, heads=heads, with_cache=cache is not None),
        grid=(b, groups, s // tq),
        in_specs=in_specs,
        out_specs=pl.BlockSpec((1, tq, vw), lambda bi, g, i: (bi, i, g)),
        out_shape=jax.ShapeDtypeStruct((b, s, MLA_WIDTH), BF16),
        compiler_params=pltpu.CompilerParams(
            dimension_semantics=("parallel", "parallel", "parallel"),
            vmem_limit_bytes=VMEM_LIMIT),
        name="attn_lat" if cache is not None else "attn_ctx",
    )(*args)


def _post_kernel(x_ref, a_ref, u_ref, up_ref, un_ref, mod_ref, wp_ref, ps_ref, wo_ref,
                 norm2_ref, w1_ref, w2_ref, fn_ref, o_ref, e_ref, *, tm, seq, ff_chunk):
    i = pl.program_id(0)
    pos0 = (i * tm) % seq
    u = u_ref[...]
    e_ref[0:POOL_HALO, :] = jnp.where(pos0 > 0, up_ref[...], 0.0)
    e_ref[POOL_HALO:POOL_HALO + tm, :] = u
    e_ref[POOL_HALO + tm:, :] = jnp.where(pos0 + tm < seq, un_ref[...], 0.0)

    t = pos0 + lax.broadcasted_iota(jnp.int32, (tm, 1), 0)
    mixed = []
    for g, w in enumerate(POOL_WINDOWS):
        half = w // 2
        cols = slice(g * POOL_GROUP, (g + 1) * POOL_GROUP)
        acc = e_ref[POOL_HALO - half:POOL_HALO - half + tm, cols]
        for d in range(-half + 1, half):
            acc = acc + e_ref[POOL_HALO + d:POOL_HALO + d + tm, cols]
        cnt = (jnp.minimum(t + half, seq) - jnp.maximum(t - half, 0)).astype(F32)
        pooled = acc * (1.0 / cnt) - u[:, cols]
        mg = _dot(pooled.astype(BF16), wp_ref[g]) * ps_ref[:, cols]
        mixed.append(mg.astype(BF16))
    cat = jnp.concatenate([a_ref[...]] + mixed, axis=1)
    y = x_ref[...] + mod_ref[0, 2:3, :] * _dot(cat, wo_ref[...])

    gain = norm2_ref[...] * (1.0 + mod_ref[0, 4:5, :])
    h2 = (_rms(y, gain) + mod_ref[0, 3:4, :]).astype(BF16)
    acc = jnp.zeros((tm, D_MODEL), F32)
    for c in range(D_FF // ff_chunk):
        a = _dot(h2, w1_ref[:, c * ff_chunk:(c + 1) * ff_chunk])
        a = jnp.square(jnp.maximum(a, 0.0)).astype(BF16)
        acc = acc + _dot(a, w2_ref[c * ff_chunk:(c + 1) * ff_chunk, :])
    x2 = y + mod_ref[0, 5:6, :] * acc
    o_ref[...] = _rms(x2, fn_ref[...])


def _resident_spec(shape):
    nd = len(shape)
    return pl.BlockSpec(shape, lambda i: (0,) * nd, pipeline_mode=pl.Buffered(1))


def _post_call(x2d, attn2d, u2d, mod3, mod_row_fn, weights, seq, tm):
    n_tok = x2d.shape[0]
    w_pool, pool_scale, w_o, norm2, w1, w2, final_norm = weights
    halo_blocks = n_tok // POOL_HALO
    blocks_per_tile = tm // POOL_HALO
    in_specs = [
        pl.BlockSpec((tm, D_MODEL), lambda i: (i, 0)),
        pl.BlockSpec((tm, MLA_WIDTH), lambda i: (i, 0)),
        pl.BlockSpec((tm, POOL_WIDTH), lambda i: (i, 0)),
        pl.BlockSpec((POOL_HALO, POOL_WIDTH), lambda i: (jnp.maximum(i * blocks_per_tile - 1, 0), 0)),
        pl.BlockSpec((POOL_HALO, POOL_WIDTH),
                     lambda i: (jnp.minimum((i + 1) * blocks_per_tile, halo_blocks - 1), 0)),
        pl.BlockSpec((1, 6, D_MODEL), lambda i: (mod_row_fn(i), 0, 0)),
        _resident_spec(w_pool.shape), _resident_spec(pool_scale.shape), _resident_spec(w_o.shape),
        _resident_spec(norm2.shape), _resident_spec(w1.shape), _resident_spec(w2.shape),
        _resident_spec(final_norm.shape),
    ]
    return pl.pallas_call(
        functools.partial(_post_kernel, tm=tm, seq=seq, ff_chunk=1024),
        grid=(n_tok // tm,),
        in_specs=in_specs,
        out_specs=pl.BlockSpec((tm, D_MODEL), lambda i: (i, 0)),
        out_shape=jax.ShapeDtypeStruct((n_tok, D_MODEL), F32),
        scratch_shapes=[pltpu.VMEM((tm + 2 * POOL_HALO, POOL_WIDTH), F32)],
        compiler_params=pltpu.CompilerParams(dimension_semantics=("parallel",),
                                             vmem_limit_bytes=VMEM_LIMIT),
        name="post",
    )(x2d, attn2d, u2d, u2d, u2d, mod3, w_pool, pool_scale, w_o, norm2, w1, w2, final_norm)


def _pack_w_in(w_in):
    qa = w_in[:, :Q_LORA_RANK + KV_LORA_RANK]
    kr = w_in[:, KR_OFF:KR_OFF + QK_ROPE_DIM]
    u = w_in[:, KR_OFF + QK_ROPE_DIM:]
    pad = jnp.zeros((D_MODEL, LANES - QK_ROPE_DIM), w_in.dtype)
    return jnp.concatenate([qa, kr, pad, u], axis=1).astype(BF16)


def _pack_wq(w_qb):
    w = w_qb.reshape(Q_LORA_RANK, N_HEADS, QK_DIM)
    w = jnp.pad(w, ((0, 0), (0, 0), (0, HEAD_SLAB - QK_DIM)))
    return w.reshape(Q_LORA_RANK, QK_WIDTH).astype(BF16)


def _pack_wkv(w_kvb):
    w = w_kvb.reshape(KV_LORA_RANK, N_HEADS, QK_NOPE_DIM + V_HEAD_DIM)
    wk_nope = jnp.pad(w[:, :, :QK_NOPE_DIM], ((0, 0), (0, 0), (0, HEAD_SLAB - QK_NOPE_DIM)))
    sel = jnp.pad(jnp.eye(QK_ROPE_DIM, dtype=w_kvb.dtype),
                  ((0, 0), (QK_NOPE_DIM, HEAD_SLAB - QK_DIM)))
    sel = jnp.broadcast_to(sel[:, None, :], (QK_ROPE_DIM, N_HEADS, HEAD_SLAB))
    zero = jnp.zeros((LANES - QK_ROPE_DIM, N_HEADS, HEAD_SLAB), w_kvb.dtype)
    wk = jnp.concatenate([wk_nope, sel, zero], axis=0).reshape(KV_LORA_RANK + LANES, QK_WIDTH)
    wv = w[:, :, QK_NOPE_DIM:].reshape(KV_LORA_RANK, MLA_WIDTH)
    return wk.astype(BF16), wv.astype(BF16)


def _rope_tables(n_tokens):
    rows = n_tokens // GRID_W
    row = jnp.repeat(jnp.arange(rows, dtype=F32), GRID_W)
    col = jnp.tile(jnp.arange(GRID_W, dtype=F32), rows)
    inv_freq = 1.0 / (ROPE_THETA ** (jnp.arange(N_FREQ, dtype=F32) * 2.0 / AXIS_ROPE))
    cos_r, sin_r = jnp.cos(row[:, None] * inv_freq), jnp.sin(row[:, None] * inv_freq)
    cos_c, sin_c = jnp.cos(col[:, None] * inv_freq), jnp.sin(col[:, None] * inv_freq)
    z = jnp.zeros_like(cos_r)
    c32 = jnp.concatenate([cos_r, cos_r, cos_c, cos_c], axis=1)
    up32 = jnp.concatenate([z, sin_r, z, sin_c], axis=1)
    dn32 = jnp.concatenate([-sin_r, z, -sin_c, z], axis=1)

    def place(t32, off, fill):
        left = jnp.full((n_tokens, off), fill, F32)
        right = jnp.zeros((n_tokens, LANES - off - QK_ROPE_DIM), F32)
        return jnp.concatenate([left, t32, right], axis=1)

    q_tabs = (place(c32 * ATTN_SCALE, QK_NOPE_DIM, ATTN_SCALE),
              place(up32 * ATTN_SCALE, QK_NOPE_DIM, 0.0),
              place(dn32 * ATTN_SCALE, QK_NOPE_DIM, 0.0))
    k_tabs = (place(c32, 0, 0.0), place(up32, 0, 0.0), place(dn32, 0, 0.0))
    return q_tabs + k_tabs


def kernel(x_prompt, x_sample, cache_ckv, cache_krope, c, c_ctx, w_ada, b_ada, norm1, w_in, q_norm,
           w_qb, kv_norm, w_kvb, w_pool, pool_scale, w_o, norm2, w1, w2, final_norm):
    batch, seq, _ = x_prompt.shape
    dec_batch, dec_seq, _ = x_sample.shape
    depth = w_in.shape[0]
    past = cache_ckv.shape[2]
    assert depth == 1, "single trunk layer"
    assert dec_batch + 1 <= MOD_ROWS

    c_all = jnp.concatenate(
        [c, c_ctx[None, :], jnp.zeros((MOD_ROWS - dec_batch - 1, D_MODEL), F32)], axis=0)
    mod3 = _mod_call(c_all, w_ada[0], b_ada[0][None, :]).reshape(MOD_ROWS, 6, D_MODEL)

    wk, wv = _pack_wkv(w_kvb[0])
    s1_weights = (norm1[0][None, :], _pack_w_in(w_in[0]), q_norm[0][None, :], _pack_wq(w_qb[0]),
                  kv_norm[0][None, :], wk, wv)
    post_weights = (w_pool[0].astype(BF16), pool_scale[0][None, :], w_o[0].astype(BF16),
                    norm2[0][None, :], w1[0].astype(BF16), w2[0].astype(BF16), final_norm[None, :])

    tm_c = 256
    xc = x_prompt.reshape(batch * seq, D_MODEL)
    ctx_row = lambda i: dec_batch
    qc, kc, vc, uc, ckv_new, kr_new = _stage1_call(xc, mod3, ctx_row, s1_weights, None, seq, tm_c, True)
    attn_c = _attn_call(qc.reshape(batch, seq, QK_WIDTH), kc.reshape(batch, seq, QK_WIDTH),
                        vc.reshape(batch, seq, MLA_WIDTH), None, seq, N_HEADS)
    y_prompt = _post_call(xc, attn_c.reshape(batch * seq, MLA_WIDTH), uc, mod3, ctx_row,
                          post_weights, seq, tm_c)

    tm_l = 512
    xl = x_sample.reshape(dec_batch * dec_seq, D_MODEL)
    lat_row = lambda i: i // (dec_seq // tm_l)
    ql, kl, vl, ul = _stage1_call(xl, mod3, lat_row, s1_weights, _rope_tables(dec_seq), dec_seq, tm_l, False)
    kcache, vcache = _kv_up_call(cache_ckv[:, 0].reshape(dec_batch * past, KV_LORA_RANK),
                                 cache_krope[:, 0].reshape(dec_batch * past, QK_ROPE_DIM), wk, wv, 512)
    attn_l = _attn_call(ql.reshape(dec_batch, dec_seq, QK_WIDTH), kl.reshape(dec_batch, dec_seq, QK_WIDTH),
                        vl.reshape(dec_batch, dec_seq, MLA_WIDTH),
                        (kcache.reshape(dec_batch, past, QK_WIDTH), vcache.reshape(dec_batch, past, MLA_WIDTH)),
                        512, 2)
    y_sample = _post_call(xl, attn_l.reshape(dec_batch * dec_seq, MLA_WIDTH), ul, mod3, lat_row,
                          post_weights, dec_seq, tm_l)

    return (y_prompt.reshape(batch, seq, D_MODEL), y_sample.reshape(dec_batch, dec_seq, D_MODEL),
            ckv_new.reshape(batch, 1, seq, KV_LORA_RANK), kr_new.reshape(batch, 1, seq, QK_ROPE_DIM))
```

```python
import functools
import math

import jax
import jax.numpy as jnp
from jax import lax
from jax.experimental import pallas as pl
from jax.experimental.pallas import tpu as pltpu

D_MODEL = 1024
N_HEADS = 8
QK_NOPE_DIM = 64
QK_ROPE_DIM = 32
V_HEAD_DIM = 64
QK_DIM = QK_NOPE_DIM + QK_ROPE_DIM
Q_LORA_RANK = 256
KV_LORA_RANK = 128
MLA_WIDTH = N_HEADS * V_HEAD_DIM
POOL_WIDTH = D_MODEL - MLA_WIDTH
POOL_WINDOWS = (2, 4, 8, 16)
POOL_GROUP = POOL_WIDTH // len(POOL_WINDOWS)
D_FF = 4 * D_MODEL
GRID_W = 64
ROPE_THETA = 10000.0
AXIS_ROPE = QK_ROPE_DIM // 2
N_FREQ = AXIS_ROPE // 2
EPS = 1e-6
ATTN_SCALE = 1.0 / math.sqrt(QK_DIM)
Q_SCALE = ATTN_SCALE * math.log2(math.e)

LANES = 128
SUBLANES = 8
HEAD_SLAB = LANES
QK_WIDTH = N_HEADS * HEAD_SLAB
MOD_ROWS = 16
POOL_HALO = max(POOL_WINDOWS) // 2
KR_OFF = Q_LORA_RANK + KV_LORA_RANK
U_OFF = KR_OFF + LANES
PROJ_COLS = U_OFF + POOL_WIDTH
VMEM_LIMIT = 56 * 1024 * 1024

F32 = jnp.float32
BF16 = jnp.bfloat16


def _dot(a, b):
    return jnp.dot(a, b, preferred_element_type=F32)


def _rms(x, g):
    return x * lax.rsqrt(jnp.mean(x * x, axis=-1, keepdims=True) + EPS) * g


def _mod_kernel(c_ref, w_ref, b_ref, o_ref):
    c = c_ref[...]
    s = (c * jax.nn.sigmoid(c)).astype(BF16)
    o_ref[...] = _dot(s, w_ref[...].astype(BF16)) + b_ref[...]


def _mod_call(c_all, w_ada, b_ada):
    tn = 1024
    n = w_ada.shape[1]
    return pl.pallas_call(
        _mod_kernel,
        grid=(n // tn,),
        in_specs=[
            pl.BlockSpec((MOD_ROWS, D_MODEL), lambda j: (0, 0)),
            pl.BlockSpec((D_MODEL, tn), lambda j: (0, j)),
            pl.BlockSpec((1, tn), lambda j: (0, j)),
        ],
        out_specs=pl.BlockSpec((MOD_ROWS, tn), lambda j: (0, j)),
        out_shape=jax.ShapeDtypeStruct((MOD_ROWS, n), F32),
        compiler_params=pltpu.CompilerParams(dimension_semantics=("arbitrary",)),
        name="mod",
    )(c_all, w_ada, b_ada)


def _rope_slab(slab, c, s_up, s_dn):
    return (slab * c + pltpu.roll(slab, N_FREQ, 1) * s_up
            + pltpu.roll(slab, LANES - N_FREQ, 1) * s_dn)


def _stage1_kernel(*refs, use_rope, emit_cache):
    it = iter(refs)
    x_ref, mod_ref, norm1_ref, w_in_ref, qn_ref, wq_ref, kvn_ref, wk_ref, wv_ref = (next(it) for _ in range(9))
    if use_rope:
        cq_ref, uq_ref, dq_ref, ck_ref, uk_ref, dk_ref = (next(it) for _ in range(6))
    q_ref, k_ref, v_ref, u_ref = (next(it) for _ in range(4))
    if emit_cache:
        ckv_ref, kr_ref = (next(it) for _ in range(2))

    x = x_ref[...]
    gain = norm1_ref[...] * (1.0 + mod_ref[0, 1:2, :])
    h = _rms(x, gain) + mod_ref[0, 0:1, :]
    proj = _dot(h.astype(BF16), w_in_ref[...])

    u_ref[...] = proj[:, U_OFF:]

    qa = _rms(proj[:, :Q_LORA_RANK], qn_ref[...]).astype(BF16)
    q = _dot(qa, wq_ref[...])
    for hd in range(N_HEADS):
        slab = q[:, hd * HEAD_SLAB:(hd + 1) * HEAD_SLAB]
        if use_rope:
            slab = _rope_slab(slab, cq_ref[...], uq_ref[...], dq_ref[...])
        else:
            slab = slab * Q_SCALE
        q_ref[:, hd * HEAD_SLAB:(hd + 1) * HEAD_SLAB] = slab.astype(BF16)

    ckv = _rms(proj[:, Q_LORA_RANK:KR_OFF], kvn_ref[...])
    kr = proj[:, KR_OFF:U_OFF]
    if emit_cache:
        ckv_ref[...] = ckv
        kr_ref[...] = kr[:, :QK_ROPE_DIM]
    if use_rope:
        kr = _rope_slab(kr, ck_ref[...], uk_ref[...], dk_ref[...])
    ckv16 = ckv.astype(BF16)
    kin = jnp.concatenate([ckv16, kr.astype(BF16)], axis=1)
    k_ref[...] = _dot(kin, wk_ref[...]).astype(BF16)
    v_ref[...] = _dot(ckv16, wv_ref[...]).astype(BF16)


def _const_spec(shape):
    nd = len(shape)
    return pl.BlockSpec(shape, lambda i: (0,) * nd)


def _stage1_call(x2d, mod3, mod_row_fn, weights, rope_tabs, seq, tm, emit_cache):
    n_tok = x2d.shape[0]
    use_rope = rope_tabs is not None
    norm1, w_in, q_norm, wq, kv_norm, wk, wv = weights
    tiles_per_seq = seq // tm
    in_specs = [
        pl.BlockSpec((tm, D_MODEL), lambda i: (i, 0)),
        pl.BlockSpec((1, 6, D_MODEL), lambda i: (mod_row_fn(i), 0, 0)),
        _const_spec(norm1.shape), _const_spec(w_in.shape), _const_spec(q_norm.shape),
        _const_spec(wq.shape), _const_spec(kv_norm.shape), _const_spec(wk.shape), _const_spec(wv.shape),
    ]
    args = [x2d, mod3, norm1, w_in, q_norm, wq, kv_norm, wk, wv]
    if use_rope:
        in_specs += [pl.BlockSpec((tm, LANES), lambda i: (i % tiles_per_seq, 0))] * 6
        args += list(rope_tabs)
    out_specs = [
        pl.BlockSpec((tm, QK_WIDTH), lambda i: (i, 0)),
        pl.BlockSpec((tm, QK_WIDTH), lambda i: (i, 0)),
        pl.BlockSpec((tm, MLA_WIDTH), lambda i: (i, 0)),
        pl.BlockSpec((tm, POOL_WIDTH), lambda i: (i, 0)),
    ]
    out_shape = [
        jax.ShapeDtypeStruct((n_tok, QK_WIDTH), BF16),
        jax.ShapeDtypeStruct((n_tok, QK_WIDTH), BF16),
        jax.ShapeDtypeStruct((n_tok, MLA_WIDTH), BF16),
        jax.ShapeDtypeStruct((n_tok, POOL_WIDTH), F32),
    ]
    if emit_cache:
        out_specs += [pl.BlockSpec((tm, KV_LORA_RANK), lambda i: (i, 0)),
                      pl.BlockSpec((tm, QK_ROPE_DIM), lambda i: (i, 0))]
        out_shape += [jax.ShapeDtypeStruct((n_tok, KV_LORA_RANK), F32),
                      jax.ShapeDtypeStruct((n_tok, QK_ROPE_DIM), F32)]
    return pl.pallas_call(
        functools.partial(_stage1_kernel, use_rope=use_rope, emit_cache=emit_cache),
        grid=(n_tok // tm,),
        in_specs=in_specs,
        out_specs=out_specs,
        out_shape=out_shape,
        compiler_params=pltpu.CompilerParams(dimension_semantics=("parallel",),
                                             vmem_limit_bytes=VMEM_LIMIT),
        name="stage1_rope" if use_rope else "stage1_ctx",
    )(*args)


def _kv_up_kernel(ckv_ref, kr_ref, wk_ref, wv_ref, k_ref, v_ref):
    ckv16 = ckv_ref[...].astype(BF16)
    kr = kr_ref[...].astype(BF16)
    pad = jnp.zeros((kr.shape[0], LANES - QK_ROPE_DIM), BF16)
    kin = jnp.concatenate([ckv16, kr, pad], axis=1)
    k_ref[...] = _dot(kin, wk_ref[...]).astype(BF16)
    v_ref[...] = _dot(ckv16, wv_ref[...]).astype(BF16)


def _kv_up_call(ckv2d, kr2d, wk, wv, tm):
    n_tok = ckv2d.shape[0]
    return pl.pallas_call(
        _kv_up_kernel,
        grid=(n_tok // tm,),
        in_specs=[
            pl.BlockSpec((tm, KV_LORA_RANK), lambda i: (i, 0)),
            pl.BlockSpec((tm, QK_ROPE_DIM), lambda i: (i, 0)),
            _const_spec(wk.shape), _const_spec(wv.shape),
        ],
        out_specs=[pl.BlockSpec((tm, QK_WIDTH), lambda i: (i, 0)),
                   pl.BlockSpec((tm, MLA_WIDTH), lambda i: (i, 0))],
        out_shape=[jax.ShapeDtypeStruct((n_tok, QK_WIDTH), BF16),
                   jax.ShapeDtypeStruct((n_tok, MLA_WIDTH), BF16)],
        compiler_params=pltpu.CompilerParams(dimension_semantics=("parallel",)),
        name="kv_up",
    )(ckv2d, kr2d, wk, wv)


def _qk(qh, kh):
    return lax.dot_general(qh, kh, (((1,), (1,)), ((), ())), preferred_element_type=F32)


def _attn_kernel(*refs, heads, with_cache):
    if with_cache:
        q_ref, k_ref, v_ref, kc_ref, vc_ref, o_ref = refs
    else:
        q_ref, k_ref, v_ref, o_ref = refs
    def scores(hd):
        qs = slice(hd * HEAD_SLAB, (hd + 1) * HEAD_SLAB)
        qh = q_ref[0, :, qs]
        s = _qk(qh, k_ref[0, :, qs])
        sc = _qk(qh, kc_ref[0, :, qs]) if with_cache else None
        return s, sc

    nxt = scores(0)
    for hd in range(heads):
        s, sc = nxt
        if hd + 1 < heads:
            nxt = scores(hd + 1)
        vs = slice(hd * V_HEAD_DIM, (hd + 1) * V_HEAD_DIM)
        m = jnp.max(s, axis=-1, keepdims=True)
        if with_cache:
            m = jnp.maximum(m, jnp.max(sc, axis=-1, keepdims=True))
        p = jnp.exp2(s - m)
        l = jnp.sum(p, axis=-1, keepdims=True)
        o = _dot(p.astype(BF16), v_ref[0, :, vs])
        if with_cache:
            pc = jnp.exp2(sc - m)
            l = l + jnp.sum(pc, axis=-1, keepdims=True)
            o = o + _dot(pc.astype(BF16), vc_ref[0, :, vs])
        o_ref[0, :, vs] = (o * (1.0 / l)).astype(BF16)


def _attn_call(q3, k3, v3, cache, tq, heads):
    b, s, _ = q3.shape
    nk = k3.shape[1]
    groups = N_HEADS // heads
    qw, vw = heads * HEAD_SLAB, heads * V_HEAD_DIM
    in_specs = [
        pl.BlockSpec((1, tq, qw), lambda bi, g, i: (bi, i, g)),
        pl.BlockSpec((1, nk, qw), lambda bi, g, i: (bi, 0, g)),
        pl.BlockSpec((1, nk, vw), lambda bi, g, i: (bi, 0, g)),
    ]
    args = [q3, k3, v3]
    if cache is not None:
        kc3, vc3 = cache
        nc = kc3.shape[1]
        in_specs += [pl.BlockSpec((1, nc, qw), lambda bi, g, i: (bi, 0, g)),
                     pl.BlockSpec((1, nc, vw), lambda bi, g, i: (bi, 0, g))]
        args += [kc3, vc3]
    return pl.pallas_call(
        functools.partial(_attn_kernel, heads=heads, with_cache=cache is not None),
        grid=(b, groups, s // tq),
        in_specs=in_specs,
        out_specs=pl.BlockSpec((1, tq, vw), lambda bi, g, i: (bi, i, g)),
        out_shape=jax.ShapeDtypeStruct((b, s, MLA_WIDTH), BF16),
        compiler_params=pltpu.CompilerParams(
            dimension_semantics=("parallel", "parallel", "parallel"),
            vmem_limit_bytes=VMEM_LIMIT),
        name="attn_lat" if cache is not None else "attn_ctx",
    )(*args)


def _post_kernel(x_ref, a_ref, u_ref, up_ref, un_ref, mod_ref, wp_ref, ps_ref, wo_ref,
                 norm2_ref, w1_ref, w2_ref, fn_ref, o_ref, e_ref, *, tm, seq, ff_chunk):
    i = pl.program_id(0)
    pos0 = (i * tm) % seq
    u = u_ref[...]
    e_ref[0:POOL_HALO, :] = jnp.where(pos0 > 0, up_ref[...], 0.0)
    e_ref[POOL_HALO:POOL_HALO + tm, :] = u
    e_ref[POOL_HALO + tm:, :] = jnp.where(pos0 + tm < seq, un_ref[...], 0.0)

    t = pos0 + lax.broadcasted_iota(jnp.int32, (tm, 1), 0)
    mixed = []
    for g, w in enumerate(POOL_WINDOWS):
        half = w // 2
        cols = slice(g * POOL_GROUP, (g + 1) * POOL_GROUP)
        acc = e_ref[POOL_HALO - half:POOL_HALO - half + tm, cols]
        for d in range(-half + 1, half):
            acc = acc + e_ref[POOL_HALO + d:POOL_HALO + d + tm, cols]
        cnt = (jnp.minimum(t + half, seq) - jnp.maximum(t - half, 0)).astype(F32)
        pooled = acc * (1.0 / cnt) - u[:, cols]
        mg = _dot(pooled.astype(BF16), wp_ref[g]) * ps_ref[:, cols]
        mixed.append(mg.astype(BF16))
    cat = jnp.concatenate([a_ref[...]] + mixed, axis=1)
    y = x_ref[...] + mod_ref[0, 2:3, :] * _dot(cat, wo_ref[...])

    gain = norm2_ref[...] * (1.0 + mod_ref[0, 4:5, :])
    h2 = (_rms(y, gain) + mod_ref[0, 3:4, :]).astype(BF16)
    acc = jnp.zeros((tm, D_MODEL), F32)
    for c in range(D_FF // ff_chunk):
        a = _dot(h2, w1_ref[:, c * ff_chunk:(c + 1) * ff_chunk])
        a = jnp.square(jnp.maximum(a, 0.0)).astype(BF16)
        acc = acc + _dot(a, w2_ref[c * ff_chunk:(c + 1) * ff_chunk, :])
    x2 = y + mod_ref[0, 5:6, :] * acc
    o_ref[...] = _rms(x2, fn_ref[...])


def _resident_spec(shape):
    nd = len(shape)
    return pl.BlockSpec(shape, lambda i: (0,) * nd, pipeline_mode=pl.Buffered(1))


def _post_call(x2d, attn2d, u2d, mod3, mod_row_fn, weights, seq, tm):
    n_tok = x2d.shape[0]
    w_pool, pool_scale, w_o, norm2, w1, w2, final_norm = weights
    halo_blocks = n_tok // POOL_HALO
    blocks_per_tile = tm // POOL_HALO
    in_specs = [
        pl.BlockSpec((tm, D_MODEL), lambda i: (i, 0)),
        pl.BlockSpec((tm, MLA_WIDTH), lambda i: (i, 0)),
        pl.BlockSpec((tm, POOL_WIDTH), lambda i: (i, 0)),
        pl.BlockSpec((POOL_HALO, POOL_WIDTH), lambda i: (jnp.maximum(i * blocks_per_tile - 1, 0), 0)),
        pl.BlockSpec((POOL_HALO, POOL_WIDTH),
                     lambda i: (jnp.minimum((i + 1) * blocks_per_tile, halo_blocks - 1), 0)),
        pl.BlockSpec((1, 6, D_MODEL), lambda i: (mod_row_fn(i), 0, 0)),
        _resident_spec(w_pool.shape), _resident_spec(pool_scale.shape), _resident_spec(w_o.shape),
        _resident_spec(norm2.shape), _resident_spec(w1.shape), _resident_spec(w2.shape),
        _resident_spec(final_norm.shape),
    ]
    return pl.pallas_call(
        functools.partial(_post_kernel, tm=tm, seq=seq, ff_chunk=1024),
        grid=(n_tok // tm,),
        in_specs=in_specs,
        out_specs=pl.BlockSpec((tm, D_MODEL), lambda i: (i, 0)),
        out_shape=jax.ShapeDtypeStruct((n_tok, D_MODEL), F32),
        scratch_shapes=[pltpu.VMEM((tm + 2 * POOL_HALO, POOL_WIDTH), F32)],
        compiler_params=pltpu.CompilerParams(dimension_semantics=("parallel",),
                                             vmem_limit_bytes=VMEM_LIMIT),
        name="post",
    )(x2d, attn2d, u2d, u2d, u2d, mod3, w_pool, pool_scale, w_o, norm2, w1, w2, final_norm)


def _pack_w_in(w_in):
    qa = w_in[:, :Q_LORA_RANK + KV_LORA_RANK]
    kr = w_in[:, KR_OFF:KR_OFF + QK_ROPE_DIM]
    u = w_in[:, KR_OFF + QK_ROPE_DIM:]
    pad = jnp.zeros((D_MODEL, LANES - QK_ROPE_DIM), w_in.dtype)
    return jnp.concatenate([qa, kr, pad, u], axis=1).astype(BF16)


def _pack_wq(w_qb):
    w = w_qb.reshape(Q_LORA_RANK, N_HEADS, QK_DIM)
    w = jnp.pad(w, ((0, 0), (0, 0), (0, HEAD_SLAB - QK_DIM)))
    return w.reshape(Q_LORA_RANK, QK_WIDTH).astype(BF16)


def _pack_wkv(w_kvb):
    w = w_kvb.reshape(KV_LORA_RANK, N_HEADS, QK_NOPE_DIM + V_HEAD_DIM)
    wk_nope = jnp.pad(w[:, :, :QK_NOPE_DIM], ((0, 0), (0, 0), (0, HEAD_SLAB - QK_NOPE_DIM)))
    sel = jnp.pad(jnp.eye(QK_ROPE_DIM, dtype=w_kvb.dtype),
                  ((0, 0), (QK_NOPE_DIM, HEAD_SLAB - QK_DIM)))
    sel = jnp.broadcast_to(sel[:, None, :], (QK_ROPE_DIM, N_HEADS, HEAD_SLAB))
    zero = jnp.zeros((LANES - QK_ROPE_DIM, N_HEADS, HEAD_SLAB), w_kvb.dtype)
    wk = jnp.concatenate([wk_nope, sel, zero], axis=0).reshape(KV_LORA_RANK + LANES, QK_WIDTH)
    wv = w[:, :, QK_NOPE_DIM:].reshape(KV_LORA_RANK, MLA_WIDTH)
    return wk.astype(BF16), wv.astype(BF16)


def _rope_tables(n_tokens):
    rows = n_tokens // GRID_W
    row = jnp.repeat(jnp.arange(rows, dtype=F32), GRID_W)
    col = jnp.tile(jnp.arange(GRID_W, dtype=F32), rows)
    inv_freq = 1.0 / (ROPE_THETA ** (jnp.arange(N_FREQ, dtype=F32) * 2.0 / AXIS_ROPE))
    cos_r, sin_r = jnp.cos(row[:, None] * inv_freq), jnp.sin(row[:, None] * inv_freq)
    cos_c, sin_c = jnp.cos(col[:, None] * inv_freq), jnp.sin(col[:, None] * inv_freq)
    z = jnp.zeros_like(cos_r)
    c32 = jnp.concatenate([cos_r, cos_r, cos_c, cos_c], axis=1)
    up32 = jnp.concatenate([z, sin_r, z, sin_c], axis=1)
    dn32 = jnp.concatenate([-sin_r, z, -sin_c, z], axis=1)

    def place(t32, off, fill):
        left = jnp.full((n_tokens, off), fill, F32)
        right = jnp.zeros((n_tokens, LANES - off - QK_ROPE_DIM), F32)
        return jnp.concatenate([left, t32, right], axis=1)

    q_tabs = (place(c32 * Q_SCALE, QK_NOPE_DIM, Q_SCALE),
              place(up32 * Q_SCALE, QK_NOPE_DIM, 0.0),
              place(dn32 * Q_SCALE, QK_NOPE_DIM, 0.0))
    k_tabs = (place(c32, 0, 0.0), place(up32, 0, 0.0), place(dn32, 0, 0.0))
    return q_tabs + k_tabs


def kernel(x_prompt, x_sample, cache_ckv, cache_krope, c, c_ctx, w_ada, b_ada, norm1, w_in, q_norm,
           w_qb, kv_norm, w_kvb, w_pool, pool_scale, w_o, norm2, w1, w2, final_norm):
    batch, seq, _ = x_prompt.shape
    dec_batch, dec_seq, _ = x_sample.shape
    depth = w_in.shape[0]
    past = cache_ckv.shape[2]
    assert depth == 1, "single trunk layer"
    assert dec_batch + 1 <= MOD_ROWS

    c_all = jnp.concatenate(
        [c, c_ctx[None, :], jnp.zeros((MOD_ROWS - dec_batch - 1, D_MODEL), F32)], axis=0)
    mod3 = _mod_call(c_all, w_ada[0], b_ada[0][None, :]).reshape(MOD_ROWS, 6, D_MODEL)

    wk, wv = _pack_wkv(w_kvb[0])
    s1_weights = (norm1[0][None, :], _pack_w_in(w_in[0]), q_norm[0][None, :], _pack_wq(w_qb[0]),
                  kv_norm[0][None, :], wk, wv)
    post_weights = (w_pool[0].astype(BF16), pool_scale[0][None, :], w_o[0].astype(BF16),
                    norm2[0][None, :], w1[0].astype(BF16), w2[0].astype(BF16), final_norm[None, :])

    tm_c = 256
    xc = x_prompt.reshape(batch * seq, D_MODEL)
    ctx_row = lambda i: dec_batch
    qc, kc, vc, uc, ckv_new, kr_new = _stage1_call(xc, mod3, ctx_row, s1_weights, None, seq, tm_c, True)
    attn_c = _attn_call(qc.reshape(batch, seq, QK_WIDTH), kc.reshape(batch, seq, QK_WIDTH),
                        vc.reshape(batch, seq, MLA_WIDTH), None, seq, N_HEADS)
    y_prompt = _post_call(xc, attn_c.reshape(batch * seq, MLA_WIDTH), uc, mod3, ctx_row,
                          post_weights, seq, tm_c)

    tm_l = 512
    xl = x_sample.reshape(dec_batch * dec_seq, D_MODEL)
    lat_row = lambda i: i // (dec_seq // tm_l)
    ql, kl, vl, ul = _stage1_call(xl, mod3, lat_row, s1_weights, _rope_tables(dec_seq), dec_seq, tm_l, False)
    kcache, vcache = _kv_up_call(cache_ckv[:, 0].reshape(dec_batch * past, KV_LORA_RANK),
                                 cache_krope[:, 0].reshape(dec_batch * past, QK_ROPE_DIM), wk, wv, 512)
    attn_l = _attn_call(ql.reshape(dec_batch, dec_seq, QK_WIDTH), kl.reshape(dec_batch, dec_seq, QK_WIDTH),
                        vl.reshape(dec_batch, dec_seq, MLA_WIDTH),
                        (kcache.reshape(dec_batch, past, QK_WIDTH), vcache.reshape(dec_batch, past, MLA_WIDTH)),
                        512, N_HEADS)
    y_sample = _post_call(xl, attn_l.reshape(dec_batch * dec_seq, MLA_WIDTH), ul, mod3, lat_row,
                          post_weights, dec_seq, tm_l)

    return (y_prompt.reshape(batch, seq, D_MODEL), y_sample.reshape(dec_batch, dec_seq, D_MODEL),
            ckv_new.reshape(batch, 1, seq, KV_LORA_RANK), kr_new.reshape(batch, 1, seq, QK_ROPE_DIM))
```

```python
import functools
import math

import jax
import jax.numpy as jnp
import numpy as np
from jax import lax
from jax.experimental import pallas as pl
from jax.experimental.pallas import tpu as pltpu

D_MODEL = 1024
N_HEADS = 8
QK_NOPE_DIM = 64
QK_ROPE_DIM = 32
V_HEAD_DIM = 64
QK_DIM = QK_NOPE_DIM + QK_ROPE_DIM
Q_LORA_RANK = 256
KV_LORA_RANK = 128
MLA_WIDTH = N_HEADS * V_HEAD_DIM
POOL_WIDTH = D_MODEL - MLA_WIDTH
POOL_WINDOWS = (2, 4, 8, 16)
POOL_GROUP = POOL_WIDTH // len(POOL_WINDOWS)
D_FF = 4 * D_MODEL
GRID_W = 64
ROPE_THETA = 10000.0
AXIS_ROPE = QK_ROPE_DIM // 2
N_FREQ = AXIS_ROPE // 2
EPS = 1e-6
ATTN_SCALE = 1.0 / math.sqrt(QK_DIM)
Q_SCALE = ATTN_SCALE * math.log2(math.e)

LANES = 128
SUBLANES = 8
HEAD_SLAB = LANES
QK_WIDTH = N_HEADS * HEAD_SLAB
MOD_ROWS = 16
POOL_HALO = max(POOL_WINDOWS) // 2
assert POOL_HALO == SUBLANES
KR_OFF = Q_LORA_RANK + KV_LORA_RANK
U_OFF = KR_OFF + LANES
PROJ_COLS = U_OFF + POOL_WIDTH
VMEM_LIMIT = 56 * 1024 * 1024
POST_TILE = 512
POST_SUB = 256

F32 = jnp.float32
BF16 = jnp.bfloat16


def _dot(a, b):
    return jnp.dot(a, b, preferred_element_type=F32)


def _rms(x, g):
    return x * lax.rsqrt(jnp.mean(x * x, axis=-1, keepdims=True) + EPS) * g


def _const_spec(shape):
    nd = len(shape)
    return pl.BlockSpec(shape, lambda i: (0,) * nd)


def _resident_spec(shape):
    nd = len(shape)
    return pl.BlockSpec(shape, lambda i: (0,) * nd, pipeline_mode=pl.Buffered(1))


def _mod_kernel(c_ref, w_ref, b_ref, o_ref):
    c = c_ref[...]
    s = (c * jax.nn.sigmoid(c)).astype(BF16)
    o_ref[...] = _dot(s, w_ref[...].astype(BF16)) + b_ref[...]


def _mod_call(c_all, w_ada, b_ada):
    tn = 1024
    n = w_ada.shape[1]
    return pl.pallas_call(
        _mod_kernel,
        grid=(n // tn,),
        in_specs=[
            pl.BlockSpec((MOD_ROWS, D_MODEL), lambda j: (0, 0)),
            pl.BlockSpec((D_MODEL, tn), lambda j: (0, j)),
            pl.BlockSpec((1, tn), lambda j: (0, j)),
        ],
        out_specs=pl.BlockSpec((MOD_ROWS, tn), lambda j: (0, j)),
        out_shape=jax.ShapeDtypeStruct((MOD_ROWS, n), F32),
        compiler_params=pltpu.CompilerParams(dimension_semantics=("arbitrary",)),
        name="mod",
    )(c_all, w_ada, b_ada)


def _rope_slab(slab, c, s_up, s_dn):
    return (slab * c + pltpu.roll(slab, N_FREQ, 1) * s_up
            + pltpu.roll(slab, LANES - N_FREQ, 1) * s_dn)


def _inv_counts(w, tm, at_seq_start, at_seq_end):
    half = w // 2
    r = lax.broadcasted_iota(jnp.int32, (SUBLANES, LANES), 0)
    first = jnp.where(at_seq_start, (r + half) - jnp.maximum(r - half, 0), w)
    last = jnp.where(at_seq_end, jnp.minimum(half, SUBLANES - r) + half, w)
    mid = jnp.full((tm - 2 * SUBLANES, LANES), 1.0 / w, F32)
    return jnp.concatenate([1.0 / first.astype(F32), mid, 1.0 / last.astype(F32)], axis=0)


def _pool_mix(e_ref, c2_ref, c4_ref, wp_ref, ps_ref, m_ref, tm, at_seq_start, at_seq_end):
    rows = tm + 2 * POOL_HALO
    g = [slice(i * POOL_GROUP, (i + 1) * POOL_GROUP) for i in range(len(POOL_WINDOWS))]
    wide = slice(POOL_GROUP, POOL_WIDTH)
    wider = slice(2 * POOL_GROUP, POOL_WIDTH)
    zeros = jnp.zeros((SUBLANES, POOL_WIDTH), F32)
    c2_ref[rows:, :] = zeros
    c4_ref[rows:, :] = zeros
    c2_ref[0:rows, wide] = e_ref[0:rows, wide] + e_ref[1:rows + 1, wide]
    c4_ref[0:rows, wider] = c2_ref[0:rows, wider] + c2_ref[2:rows + 2, wider]
    c8 = c4_ref[0:rows, g[3]] + c4_ref[4:rows + 4, g[3]]
    sums = [
        e_ref[7:7 + tm, g[0]] + e_ref[8:8 + tm, g[0]],
        c2_ref[6:6 + tm, g[1]] + c2_ref[8:8 + tm, g[1]],
        c4_ref[4:4 + tm, g[2]] + c4_ref[8:8 + tm, g[2]],
        c8[0:tm] + c8[8:8 + tm],
    ]
    for i, w in enumerate(POOL_WINDOWS):
        inv = _inv_counts(w, tm, at_seq_start, at_seq_end)
        pooled = sums[i] * inv - e_ref[POOL_HALO:POOL_HALO + tm, g[i]]
        mg = _dot(pooled.astype(BF16), wp_ref[i]) * ps_ref[:, g[i]]
        m_ref[:, g[i]] = mg.astype(BF16)


def _stage1_kernel(*refs, use_rope, halo, emit_cache, tm, seq):
    it = iter(refs)
    x_ref = next(it)
    if halo:
        xp_ref, xn_ref = next(it), next(it)
    (mod_ref, norm1_ref, w_in_ref, qn_ref, wq_ref, kvn_ref, wk_ref, wv_ref, wp_ref,
     ps_ref) = (next(it) for _ in range(10))
    if use_rope:
        q_tabs = [next(it) for _ in range(2)]
        k_tabs = [next(it) for _ in range(3)]
    q_ref, k_ref, v_ref, m_ref = (next(it) for _ in range(4))
    if emit_cache:
        ckv_ref, kr_ref = next(it), next(it)
    e_ref, c2_ref, c4_ref = next(it), next(it), next(it)

    pos0 = (pl.program_id(0) * tm) % seq
    at_seq_start = pos0 == 0
    at_seq_end = pos0 + tm == seq

    if halo:
        x = jnp.concatenate([xp_ref[...], x_ref[...], xn_ref[...]], axis=0)
    else:
        x = x_ref[...]
    gain = norm1_ref[...] * (1.0 + mod_ref[0, 1:2, :])
    h = _rms(x, gain) + mod_ref[0, 0:1, :]
    proj = _dot(h.astype(BF16), w_in_ref[...])

    u = proj[:, U_OFF:]
    zeros = jnp.zeros((POOL_HALO, POOL_WIDTH), F32)
    if halo:
        e_ref[0:POOL_HALO, :] = jnp.where(at_seq_start, 0.0, u[0:POOL_HALO])
        e_ref[POOL_HALO:POOL_HALO + tm, :] = u[POOL_HALO:POOL_HALO + tm]
        e_ref[POOL_HALO + tm:2 * POOL_HALO + tm, :] = jnp.where(at_seq_end, 0.0, u[POOL_HALO + tm:])
        proj = proj[POOL_HALO:POOL_HALO + tm]
    else:
        e_ref[0:POOL_HALO, :] = zeros
        e_ref[POOL_HALO:POOL_HALO + tm, :] = u
        e_ref[POOL_HALO + tm:2 * POOL_HALO + tm, :] = zeros
    e_ref[2 * POOL_HALO + tm:, :] = zeros
    _pool_mix(e_ref, c2_ref, c4_ref, wp_ref, ps_ref, m_ref, tm, at_seq_start, at_seq_end)

    qa = _rms(proj[:, :Q_LORA_RANK], qn_ref[...]).astype(BF16)
    q = _dot(qa, wq_ref[...])
    if use_rope:
        pos = pl.multiple_of(pos0, tm)
        cq, sq = (r[pl.ds(pos, tm), :] for r in q_tabs)
    for hd in range(N_HEADS):
        slab = q[:, hd * HEAD_SLAB:(hd + 1) * HEAD_SLAB]
        if use_rope:
            swapped = q[:, QK_WIDTH + hd * HEAD_SLAB:QK_WIDTH + (hd + 1) * HEAD_SLAB]
            slab = slab * cq + swapped * sq
        else:
            slab = slab * Q_SCALE
        q_ref[:, hd * HEAD_SLAB:(hd + 1) * HEAD_SLAB] = slab.astype(BF16)

    ckv = _rms(proj[:, Q_LORA_RANK:KR_OFF], kvn_ref[...])
    kr = proj[:, KR_OFF:U_OFF]
    if emit_cache:
        ckv_ref[...] = ckv
        kr_ref[...] = kr[:, :QK_ROPE_DIM]
    if use_rope:
        kr = _rope_slab(kr, *(r[pl.ds(pos, tm), :] for r in k_tabs))
    ckv16 = ckv.astype(BF16)
    kin = jnp.concatenate([ckv16, kr.astype(BF16)], axis=1)
    k_ref[...] = _dot(kin, wk_ref[...]).astype(BF16)
    v_ref[...] = _dot(ckv16, wv_ref[...]).astype(BF16)


def _stage1_call(x2d, mod3, mod_row_fn, weights, rope_tabs, seq, tm, emit_cache):
    n_tok = x2d.shape[0]
    assert seq % tm == 0 and seq >= 2 * SUBLANES and tm >= 2 * SUBLANES
    use_rope = rope_tabs is not None
    halo = seq > tm
    halo_blocks = n_tok // POOL_HALO
    per_tile = tm // POOL_HALO
    in_specs = [pl.BlockSpec((tm, D_MODEL), lambda i: (i, 0))]
    args = [x2d]
    if halo:
        in_specs += [
            pl.BlockSpec((POOL_HALO, D_MODEL), lambda i: (jnp.maximum(i * per_tile - 1, 0), 0)),
            pl.BlockSpec((POOL_HALO, D_MODEL),
                         lambda i: (jnp.minimum((i + 1) * per_tile, halo_blocks - 1), 0)),
        ]
        args += [x2d, x2d]
    in_specs += [pl.BlockSpec((1, 6, D_MODEL), lambda i: (mod_row_fn(i), 0, 0))]
    in_specs += [_const_spec(w.shape) for w in weights]
    args += [mod3] + list(weights)
    if use_rope:
        in_specs += [_resident_spec((seq, LANES))] * len(rope_tabs)
        args += list(rope_tabs)
    out_specs = [
        pl.BlockSpec((tm, QK_WIDTH), lambda i: (i, 0)),
        pl.BlockSpec((tm, QK_WIDTH), lambda i: (i, 0)),
        pl.BlockSpec((tm, MLA_WIDTH), lambda i: (i, 0)),
        pl.BlockSpec((tm, POOL_WIDTH), lambda i: (i, 0)),
    ]
    out_shape = [
        jax.ShapeDtypeStruct((n_tok, QK_WIDTH), BF16),
        jax.ShapeDtypeStruct((n_tok, QK_WIDTH), BF16),
        jax.ShapeDtypeStruct((n_tok, MLA_WIDTH), BF16),
        jax.ShapeDtypeStruct((n_tok, POOL_WIDTH), BF16),
    ]
    if emit_cache:
        out_specs += [pl.BlockSpec((tm, KV_LORA_RANK), lambda i: (i, 0)),
                      pl.BlockSpec((tm, QK_ROPE_DIM), lambda i: (i, 0))]
        out_shape += [jax.ShapeDtypeStruct((n_tok, KV_LORA_RANK), F32),
                      jax.ShapeDtypeStruct((n_tok, QK_ROPE_DIM), F32)]
    stencil_rows = tm + 3 * POOL_HALO
    return pl.pallas_call(
        functools.partial(_stage1_kernel, use_rope=use_rope, halo=halo, emit_cache=emit_cache,
                          tm=tm, seq=seq),
        grid=(n_tok // tm,),
        in_specs=in_specs,
        out_specs=out_specs,
        out_shape=out_shape,
        scratch_shapes=[pltpu.VMEM((stencil_rows, POOL_WIDTH), F32)] * 3,
        compiler_params=pltpu.CompilerParams(dimension_semantics=("parallel",),
                                             vmem_limit_bytes=VMEM_LIMIT),
        name="stage1_rope" if use_rope else "stage1_ctx",
    )(*args)


def _kv_up_kernel(ckv_ref, kr_ref, wk_ref, wv_ref, k_ref, v_ref):
    ckv16 = ckv_ref[...].astype(BF16)
    kr = kr_ref[...].astype(BF16)
    pad = jnp.zeros((kr.shape[0], LANES - QK_ROPE_DIM), BF16)
    kin = jnp.concatenate([ckv16, kr, pad], axis=1)
    k_ref[...] = _dot(kin, wk_ref[...]).astype(BF16)
    v_ref[...] = _dot(ckv16, wv_ref[...]).astype(BF16)


def _kv_up_call(ckv2d, kr2d, wk, wv, tm):
    n_tok = ckv2d.shape[0]
    return pl.pallas_call(
        _kv_up_kernel,
        grid=(n_tok // tm,),
        in_specs=[
            pl.BlockSpec((tm, KV_LORA_RANK), lambda i: (i, 0)),
            pl.BlockSpec((tm, QK_ROPE_DIM), lambda i: (i, 0)),
            _const_spec(wk.shape), _const_spec(wv.shape),
        ],
        out_specs=[pl.BlockSpec((tm, QK_WIDTH), lambda i: (i, 0)),
                   pl.BlockSpec((tm, MLA_WIDTH), lambda i: (i, 0))],
        out_shape=[jax.ShapeDtypeStruct((n_tok, QK_WIDTH), BF16),
                   jax.ShapeDtypeStruct((n_tok, MLA_WIDTH), BF16)],
        compiler_params=pltpu.CompilerParams(dimension_semantics=("parallel",)),
        name="kv_up",
    )(ckv2d, kr2d, wk, wv)


def _qk(qh, kh):
    return lax.dot_general(qh, kh, (((1,), (1,)), ((), ())), preferred_element_type=F32)


def _attn_kernel(*refs, heads, with_cache):
    if with_cache:
        q_ref, k_ref, v_ref, kc_ref, vc_ref, o_ref = refs
    else:
        q_ref, k_ref, v_ref, o_ref = refs

    def scores(hd):
        qs = slice(hd * HEAD_SLAB, (hd + 1) * HEAD_SLAB)
        qh = q_ref[0, :, qs]
        s = _qk(qh, k_ref[0, :, qs])
        sc = _qk(qh, kc_ref[0, :, qs]) if with_cache else None
        return s, sc

    nxt = scores(0)
    for hd in range(heads):
        s, sc = nxt
        if hd + 1 < heads:
            nxt = scores(hd + 1)
        vs = slice(hd * V_HEAD_DIM, (hd + 1) * V_HEAD_DIM)
        m = jnp.max(s, axis=-1, keepdims=True)
        if with_cache:
            m = jnp.maximum(m, jnp.max(sc, axis=-1, keepdims=True))
        p = jnp.exp2(s - m)
        l = jnp.sum(p, axis=-1, keepdims=True)
        o = _dot(p.astype(BF16), v_ref[0, :, vs])
        if with_cache:
            pc = jnp.exp2(sc - m)
            l = l + jnp.sum(pc, axis=-1, keepdims=True)
            o = o + _dot(pc.astype(BF16), vc_ref[0, :, vs])
        o_ref[0, :, vs] = (o * (1.0 / l)).astype(BF16)


def _attn_call(q3, k3, v3, cache, tq, heads):
    b, s, _ = q3.shape
    nk = k3.shape[1]
    groups = N_HEADS // heads
    qw, vw = heads * HEAD_SLAB, heads * V_HEAD_DIM
    in_specs = [
        pl.BlockSpec((1, tq, qw), lambda bi, g, i: (bi, i, g)),
        pl.BlockSpec((1, nk, qw), lambda bi, g, i: (bi, 0, g)),
        pl.BlockSpec((1, nk, vw), lambda bi, g, i: (bi, 0, g)),
    ]
    args = [q3, k3, v3]
    if cache is not None:
        kc3, vc3 = cache
        nc = kc3.shape[1]
        in_specs += [pl.BlockSpec((1, nc, qw), lambda bi, g, i: (bi, 0, g)),
                     pl.BlockSpec((1, nc, vw), lambda bi, g, i: (bi, 0, g))]
        args += [kc3, vc3]
    return pl.pallas_call(
        functools.partial(_attn_kernel, heads=heads, with_cache=cache is not None),
        grid=(b, groups, s // tq),
        in_specs=in_specs,
        out_specs=pl.BlockSpec((1, tq, vw), lambda bi, g, i: (bi, i, g)),
        out_shape=jax.ShapeDtypeStruct((b, s, MLA_WIDTH), BF16),
        compiler_params=pltpu.CompilerParams(
            dimension_semantics=("parallel", "parallel", "parallel"),
            vmem_limit_bytes=VMEM_LIMIT),
        name="attn_lat" if cache is not None else "attn_ctx",
    )(*args)


def _post_kernel(x_ref, a_ref, m_ref, mod_ref, wo_ref, norm2_ref, w1_ref, w2_ref, fn_ref, o_ref,
                 *, tm, sub, ff_chunk):
    def sub_tile(j):
        rows = slice(j * sub, (j + 1) * sub)
        cat = jnp.concatenate([a_ref[rows, :], m_ref[rows, :]], axis=1)
        y = x_ref[rows, :] + mod_ref[0, 2:3, :] * _dot(cat, wo_ref[...])
        gain = norm2_ref[...] * (1.0 + mod_ref[0, 4:5, :])
        h2 = (_rms(y, gain) + mod_ref[0, 3:4, :]).astype(BF16)
        yield
        acc = jnp.zeros((sub, D_MODEL), F32)
        for c in range(D_FF // ff_chunk):
            a = _dot(h2, w1_ref[:, c * ff_chunk:(c + 1) * ff_chunk])
            a = jnp.square(jnp.maximum(a, 0.0)).astype(BF16)
            acc = acc + _dot(a, w2_ref[c * ff_chunk:(c + 1) * ff_chunk, :])
            yield
        x2 = y + mod_ref[0, 5:6, :] * acc
        o_ref[rows, :] = _rms(x2, fn_ref[...])

    active = [sub_tile(j) for j in range(tm // sub)]
    while active:
        for gen in list(active):
            if next(gen, "done") == "done":
                active.remove(gen)


def _post_call(x2d, attn2d, mixed2d, mod3, mod_row_fn, weights, tm, sub):
    n_tok = x2d.shape[0]
    assert tm % sub == 0
    w_o, norm2, w1, w2, final_norm = weights
    in_specs = [
        pl.BlockSpec((tm, D_MODEL), lambda i: (i, 0)),
        pl.BlockSpec((tm, MLA_WIDTH), lambda i: (i, 0)),
        pl.BlockSpec((tm, POOL_WIDTH), lambda i: (i, 0)),
        pl.BlockSpec((1, 6, D_MODEL), lambda i: (mod_row_fn(i), 0, 0)),
        _resident_spec(w_o.shape), _resident_spec(norm2.shape), _resident_spec(w1.shape),
        _resident_spec(w2.shape), _resident_spec(final_norm.shape),
    ]
    return pl.pallas_call(
        functools.partial(_post_kernel, tm=tm, sub=sub, ff_chunk=1024),
        grid=(n_tok // tm,),
        in_specs=in_specs,
        out_specs=pl.BlockSpec((tm, D_MODEL), lambda i: (i, 0)),
        out_shape=jax.ShapeDtypeStruct((n_tok, D_MODEL), F32),
        compiler_params=pltpu.CompilerParams(dimension_semantics=("parallel",),
                                             vmem_limit_bytes=VMEM_LIMIT),
        name="post",
    )(x2d, attn2d, mixed2d, mod3, w_o, norm2, w1, w2, final_norm)


def _pack_w_in(w_in):
    qa = w_in[:, :Q_LORA_RANK + KV_LORA_RANK]
    kr = w_in[:, KR_OFF:KR_OFF + QK_ROPE_DIM]
    u = w_in[:, KR_OFF + QK_ROPE_DIM:]
    pad = jnp.zeros((D_MODEL, LANES - QK_ROPE_DIM), w_in.dtype)
    return jnp.concatenate([qa, kr, pad, u], axis=1).astype(BF16)


def _pack_wq(w_qb, with_swapped):
    w = w_qb.reshape(Q_LORA_RANK, N_HEADS, QK_DIM)
    tail = ((0, 0), (0, 0), (0, HEAD_SLAB - QK_DIM))
    plain = jnp.pad(w, tail).reshape(Q_LORA_RANK, QK_WIDTH)
    if not with_swapped:
        return plain.astype(BF16)
    r = w[:, :, QK_NOPE_DIM:].reshape(Q_LORA_RANK, N_HEADS, 2, 2, N_FREQ)
    sw = jnp.stack([-r[:, :, :, 1, :], r[:, :, :, 0, :]], axis=3).reshape(Q_LORA_RANK, N_HEADS, QK_ROPE_DIM)
    sw = jnp.concatenate([jnp.zeros((Q_LORA_RANK, N_HEADS, QK_NOPE_DIM), w.dtype), sw], axis=2)
    swapped = jnp.pad(sw, tail).reshape(Q_LORA_RANK, QK_WIDTH)
    return jnp.concatenate([plain, swapped], axis=1).astype(BF16)


def _pack_wkv(w_kvb):
    w = w_kvb.reshape(KV_LORA_RANK, N_HEADS, QK_NOPE_DIM + V_HEAD_DIM)
    wk_nope = jnp.pad(w[:, :, :QK_NOPE_DIM], ((0, 0), (0, 0), (0, HEAD_SLAB - QK_NOPE_DIM)))
    sel = jnp.pad(jnp.eye(QK_ROPE_DIM, dtype=w_kvb.dtype),
                  ((0, 0), (QK_NOPE_DIM, HEAD_SLAB - QK_DIM)))
    sel = jnp.broadcast_to(sel[:, None, :], (QK_ROPE_DIM, N_HEADS, HEAD_SLAB))
    zero = jnp.zeros((LANES - QK_ROPE_DIM, N_HEADS, HEAD_SLAB), w_kvb.dtype)
    wk = jnp.concatenate([wk_nope, sel, zero], axis=0).reshape(KV_LORA_RANK + LANES, QK_WIDTH)
    wv = w[:, :, QK_NOPE_DIM:].reshape(KV_LORA_RANK, MLA_WIDTH)
    return wk.astype(BF16), wv.astype(BF16)


def _rope_tables(n_tokens):
    f = np.float32
    rows = n_tokens // GRID_W
    row = np.repeat(np.arange(rows, dtype=f), GRID_W)
    col = np.tile(np.arange(GRID_W, dtype=f), rows)
    inv_freq = (1.0 / (f(ROPE_THETA) ** (np.arange(N_FREQ, dtype=f) * f(2.0) / f(AXIS_ROPE)))).astype(f)
    cos_r, sin_r = np.cos(row[:, None] * inv_freq), np.sin(row[:, None] * inv_freq)
    cos_c, sin_c = np.cos(col[:, None] * inv_freq), np.sin(col[:, None] * inv_freq)
    z = np.zeros_like(cos_r)
    c32 = np.concatenate([cos_r, cos_r, cos_c, cos_c], axis=1)
    s32 = np.concatenate([sin_r, sin_r, sin_c, sin_c], axis=1)
    up32 = np.concatenate([z, sin_r, z, sin_c], axis=1)
    dn32 = np.concatenate([-sin_r, z, -sin_c, z], axis=1)

    def place(t32, off, fill):
        left = np.full((n_tokens, off), fill, f)
        right = np.zeros((n_tokens, LANES - off - QK_ROPE_DIM), f)
        return np.concatenate([left, t32.astype(f), right], axis=1)

    q_tabs = (place(c32 * f(Q_SCALE), QK_NOPE_DIM, Q_SCALE), place(s32 * f(Q_SCALE), QK_NOPE_DIM, 0.0))
    k_tabs = (place(c32, 0, 0.0), place(up32, 0, 0.0), place(dn32, 0, 0.0))
    return q_tabs + k_tabs


def kernel(x_prompt, x_sample, cache_ckv, cache_krope, c, c_ctx, w_ada, b_ada, norm1, w_in, q_norm,
           w_qb, kv_norm, w_kvb, w_pool, pool_scale, w_o, norm2, w1, w2, final_norm):
    batch, seq, _ = x_prompt.shape
    dec_batch, dec_seq, _ = x_sample.shape
    depth = w_in.shape[0]
    past = cache_ckv.shape[2]
    assert depth == 1, "single trunk layer"
    assert dec_batch + 1 <= MOD_ROWS

    c_all = jnp.concatenate(
        [c, c_ctx[None, :], jnp.zeros((MOD_ROWS - dec_batch - 1, D_MODEL), F32)], axis=0)
    mod3 = _mod_call(c_all, w_ada[0], b_ada[0][None, :]).reshape(MOD_ROWS, 6, D_MODEL)

    wk, wv = _pack_wkv(w_kvb[0])

    def s1_weights(with_swapped):
        return (norm1[0][None, :], _pack_w_in(w_in[0]), q_norm[0][None, :],
                _pack_wq(w_qb[0], with_swapped), kv_norm[0][None, :], wk, wv,
                w_pool[0].astype(BF16), pool_scale[0][None, :])

    post_weights = (w_o[0].astype(BF16), norm2[0][None, :], w1[0].astype(BF16), w2[0].astype(BF16),
                    final_norm[None, :])

    xc = x_prompt.reshape(batch * seq, D_MODEL)
    ctx_row = lambda i: dec_batch
    qc, kc, vc, mc, ckv_new, kr_new = _stage1_call(xc, mod3, ctx_row, s1_weights(False), None, seq, seq, True)
    attn_c = _attn_call(qc.reshape(batch, seq, QK_WIDTH), kc.reshape(batch, seq, QK_WIDTH),
                        vc.reshape(batch, seq, MLA_WIDTH), None, seq, N_HEADS)
    y_prompt = _post_call(xc, attn_c.reshape(batch * seq, MLA_WIDTH), mc, mod3, ctx_row,
                          post_weights, POST_TILE, POST_SUB)

    tm_l = 512
    xl = x_sample.reshape(dec_batch * dec_seq, D_MODEL)
    lat_row = lambda i: i // (dec_seq // tm_l)
    ql, kl, vl, ml = _stage1_call(xl, mod3, lat_row, s1_weights(True), _rope_tables(dec_seq), dec_seq,
                                  tm_l, False)
    kcache, vcache = _kv_up_call(cache_ckv[:, 0].reshape(dec_batch * past, KV_LORA_RANK),
                                 cache_krope[:, 0].reshape(dec_batch * past, QK_ROPE_DIM), wk, wv, 512)
    attn_l = _attn_call(ql.reshape(dec_batch, dec_seq, QK_WIDTH), kl.reshape(dec_batch, dec_seq, QK_WIDTH),
                        vl.reshape(dec_batch, dec_seq, MLA_WIDTH),
                        (kcache.reshape(dec_batch, past, QK_WIDTH), vcache.reshape(dec_batch, past, MLA_WIDTH)),
                        512, N_HEADS)
    post_row = lambda i: i // (dec_seq // POST_TILE)
    y_sample = _post_call(xl, attn_l.reshape(dec_batch * dec_seq, MLA_WIDTH), ml, mod3, post_row,
                          post_weights, POST_TILE, POST_SUB)

    return (y_prompt.reshape(batch, seq, D_MODEL), y_sample.reshape(dec_batch, dec_seq, D_MODEL),
            ckv_new.reshape(batch, 1, seq, KV_LORA_RANK), kr_new.reshape(batch, 1, seq, QK_ROPE_DIM))
```

```python
import functools
import math

import jax
import jax.numpy as jnp
import numpy as np
from jax import lax
from jax.experimental import pallas as pl
from jax.experimental.pallas import tpu as pltpu

D_MODEL = 1024
N_HEADS = 8
QK_NOPE_DIM = 64
QK_ROPE_DIM = 32
V_HEAD_DIM = 64
QK_DIM = QK_NOPE_DIM + QK_ROPE_DIM
Q_LORA_RANK = 256
KV_LORA_RANK = 128
MLA_WIDTH = N_HEADS * V_HEAD_DIM
POOL_WIDTH = D_MODEL - MLA_WIDTH
POOL_WINDOWS = (2, 4, 8, 16)
POOL_GROUP = POOL_WIDTH // len(POOL_WINDOWS)
D_FF = 4 * D_MODEL
GRID_W = 64
ROPE_THETA = 10000.0
AXIS_ROPE = QK_ROPE_DIM // 2
N_FREQ = AXIS_ROPE // 2
EPS = 1e-6
ATTN_SCALE = 1.0 / math.sqrt(QK_DIM)
Q_SCALE = ATTN_SCALE * math.log2(math.e)

LANES = 128
SUBLANES = 8
HEAD_SLAB = LANES
QK_WIDTH = N_HEADS * HEAD_SLAB
V_WIDTH = N_HEADS * HEAD_SLAB
MOD_ROWS = 16
POOL_HALO = max(POOL_WINDOWS) // 2
assert POOL_HALO == SUBLANES
KR_OFF = Q_LORA_RANK + KV_LORA_RANK
U_OFF = KR_OFF + LANES
PROJ_COLS = U_OFF + POOL_WIDTH
VMEM_LIMIT = 56 * 1024 * 1024
POST_TILE = 512
POST_SUB = 256
F32 = jnp.float32
BF16 = jnp.bfloat16


def _dot(a, b):
    return jnp.dot(a, b, preferred_element_type=F32)


def _rms(x, g):
    return x * lax.rsqrt(jnp.mean(x * x, axis=-1, keepdims=True) + EPS) * g


def _const_spec(shape):
    nd = len(shape)
    return pl.BlockSpec(shape, lambda i: (0,) * nd)


def _resident_spec(shape):
    nd = len(shape)
    return pl.BlockSpec(shape, lambda i: (0,) * nd, pipeline_mode=pl.Buffered(1))


def _mod_kernel(c_ref, w_ref, b_ref, o_ref):
    c = c_ref[...]
    s = (c * jax.nn.sigmoid(c)).astype(BF16)
    o_ref[...] = _dot(s, w_ref[...].astype(BF16)) + b_ref[...]


def _mod_call(c_all, w_ada, b_ada):
    tn = 1024
    n = w_ada.shape[1]
    return pl.pallas_call(
        _mod_kernel,
        grid=(n // tn,),
        in_specs=[
            pl.BlockSpec((MOD_ROWS, D_MODEL), lambda j: (0, 0)),
            pl.BlockSpec((D_MODEL, tn), lambda j: (0, j)),
            pl.BlockSpec((1, tn), lambda j: (0, j)),
        ],
        out_specs=pl.BlockSpec((MOD_ROWS, tn), lambda j: (0, j)),
        out_shape=jax.ShapeDtypeStruct((MOD_ROWS, n), F32),
        compiler_params=pltpu.CompilerParams(dimension_semantics=("arbitrary",)),
        name="mod",
    )(c_all, w_ada, b_ada)


def _values_ext(ckv16, wv_ref):
    v = _dot(ckv16, wv_ref[...])
    lane = lax.broadcasted_iota(jnp.int32, v.shape, 1) % (2 * HEAD_SLAB)
    ones = (lane >= V_HEAD_DIM) & (lane < 2 * HEAD_SLAB - V_HEAD_DIM)
    return jnp.where(ones, 1.0, v).astype(BF16)


def _rope_slab(slab, c, s_up, s_dn):
    return (slab * c + pltpu.roll(slab, N_FREQ, 1) * s_up
            + pltpu.roll(slab, LANES - N_FREQ, 1) * s_dn)


def _inv_counts(w, tm, at_seq_start, at_seq_end):
    half = w // 2
    r = lax.broadcasted_iota(jnp.int32, (SUBLANES, LANES), 0)
    first = jnp.where(at_seq_start, (r + half) - jnp.maximum(r - half, 0), w)
    last = jnp.where(at_seq_end, jnp.minimum(half, SUBLANES - r) + half, w)
    mid = jnp.full((tm - 2 * SUBLANES, LANES), 1.0 / w, F32)
    return jnp.concatenate([1.0 / first.astype(F32), mid, 1.0 / last.astype(F32)], axis=0)


def _pool_mix(e_ref, c2_ref, c4_ref, wp_ref, ps_ref, m_ref, tm, at_seq_start, at_seq_end):
    rows = tm + 2 * POOL_HALO
    g = [slice(i * POOL_GROUP, (i + 1) * POOL_GROUP) for i in range(len(POOL_WINDOWS))]
    wide = slice(POOL_GROUP, POOL_WIDTH)
    wider = slice(2 * POOL_GROUP, POOL_WIDTH)
    zeros = jnp.zeros((SUBLANES, POOL_WIDTH), F32)
    c2_ref[rows:, :] = zeros
    c4_ref[rows:, :] = zeros
    c2_ref[0:rows, wide] = e_ref[0:rows, wide] + e_ref[1:rows + 1, wide]
    c4_ref[0:rows, wider] = c2_ref[0:rows, wider] + c2_ref[2:rows + 2, wider]
    c8 = c4_ref[0:rows, g[3]] + c4_ref[4:rows + 4, g[3]]
    sums = [
        e_ref[7:7 + tm, g[0]] + e_ref[8:8 + tm, g[0]],
        c2_ref[6:6 + tm, g[1]] + c2_ref[8:8 + tm, g[1]],
        c4_ref[4:4 + tm, g[2]] + c4_ref[8:8 + tm, g[2]],
        c8[0:tm] + c8[8:8 + tm],
    ]
    for i, w in enumerate(POOL_WINDOWS):
        inv = _inv_counts(w, tm, at_seq_start, at_seq_end)
        pooled = sums[i] * inv - e_ref[POOL_HALO:POOL_HALO + tm, g[i]]
        mg = _dot(pooled.astype(BF16), wp_ref[i]) * ps_ref[:, g[i]]
        m_ref[:, g[i]] = mg.astype(BF16)


def _stage1_kernel(*refs, use_rope, halo, emit_cache, tm, seq):
    it = iter(refs)
    x_ref = next(it)
    if halo:
        xp_ref, xn_ref = next(it), next(it)
    (mod_ref, norm1_ref, w_in_ref, qn_ref, wq_ref, kvn_ref, wk_ref, wv_ref, wp_ref,
     ps_ref) = (next(it) for _ in range(10))
    if use_rope:
        q_tabs = [next(it) for _ in range(2)]
        k_tabs = [next(it) for _ in range(3)]
    q_ref, k_ref, v_ref, m_ref = (next(it) for _ in range(4))
    if emit_cache:
        ckv_ref, kr_ref = next(it), next(it)
    e_ref, c2_ref, c4_ref = next(it), next(it), next(it)

    pos0 = (pl.program_id(0) * tm) % seq
    at_seq_start = pos0 == 0
    at_seq_end = pos0 + tm == seq

    if halo:
        x = jnp.concatenate([xp_ref[...], x_ref[...], xn_ref[...]], axis=0)
    else:
        x = x_ref[...]
    gain = norm1_ref[...] * (1.0 + mod_ref[0, 1:2, :])
    h = _rms(x, gain) + mod_ref[0, 0:1, :]
    proj = _dot(h.astype(BF16), w_in_ref[...])

    u = proj[:, U_OFF:]
    zeros = jnp.zeros((POOL_HALO, POOL_WIDTH), F32)
    if halo:
        e_ref[0:POOL_HALO, :] = jnp.where(at_seq_start, 0.0, u[0:POOL_HALO])
        e_ref[POOL_HALO:POOL_HALO + tm, :] = u[POOL_HALO:POOL_HALO + tm]
        e_ref[POOL_HALO + tm:2 * POOL_HALO + tm, :] = jnp.where(at_seq_end, 0.0, u[POOL_HALO + tm:])
        proj = proj[POOL_HALO:POOL_HALO + tm]
    else:
        e_ref[0:POOL_HALO, :] = zeros
        e_ref[POOL_HALO:POOL_HALO + tm, :] = u
        e_ref[POOL_HALO + tm:2 * POOL_HALO + tm, :] = zeros
    e_ref[2 * POOL_HALO + tm:, :] = zeros
    _pool_mix(e_ref, c2_ref, c4_ref, wp_ref, ps_ref, m_ref, tm, at_seq_start, at_seq_end)

    qa = _rms(proj[:, :Q_LORA_RANK], qn_ref[...]).astype(BF16)
    q = _dot(qa, wq_ref[...])
    if use_rope:
        pos = pl.multiple_of(pos0, tm)
        cq, sq = (r[pl.ds(pos, tm), :] for r in q_tabs)
    for hd in range(N_HEADS):
        slab = q[:, hd * HEAD_SLAB:(hd + 1) * HEAD_SLAB]
        if use_rope:
            swapped = q[:, QK_WIDTH + hd * HEAD_SLAB:QK_WIDTH + (hd + 1) * HEAD_SLAB]
            slab = slab * cq + swapped * sq
        else:
            slab = slab * Q_SCALE
        q_ref[:, hd * HEAD_SLAB:(hd + 1) * HEAD_SLAB] = slab.astype(BF16)

    ckv = _rms(proj[:, Q_LORA_RANK:KR_OFF], kvn_ref[...])
    kr = proj[:, KR_OFF:U_OFF]
    if emit_cache:
        ckv_ref[...] = ckv
        kr_ref[...] = kr[:, :QK_ROPE_DIM]
    if use_rope:
        kr = _rope_slab(kr, *(r[pl.ds(pos, tm), :] for r in k_tabs))
    ckv16 = ckv.astype(BF16)
    kin = jnp.concatenate([ckv16, kr.astype(BF16)], axis=1)
    k_ref[...] = _dot(kin, wk_ref[...]).astype(BF16)
    v_ref[...] = _values_ext(ckv16, wv_ref)


def _stage1_call(x2d, mod3, mod_row_fn, weights, rope_tabs, seq, tm, emit_cache):
    n_tok = x2d.shape[0]
    assert seq % tm == 0 and seq >= 2 * SUBLANES and tm >= 2 * SUBLANES
    use_rope = rope_tabs is not None
    halo = seq > tm
    halo_blocks = n_tok // POOL_HALO
    per_tile = tm // POOL_HALO
    in_specs = [pl.BlockSpec((tm, D_MODEL), lambda i: (i, 0))]
    args = [x2d]
    if halo:
        in_specs += [
            pl.BlockSpec((POOL_HALO, D_MODEL), lambda i: (jnp.maximum(i * per_tile - 1, 0), 0)),
            pl.BlockSpec((POOL_HALO, D_MODEL),
                         lambda i: (jnp.minimum((i + 1) * per_tile, halo_blocks - 1), 0)),
        ]
        args += [x2d, x2d]
    in_specs += [pl.BlockSpec((1, 6, D_MODEL), lambda i: (mod_row_fn(i), 0, 0))]
    in_specs += [_const_spec(w.shape) for w in weights]
    args += [mod3] + list(weights)
    if use_rope:
        in_specs += [_resident_spec((seq, LANES))] * len(rope_tabs)
        args += list(rope_tabs)
    out_specs = [
        pl.BlockSpec((tm, QK_WIDTH), lambda i: (i, 0)),
        pl.BlockSpec((tm, QK_WIDTH), lambda i: (i, 0)),
        pl.BlockSpec((tm, V_WIDTH), lambda i: (i, 0)),
        pl.BlockSpec((tm, POOL_WIDTH), lambda i: (i, 0)),
    ]
    out_shape = [
        jax.ShapeDtypeStruct((n_tok, QK_WIDTH), BF16),
        jax.ShapeDtypeStruct((n_tok, QK_WIDTH), BF16),
        jax.ShapeDtypeStruct((n_tok, V_WIDTH), BF16),
        jax.ShapeDtypeStruct((n_tok, POOL_WIDTH), BF16),
    ]
    if emit_cache:
        out_specs += [pl.BlockSpec((tm, KV_LORA_RANK), lambda i: (i, 0)),
                      pl.BlockSpec((tm, QK_ROPE_DIM), lambda i: (i, 0))]
        out_shape += [jax.ShapeDtypeStruct((n_tok, KV_LORA_RANK), F32),
                      jax.ShapeDtypeStruct((n_tok, QK_ROPE_DIM), F32)]
    stencil_rows = tm + 3 * POOL_HALO
    return pl.pallas_call(
        functools.partial(_stage1_kernel, use_rope=use_rope, halo=halo, emit_cache=emit_cache,
                          tm=tm, seq=seq),
        grid=(n_tok // tm,),
        in_specs=in_specs,
        out_specs=out_specs,
        out_shape=out_shape,
        scratch_shapes=[pltpu.VMEM((stencil_rows, POOL_WIDTH), F32)] * 3,
        compiler_params=pltpu.CompilerParams(dimension_semantics=("parallel",),
                                             vmem_limit_bytes=VMEM_LIMIT),
        name="stage1_rope" if use_rope else "stage1_ctx",
    )(*args)


def _kv_up_kernel(ckv_ref, kr_ref, wk_ref, wv_ref, k_ref, v_ref):
    ckv16 = ckv_ref[...].astype(BF16)
    kr = kr_ref[...].astype(BF16)
    pad = jnp.zeros((kr.shape[0], LANES - QK_ROPE_DIM), BF16)
    kin = jnp.concatenate([ckv16, kr, pad], axis=1)
    k_ref[...] = _dot(kin, wk_ref[...]).astype(BF16)
    v_ref[...] = _values_ext(ckv16, wv_ref)


def _kv_up_call(ckv2d, kr2d, wk, wv, tm):
    n_tok = ckv2d.shape[0]
    return pl.pallas_call(
        _kv_up_kernel,
        grid=(n_tok // tm,),
        in_specs=[
            pl.BlockSpec((tm, KV_LORA_RANK), lambda i: (i, 0)),
            pl.BlockSpec((tm, QK_ROPE_DIM), lambda i: (i, 0)),
            _const_spec(wk.shape), _const_spec(wv.shape),
        ],
        out_specs=[pl.BlockSpec((tm, QK_WIDTH), lambda i: (i, 0)),
                   pl.BlockSpec((tm, V_WIDTH), lambda i: (i, 0))],
        out_shape=[jax.ShapeDtypeStruct((n_tok, QK_WIDTH), BF16),
                   jax.ShapeDtypeStruct((n_tok, V_WIDTH), BF16)],
        compiler_params=pltpu.CompilerParams(dimension_semantics=("parallel",)),
        name="kv_up",
    )(ckv2d, kr2d, wk, wv)


def _qk(qh, kh):
    return lax.dot_general(qh, kh, (((1,), (1,)), ((), ())), preferred_element_type=F32)


def _attn_kernel(*refs, heads, with_cache):
    if with_cache:
        q_ref, k_ref, v_ref, kc_ref, vc_ref, o_ref = refs
    else:
        q_ref, k_ref, v_ref, o_ref = refs

    def scores(hd):
        qs = slice(hd * HEAD_SLAB, (hd + 1) * HEAD_SLAB)
        qh = q_ref[0, :, qs]
        s = _qk(qh, k_ref[0, :, qs])
        sc = _qk(qh, kc_ref[0, :, qs]) if with_cache else None
        return s, sc

    nxt = scores(0)
    for hd in range(heads):
        s, sc = nxt
        if hd + 1 < heads:
            nxt = scores(hd + 1)
        vs = slice(hd * HEAD_SLAB, (hd + 1) * HEAD_SLAB)
        m = jnp.max(s, axis=-1, keepdims=True)
        if with_cache:
            m = jnp.maximum(m, jnp.max(sc, axis=-1, keepdims=True))
        o = _dot(jnp.exp2(s - m).astype(BF16), v_ref[0, :, vs])
        if with_cache:
            o = o + _dot(jnp.exp2(sc - m).astype(BF16), vc_ref[0, :, vs])
        if hd % 2 == 0:
            o_even = o
        else:
            low = lax.broadcasted_iota(jnp.int32, o.shape, 1) < V_HEAD_DIM
            num = jnp.where(low, o_even, o)
            den = pltpu.roll(jnp.where(low, o, o_even), V_HEAD_DIM, 1)
            o_ref[0, :, (hd - 1) * V_HEAD_DIM:(hd + 1) * V_HEAD_DIM] = (num * (1.0 / den)).astype(BF16)


def _attn_call(q3, k3, v3, cache, tq, heads):
    b, s, _ = q3.shape
    nk = k3.shape[1]
    groups = N_HEADS // heads
    qw, vw, ow = heads * HEAD_SLAB, heads * HEAD_SLAB, heads * V_HEAD_DIM
    in_specs = [
        pl.BlockSpec((1, tq, qw), lambda bi, g, i: (bi, i, g)),
        pl.BlockSpec((1, nk, qw), lambda bi, g, i: (bi, 0, g)),
        pl.BlockSpec((1, nk, vw), lambda bi, g, i: (bi, 0, g)),
    ]
    args = [q3, k3, v3]
    if cache is not None:
        kc3, vc3 = cache
        nc = kc3.shape[1]
        in_specs += [pl.BlockSpec((1, nc, qw), lambda bi, g, i: (bi, 0, g)),
                     pl.BlockSpec((1, nc, vw), lambda bi, g, i: (bi, 0, g))]
        args += [kc3, vc3]
    return pl.pallas_call(
        functools.partial(_attn_kernel, heads=heads, with_cache=cache is not None),
        grid=(b, groups, s // tq),
        in_specs=in_specs,
        out_specs=pl.BlockSpec((1, tq, ow), lambda bi, g, i: (bi, i, g)),
        out_shape=jax.ShapeDtypeStruct((b, s, MLA_WIDTH), BF16),
        compiler_params=pltpu.CompilerParams(
            dimension_semantics=("parallel", "parallel", "parallel"),
            vmem_limit_bytes=VMEM_LIMIT),
        name="attn_lat" if cache is not None else "attn_ctx",
    )(*args)


def _post_kernel(x_ref, a_ref, m_ref, mod_ref, wo_ref, norm2_ref, w1_ref, w2_ref, fn_ref, o_ref,
                 *, tm, sub, ff_chunk):
    def sub_tile(j):
        rows = slice(j * sub, (j + 1) * sub)
        cat = jnp.concatenate([a_ref[rows, :], m_ref[rows, :]], axis=1)
        y = x_ref[rows, :] + mod_ref[0, 2:3, :] * _dot(cat, wo_ref[...])
        gain = norm2_ref[...] * (1.0 + mod_ref[0, 4:5, :])
        h2 = (_rms(y, gain) + mod_ref[0, 3:4, :]).astype(BF16)
        yield
        acc = jnp.zeros((sub, D_MODEL), F32)
        for c in range(D_FF // ff_chunk):
            a = _dot(h2, w1_ref[:, c * ff_chunk:(c + 1) * ff_chunk])
            a = jnp.square(jnp.maximum(a, 0.0)).astype(BF16)
            acc = acc + _dot(a, w2_ref[c * ff_chunk:(c + 1) * ff_chunk, :])
            yield
        x2 = y + mod_ref[0, 5:6, :] * acc
        o_ref[rows, :] = _rms(x2, fn_ref[...])

    active = [sub_tile(j) for j in range(tm // sub)]
    while active:
        for gen in list(active):
            if next(gen, "done") == "done":
                active.remove(gen)


def _post_call(x2d, attn2d, mixed2d, mod3, mod_row_fn, weights, tm, sub):
    n_tok = x2d.shape[0]
    assert tm % sub == 0
    w_o, norm2, w1, w2, final_norm = weights
    in_specs = [
        pl.BlockSpec((tm, D_MODEL), lambda i: (i, 0)),
        pl.BlockSpec((tm, MLA_WIDTH), lambda i: (i, 0)),
        pl.BlockSpec((tm, POOL_WIDTH), lambda i: (i, 0)),
        pl.BlockSpec((1, 6, D_MODEL), lambda i: (mod_row_fn(i), 0, 0)),
        _resident_spec(w_o.shape), _resident_spec(norm2.shape), _resident_spec(w1.shape),
        _resident_spec(w2.shape), _resident_spec(final_norm.shape),
    ]
    return pl.pallas_call(
        functools.partial(_post_kernel, tm=tm, sub=sub, ff_chunk=1024),
        grid=(n_tok // tm,),
        in_specs=in_specs,
        out_specs=pl.BlockSpec((tm, D_MODEL), lambda i: (i, 0)),
        out_shape=jax.ShapeDtypeStruct((n_tok, D_MODEL), F32),
        compiler_params=pltpu.CompilerParams(dimension_semantics=("parallel",),
                                             vmem_limit_bytes=VMEM_LIMIT),
        name="post",
    )(x2d, attn2d, mixed2d, mod3, w_o, norm2, w1, w2, final_norm)


def _pack_w_in(w_in):
    qa = w_in[:, :Q_LORA_RANK + KV_LORA_RANK]
    kr = w_in[:, KR_OFF:KR_OFF + QK_ROPE_DIM]
    u = w_in[:, KR_OFF + QK_ROPE_DIM:]
    pad = jnp.zeros((D_MODEL, LANES - QK_ROPE_DIM), w_in.dtype)
    return jnp.concatenate([qa, kr, pad, u], axis=1).astype(BF16)


def _pack_wq(w_qb, with_swapped):
    w = w_qb.reshape(Q_LORA_RANK, N_HEADS, QK_DIM)
    tail = ((0, 0), (0, 0), (0, HEAD_SLAB - QK_DIM))
    plain = jnp.pad(w, tail).reshape(Q_LORA_RANK, QK_WIDTH)
    if not with_swapped:
        return plain.astype(BF16)
    r = w[:, :, QK_NOPE_DIM:].reshape(Q_LORA_RANK, N_HEADS, 2, 2, N_FREQ)
    sw = jnp.stack([-r[:, :, :, 1, :], r[:, :, :, 0, :]], axis=3).reshape(Q_LORA_RANK, N_HEADS, QK_ROPE_DIM)
    sw = jnp.concatenate([jnp.zeros((Q_LORA_RANK, N_HEADS, QK_NOPE_DIM), w.dtype), sw], axis=2)
    swapped = jnp.pad(sw, tail).reshape(Q_LORA_RANK, QK_WIDTH)
    return jnp.concatenate([plain, swapped], axis=1).astype(BF16)


def _pack_wkv(w_kvb):
    w = w_kvb.reshape(KV_LORA_RANK, N_HEADS, QK_NOPE_DIM + V_HEAD_DIM)
    wk_nope = jnp.pad(w[:, :, :QK_NOPE_DIM], ((0, 0), (0, 0), (0, HEAD_SLAB - QK_NOPE_DIM)))
    sel = jnp.pad(jnp.eye(QK_ROPE_DIM, dtype=w_kvb.dtype),
                  ((0, 0), (QK_NOPE_DIM, HEAD_SLAB - QK_DIM)))
    sel = jnp.broadcast_to(sel[:, None, :], (QK_ROPE_DIM, N_HEADS, HEAD_SLAB))
    zero = jnp.zeros((LANES - QK_ROPE_DIM, N_HEADS, HEAD_SLAB), w_kvb.dtype)
    wk = jnp.concatenate([wk_nope, sel, zero], axis=0).reshape(KV_LORA_RANK + LANES, QK_WIDTH)
    wv = w[:, :, QK_NOPE_DIM:].reshape(KV_LORA_RANK, N_HEADS // 2, 2, V_HEAD_DIM)
    gap = jnp.zeros((KV_LORA_RANK, N_HEADS // 2, 2 * (HEAD_SLAB - V_HEAD_DIM)), w_kvb.dtype)
    wv = jnp.concatenate([wv[:, :, 0], gap, wv[:, :, 1]], axis=2).reshape(KV_LORA_RANK, V_WIDTH)
    return wk.astype(BF16), wv.astype(BF16)


def _rope_tables(n_tokens):
    f = np.float32
    rows = n_tokens // GRID_W
    row = np.repeat(np.arange(rows, dtype=f), GRID_W)
    col = np.tile(np.arange(GRID_W, dtype=f), rows)
    inv_freq = (1.0 / (f(ROPE_THETA) ** (np.arange(N_FREQ, dtype=f) * f(2.0) / f(AXIS_ROPE)))).astype(f)
    cos_r, sin_r = np.cos(row[:, None] * inv_freq), np.sin(row[:, None] * inv_freq)
    cos_c, sin_c = np.cos(col[:, None] * inv_freq), np.sin(col[:, None] * inv_freq)
    z = np.zeros_like(cos_r)
    c32 = np.concatenate([cos_r, cos_r, cos_c, cos_c], axis=1)
    s32 = np.concatenate([sin_r, sin_r, sin_c, sin_c], axis=1)
    up32 = np.concatenate([z, sin_r, z, sin_c], axis=1)
    dn32 = np.concatenate([-sin_r, z, -sin_c, z], axis=1)

    def place(t32, off, fill):
        left = np.full((n_tokens, off), fill, f)
        right = np.zeros((n_tokens, LANES - off - QK_ROPE_DIM), f)
        return np.concatenate([left, t32.astype(f), right], axis=1)

    q_tabs = (place(c32 * f(Q_SCALE), QK_NOPE_DIM, Q_SCALE), place(s32 * f(Q_SCALE), QK_NOPE_DIM, 0.0))
    k_tabs = (place(c32, 0, 0.0), place(up32, 0, 0.0), place(dn32, 0, 0.0))
    return q_tabs + k_tabs


def kernel(x_prompt, x_sample, cache_ckv, cache_krope, c, c_ctx, w_ada, b_ada, norm1, w_in, q_norm,
           w_qb, kv_norm, w_kvb, w_pool, pool_scale, w_o, norm2, w1, w2, final_norm):
    batch, seq, _ = x_prompt.shape
    dec_batch, dec_seq, _ = x_sample.shape
    depth = w_in.shape[0]
    past = cache_ckv.shape[2]
    assert depth == 1, "single trunk layer"
    assert dec_batch + 1 <= MOD_ROWS

    c_all = jnp.concatenate(
        [c, c_ctx[None, :], jnp.zeros((MOD_ROWS - dec_batch - 1, D_MODEL), F32)], axis=0)
    mod3 = _mod_call(c_all, w_ada[0], b_ada[0][None, :]).reshape(MOD_ROWS, 6, D_MODEL)

    wk, wv = _pack_wkv(w_kvb[0])

    def s1_weights(with_swapped):
        return (norm1[0][None, :], _pack_w_in(w_in[0]), q_norm[0][None, :],
                _pack_wq(w_qb[0], with_swapped), kv_norm[0][None, :], wk, wv,
                w_pool[0].astype(BF16), pool_scale[0][None, :])

    post_weights = (w_o[0].astype(BF16), norm2[0][None, :], w1[0].astype(BF16), w2[0].astype(BF16),
                    final_norm[None, :])

    xc = x_prompt.reshape(batch * seq, D_MODEL)
    ctx_row = lambda i: dec_batch
    qc, kc, vc, mc, ckv_new, kr_new = _stage1_call(xc, mod3, ctx_row, s1_weights(False), None, seq, seq, True)
    attn_c = _attn_call(qc.reshape(batch, seq, QK_WIDTH), kc.reshape(batch, seq, QK_WIDTH),
                        vc.reshape(batch, seq, V_WIDTH), None, seq, N_HEADS)
    y_prompt = _post_call(xc, attn_c.reshape(batch * seq, MLA_WIDTH), mc, mod3, ctx_row,
                          post_weights, POST_TILE, POST_SUB)

    tm_l = 512
    xl = x_sample.reshape(dec_batch * dec_seq, D_MODEL)
    lat_row = lambda i: i // (dec_seq // tm_l)
    ql, kl, vl, ml = _stage1_call(xl, mod3, lat_row, s1_weights(True), _rope_tables(dec_seq), dec_seq,
                                  tm_l, False)
    kcache, vcache = _kv_up_call(cache_ckv[:, 0].reshape(dec_batch * past, KV_LORA_RANK),
                                 cache_krope[:, 0].reshape(dec_batch * past, QK_ROPE_DIM), wk, wv, 512)
    attn_l = _attn_call(ql.reshape(dec_batch, dec_seq, QK_WIDTH), kl.reshape(dec_batch, dec_seq, QK_WIDTH),
                        vl.reshape(dec_batch, dec_seq, V_WIDTH),
                        (kcache.reshape(dec_batch, past, QK_WIDTH), vcache.reshape(dec_batch, past, V_WIDTH)),
                        512, N_HEADS)
    post_row = lambda i: i // (dec_seq // POST_TILE)
    y_sample = _post_call(xl, attn_l.reshape(dec_batch * dec_seq, MLA_WIDTH), ml, mod3, post_row,
                          post_weights, POST_TILE, POST_SUB)

    return (y_prompt.reshape(batch, seq, D_MODEL), y_sample.reshape(dec_batch, dec_seq, D_MODEL),
            ckv_new.reshape(batch, 1, seq, KV_LORA_RANK), kr_new.reshape(batch, 1, seq, QK_ROPE_DIM))
```

```python
import functools
import math

import jax
import jax.numpy as jnp
import numpy as np
from jax import lax
from jax.experimental import pallas as pl
from jax.experimental.pallas import tpu as pltpu

D_MODEL = 1024
N_HEADS = 8
QK_NOPE_DIM = 64
QK_ROPE_DIM = 32
V_HEAD_DIM = 64
QK_DIM = QK_NOPE_DIM + QK_ROPE_DIM
Q_LORA_RANK = 256
KV_LORA_RANK = 128
MLA_WIDTH = N_HEADS * V_HEAD_DIM
POOL_WIDTH = D_MODEL - MLA_WIDTH
POOL_WINDOWS = (2, 4, 8, 16)
POOL_GROUP = POOL_WIDTH // len(POOL_WINDOWS)
D_FF = 4 * D_MODEL
GRID_W = 64
ROPE_THETA = 10000.0
AXIS_ROPE = QK_ROPE_DIM // 2
N_FREQ = AXIS_ROPE // 2
EPS = 1e-6
ATTN_SCALE = 1.0 / math.sqrt(QK_DIM)
Q_SCALE = ATTN_SCALE * math.log2(math.e)

LANES = 128
SUBLANES = 8
HEAD_SLAB = LANES
QK_WIDTH = N_HEADS * HEAD_SLAB
MOD_ROWS = 16
POOL_HALO = max(POOL_WINDOWS) // 2
assert POOL_HALO == SUBLANES
KR_OFF = Q_LORA_RANK + KV_LORA_RANK
U_OFF = KR_OFF + LANES
PROJ_COLS = U_OFF + POOL_WIDTH
VMEM_LIMIT = 56 * 1024 * 1024
S1_TILE = 512
S1_SUB = 256
LAT_Q_TILE = 512
LAT_Q_SUBS = 1
POST_TILE = 512
POST_SUB = 256
F32 = jnp.float32
BF16 = jnp.bfloat16


def _dot(a, b):
    return jnp.dot(a, b, preferred_element_type=F32)


def _rms(x, g):
    return x * lax.rsqrt(jnp.mean(x * x, axis=-1, keepdims=True) + EPS) * g


def _round_robin(phased):
    active = list(phased)
    while active:
        for gen in list(active):
            if next(gen, "done") == "done":
                active.remove(gen)


def _const_spec(shape):
    nd = len(shape)
    return pl.BlockSpec(shape, lambda i: (0,) * nd)


def _resident_spec(shape):
    nd = len(shape)
    return pl.BlockSpec(shape, lambda i: (0,) * nd, pipeline_mode=pl.Buffered(1))


def _mod_kernel(c_ref, w_ref, b_ref, o_ref):
    c = c_ref[...]
    s = (c * jax.nn.sigmoid(c)).astype(BF16)
    o_ref[...] = _dot(s, w_ref[...].astype(BF16)) + b_ref[...]


def _mod_call(c_all, w_ada, b_ada):
    tn = 1024
    n = w_ada.shape[1]
    return pl.pallas_call(
        _mod_kernel,
        grid=(n // tn,),
        in_specs=[
            pl.BlockSpec((MOD_ROWS, D_MODEL), lambda j: (0, 0)),
            pl.BlockSpec((D_MODEL, tn), lambda j: (0, j)),
            pl.BlockSpec((1, tn), lambda j: (0, j)),
        ],
        out_specs=pl.BlockSpec((MOD_ROWS, tn), lambda j: (0, j)),
        out_shape=jax.ShapeDtypeStruct((MOD_ROWS, n), F32),
        compiler_params=pltpu.CompilerParams(dimension_semantics=("arbitrary",)),
        name="mod",
    )(c_all, w_ada, b_ada)


def _rope_slab(slab, c, s_up, s_dn):
    return (slab * c + pltpu.roll(slab, N_FREQ, 1) * s_up
            + pltpu.roll(slab, LANES - N_FREQ, 1) * s_dn)


def _inv_counts(w, tm, at_seq_start, at_seq_end):
    half = w // 2
    r = lax.broadcasted_iota(jnp.int32, (SUBLANES, LANES), 0)
    first = jnp.where(at_seq_start, (r + half) - jnp.maximum(r - half, 0), w)
    last = jnp.where(at_seq_end, jnp.minimum(half, SUBLANES - r) + half, w)
    mid = jnp.full((tm - 2 * SUBLANES, LANES), 1.0 / w, F32)
    return jnp.concatenate([1.0 / first.astype(F32), mid, 1.0 / last.astype(F32)], axis=0)


def _pool_windows(e_ref, c2_ref, c4_ref, tm, at_seq_start, at_seq_end):
    rows = tm + 2 * POOL_HALO
    g = [slice(i * POOL_GROUP, (i + 1) * POOL_GROUP) for i in range(len(POOL_WINDOWS))]
    wide = slice(POOL_GROUP, POOL_WIDTH)
    wider = slice(2 * POOL_GROUP, POOL_WIDTH)
    zeros = jnp.zeros((SUBLANES, POOL_WIDTH), F32)
    c2_ref[rows:, :] = zeros
    c4_ref[rows:, :] = zeros
    c2_ref[0:rows, wide] = e_ref[0:rows, wide] + e_ref[1:rows + 1, wide]
    c4_ref[0:rows, wider] = c2_ref[0:rows, wider] + c2_ref[2:rows + 2, wider]
    c8 = c4_ref[0:rows, g[3]] + c4_ref[4:rows + 4, g[3]]
    sums = [
        e_ref[7:7 + tm, g[0]] + e_ref[8:8 + tm, g[0]],
        c2_ref[6:6 + tm, g[1]] + c2_ref[8:8 + tm, g[1]],
        c4_ref[4:4 + tm, g[2]] + c4_ref[8:8 + tm, g[2]],
        c8[0:tm] + c8[8:8 + tm],
    ]
    return [(sums[i] * _inv_counts(w, tm, at_seq_start, at_seq_end)
             - e_ref[POOL_HALO:POOL_HALO + tm, g[i]]).astype(BF16)
            for i, w in enumerate(POOL_WINDOWS)]


def _stage1_kernel(*refs, use_rope, halo, emit_cache, tm, sub, seq):
    it = iter(refs)
    x_ref = next(it)
    if halo:
        xp_ref, xn_ref = next(it), next(it)
    (mod_ref, norm1_ref, w_in_ref, qn_ref, wq_ref, kvn_ref, wk_ref, wv_ref, wp_ref,
     ps_ref) = (next(it) for _ in range(10))
    if use_rope:
        q_tabs = [next(it) for _ in range(2)]
        k_tabs = [next(it) for _ in range(3)]
    q_ref, k_ref, v_ref, m_ref = (next(it) for _ in range(4))
    if emit_cache:
        ckv_ref, kr_ref = next(it), next(it)
    e_ref, c2_ref, c4_ref = next(it), next(it), next(it)

    n_sub = tm // sub

    def sub_tile(j):
        r0 = j * sub
        rows = slice(r0, r0 + sub)
        pos0 = (pl.program_id(0) * tm + r0) % seq
        at_seq_start = pos0 == 0
        at_seq_end = pos0 + sub == seq
        e, c2, c4 = e_ref.at[j], c2_ref.at[j], c4_ref.at[j]

        if halo:
            before = xp_ref[...] if j == 0 else x_ref[r0 - POOL_HALO:r0, :]
            after = xn_ref[...] if j == n_sub - 1 else x_ref[r0 + sub:r0 + sub + POOL_HALO, :]
            x = jnp.concatenate([before, x_ref[rows, :], after], axis=0)
        else:
            x = x_ref[rows, :]
        gain = norm1_ref[...] * (1.0 + mod_ref[0, 1:2, :])
        h = (_rms(x, gain) + mod_ref[0, 0:1, :]).astype(BF16)
        yield
        proj = _dot(h, w_in_ref[...])
        yield
        u = proj[:, U_OFF:]
        zeros = jnp.zeros((POOL_HALO, POOL_WIDTH), F32)
        if halo:
            e[0:POOL_HALO, :] = jnp.where(at_seq_start, 0.0, u[0:POOL_HALO])
            e[POOL_HALO:POOL_HALO + sub, :] = u[POOL_HALO:POOL_HALO + sub]
            e[POOL_HALO + sub:2 * POOL_HALO + sub, :] = jnp.where(at_seq_end, 0.0, u[POOL_HALO + sub:])
            proj = proj[POOL_HALO:POOL_HALO + sub]
        else:
            e[0:POOL_HALO, :] = zeros
            e[POOL_HALO:POOL_HALO + sub, :] = u
            e[POOL_HALO + sub:2 * POOL_HALO + sub, :] = zeros
        e[2 * POOL_HALO + sub:, :] = zeros
        pooled = _pool_windows(e, c2, c4, sub, at_seq_start, at_seq_end)
        qa = _rms(proj[:, :Q_LORA_RANK], qn_ref[...]).astype(BF16)
        yield
        for g in range(len(POOL_WINDOWS)):
            cols = slice(g * POOL_GROUP, (g + 1) * POOL_GROUP)
            m_ref[rows, cols] = (_dot(pooled[g], wp_ref[g]) * ps_ref[:, cols]).astype(BF16)
        q = _dot(qa, wq_ref[...])
        yield
        if use_rope:
            pos = pl.multiple_of(pos0, sub)
            cq, sq = (r[pl.ds(pos, sub), :] for r in q_tabs)
        for hd in range(N_HEADS):
            slab = q[:, hd * HEAD_SLAB:(hd + 1) * HEAD_SLAB]
            if use_rope:
                swapped = q[:, QK_WIDTH + hd * HEAD_SLAB:QK_WIDTH + (hd + 1) * HEAD_SLAB]
                slab = slab * cq + swapped * sq
            else:
                slab = slab * Q_SCALE
            q_ref[rows, hd * HEAD_SLAB:(hd + 1) * HEAD_SLAB] = slab.astype(BF16)
        ckv = _rms(proj[:, Q_LORA_RANK:KR_OFF], kvn_ref[...])
        kr = proj[:, KR_OFF:U_OFF]
        if emit_cache:
            ckv_ref[rows, :] = ckv
            kr_ref[rows, :] = kr[:, :QK_ROPE_DIM]
        if use_rope:
            kr = _rope_slab(kr, *(r[pl.ds(pos, sub), :] for r in k_tabs))
        ckv16 = ckv.astype(BF16)
        kin = jnp.concatenate([ckv16, kr.astype(BF16)], axis=1)
        yield
        k_ref[rows, :] = _dot(kin, wk_ref[...]).astype(BF16)
        v_ref[rows, :] = _dot(ckv16, wv_ref[...]).astype(BF16)

    _round_robin([sub_tile(j) for j in range(n_sub)])


def _stage1_call(x2d, mod3, mod_row_fn, weights, rope_tabs, seq, tm, sub, emit_cache):
    n_tok = x2d.shape[0]
    assert tm % sub == 0 and seq % sub == 0 and sub >= 2 * SUBLANES
    assert seq % tm == 0 or tm % seq == 0
    use_rope = rope_tabs is not None
    halo = seq > sub
    assert halo or sub == seq
    halo_blocks = n_tok // POOL_HALO
    per_tile = tm // POOL_HALO
    in_specs = [pl.BlockSpec((tm, D_MODEL), lambda i: (i, 0))]
    args = [x2d]
    if halo:
        in_specs += [
            pl.BlockSpec((POOL_HALO, D_MODEL), lambda i: (jnp.maximum(i * per_tile - 1, 0), 0)),
            pl.BlockSpec((POOL_HALO, D_MODEL),
                         lambda i: (jnp.minimum((i + 1) * per_tile, halo_blocks - 1), 0)),
        ]
        args += [x2d, x2d]
    in_specs += [pl.BlockSpec((1, 6, D_MODEL), lambda i: (mod_row_fn(i), 0, 0))]
    in_specs += [_const_spec(w.shape) for w in weights]
    args += [mod3] + list(weights)
    if use_rope:
        in_specs += [_resident_spec((seq, LANES))] * len(rope_tabs)
        args += list(rope_tabs)
    out_specs = [
        pl.BlockSpec((tm, QK_WIDTH), lambda i: (i, 0)),
        pl.BlockSpec((tm, QK_WIDTH), lambda i: (i, 0)),
        pl.BlockSpec((tm, MLA_WIDTH), lambda i: (i, 0)),
        pl.BlockSpec((tm, POOL_WIDTH), lambda i: (i, 0)),
    ]
    out_shape = [
        jax.ShapeDtypeStruct((n_tok, QK_WIDTH), BF16),
        jax.ShapeDtypeStruct((n_tok, QK_WIDTH), BF16),
        jax.ShapeDtypeStruct((n_tok, MLA_WIDTH), BF16),
        jax.ShapeDtypeStruct((n_tok, POOL_WIDTH), BF16),
    ]
    if emit_cache:
        out_specs += [pl.BlockSpec((tm, KV_LORA_RANK), lambda i: (i, 0)),
                      pl.BlockSpec((tm, QK_ROPE_DIM), lambda i: (i, 0))]
        out_shape += [jax.ShapeDtypeStruct((n_tok, KV_LORA_RANK), F32),
                      jax.ShapeDtypeStruct((n_tok, QK_ROPE_DIM), F32)]
    stencil_rows = sub + 3 * POOL_HALO
    return pl.pallas_call(
        functools.partial(_stage1_kernel, use_rope=use_rope, halo=halo, emit_cache=emit_cache,
                          tm=tm, sub=sub, seq=seq),
        grid=(n_tok // tm,),
        in_specs=in_specs,
        out_specs=out_specs,
        out_shape=out_shape,
        scratch_shapes=[pltpu.VMEM((tm // sub, stencil_rows, POOL_WIDTH), F32)] * 3,
        compiler_params=pltpu.CompilerParams(dimension_semantics=("parallel",),
                                             vmem_limit_bytes=VMEM_LIMIT),
        name="stage1_rope" if use_rope else "stage1_ctx",
    )(*args)


def _kv_up_kernel(ckv_ref, kr_ref, wk_ref, wv_ref, k_ref, v_ref):
    ckv16 = ckv_ref[...].astype(BF16)
    kr = kr_ref[...].astype(BF16)
    pad = jnp.zeros((kr.shape[0], LANES - QK_ROPE_DIM), BF16)
    kin = jnp.concatenate([ckv16, kr, pad], axis=1)
    k_ref[...] = _dot(kin, wk_ref[...]).astype(BF16)
    v_ref[...] = _dot(ckv16, wv_ref[...]).astype(BF16)


def _kv_up_call(ckv2d, kr2d, wk, wv, tm):
    n_tok = ckv2d.shape[0]
    return pl.pallas_call(
        _kv_up_kernel,
        grid=(n_tok // tm,),
        in_specs=[
            pl.BlockSpec((tm, KV_LORA_RANK), lambda i: (i, 0)),
            pl.BlockSpec((tm, QK_ROPE_DIM), lambda i: (i, 0)),
            _const_spec(wk.shape), _const_spec(wv.shape),
        ],
        out_specs=[pl.BlockSpec((tm, QK_WIDTH), lambda i: (i, 0)),
                   pl.BlockSpec((tm, MLA_WIDTH), lambda i: (i, 0))],
        out_shape=[jax.ShapeDtypeStruct((n_tok, QK_WIDTH), BF16),
                   jax.ShapeDtypeStruct((n_tok, MLA_WIDTH), BF16)],
        compiler_params=pltpu.CompilerParams(dimension_semantics=("parallel",)),
        name="kv_up",
    )(ckv2d, kr2d, wk, wv)


def _qk(qh, kh):
    return lax.dot_general(qh, kh, (((1,), (1,)), ((), ())), preferred_element_type=F32)


def _attn_kernel(*refs, heads, tq, q_subs, with_cache, n_casts):
    n_in = 5 if with_cache else 3
    if with_cache:
        q_ref, k_ref, v_ref, kc_ref, vc_ref = refs[:n_in]
    else:
        q_ref, k_ref, v_ref = refs[:n_in]
    o_ref = refs[n_in + n_casts]
    for src, dst in zip(refs[n_in:n_in + n_casts], refs[n_in + n_casts + 1:]):
        dst[...] = src[...].astype(BF16)

    def scores(qi, hd):
        qs = slice(hd * HEAD_SLAB, (hd + 1) * HEAD_SLAB)
        qh = q_ref[0, qi * tq:(qi + 1) * tq, qs]
        s = _qk(qh, k_ref[0, :, qs])
        sc = _qk(qh, kc_ref[0, :, qs]) if with_cache else None
        return s, sc

    def values(ref, hd):
        pair = ref[0, :, (hd // 2) * HEAD_SLAB:(hd // 2 + 1) * HEAD_SLAB]
        own = (lax.broadcasted_iota(jnp.int32, pair.shape, 1) < V_HEAD_DIM) == (hd % 2 == 0)
        return jnp.where(own, pair, jnp.ones_like(pair))

    units = [(qi, hd) for qi in range(q_subs) for hd in range(heads)]
    nxt = scores(*units[0])
    for idx, (qi, hd) in enumerate(units):
        s, sc = nxt
        if idx + 1 < len(units):
            nxt = scores(*units[idx + 1])
        m = jnp.max(s, axis=-1, keepdims=True)
        if with_cache:
            m = jnp.maximum(m, jnp.max(sc, axis=-1, keepdims=True))
        o = _dot(jnp.exp2(s - m).astype(BF16), values(v_ref, hd))
        if with_cache:
            o = o + _dot(jnp.exp2(sc - m).astype(BF16), values(vc_ref, hd))
        if hd % 2 == 0:
            o_even = o
        else:
            low = lax.broadcasted_iota(jnp.int32, o.shape, 1) < V_HEAD_DIM
            num = jnp.where(low, o_even, o)
            den = pltpu.roll(jnp.where(low, o, o_even), V_HEAD_DIM, 1)
            o_ref[0, qi * tq:(qi + 1) * tq, (hd - 1) * V_HEAD_DIM:(hd + 1) * V_HEAD_DIM] = (
                num * (1.0 / den)).astype(BF16)


def _attn_call(q3, k3, v3, cache, tq, q_subs, heads, casts=()):
    b, s, _ = q3.shape
    nk = k3.shape[1]
    assert heads % 2 == 0 and N_HEADS % heads == 0 and s % (tq * q_subs) == 0
    groups = N_HEADS // heads
    rows = tq * q_subs
    steps_i = s // rows
    n_steps = b * groups * steps_i
    qw, vw = heads * HEAD_SLAB, heads * V_HEAD_DIM
    in_specs = [
        pl.BlockSpec((1, rows, qw), lambda bi, g, i: (bi, i, g)),
        pl.BlockSpec((1, nk, qw), lambda bi, g, i: (bi, 0, g)),
        pl.BlockSpec((1, nk, vw), lambda bi, g, i: (bi, 0, g)),
    ]
    args = [q3, k3, v3]
    if cache is not None:
        kc3, vc3 = cache
        nc = kc3.shape[1]
        in_specs += [pl.BlockSpec((1, nc, qw), lambda bi, g, i: (bi, 0, g)),
                     pl.BlockSpec((1, nc, vw), lambda bi, g, i: (bi, 0, g))]
        args += [kc3, vc3]
    out_specs = [pl.BlockSpec((1, rows, vw), lambda bi, g, i: (bi, i, g))]
    out_shape = [jax.ShapeDtypeStruct((b, s, MLA_WIDTH), BF16)]
    for w in casts:
        assert w.shape[0] % (n_steps * 2 * SUBLANES) == 0
        blk = (w.shape[0] // n_steps, w.shape[1])
        spec = pl.BlockSpec(blk, lambda bi, g, i: ((bi * groups + g) * steps_i + i, 0))
        in_specs.append(spec)
        out_specs.append(spec)
        args.append(w)
        out_shape.append(jax.ShapeDtypeStruct(w.shape, BF16))
    outs = pl.pallas_call(
        functools.partial(_attn_kernel, heads=heads, tq=tq, q_subs=q_subs, with_cache=cache is not None,
                          n_casts=len(casts)),
        grid=(b, groups, steps_i),
        in_specs=in_specs,
        out_specs=out_specs,
        out_shape=out_shape,
        compiler_params=pltpu.CompilerParams(
            dimension_semantics=("parallel", "parallel", "parallel"),
            vmem_limit_bytes=VMEM_LIMIT),
        name="attn_lat" if cache is not None else "attn_ctx",
    )(*args)
    return outs if casts else outs[0]


def _post_kernel(x_ref, a_ref, m_ref, mod_ref, wo_ref, norm2_ref, w1_ref, w2_ref, fn_ref, o_ref,
                 *, tm, sub, ff_chunk):
    def sub_tile(j):
        rows = slice(j * sub, (j + 1) * sub)
        cat = jnp.concatenate([a_ref[rows, :], m_ref[rows, :]], axis=1)
        y = x_ref[rows, :] + mod_ref[0, 2:3, :] * _dot(cat, wo_ref[...])
        gain = norm2_ref[...] * (1.0 + mod_ref[0, 4:5, :])
        h2 = (_rms(y, gain) + mod_ref[0, 3:4, :]).astype(BF16)
        yield
        acc = jnp.zeros((sub, D_MODEL), F32)
        for c in range(D_FF // ff_chunk):
            a = _dot(h2, w1_ref[:, c * ff_chunk:(c + 1) * ff_chunk])
            a = jnp.square(jnp.maximum(a, 0.0)).astype(BF16)
            acc = acc + _dot(a, w2_ref[c * ff_chunk:(c + 1) * ff_chunk, :])
            yield
        x2 = y + mod_ref[0, 5:6, :] * acc
        o_ref[rows, :] = _rms(x2, fn_ref[...])

    _round_robin([sub_tile(j) for j in range(tm // sub)])


def _post_call(x2d, attn2d, mixed2d, mod3, mod_row_fn, weights, tm, sub):
    n_tok = x2d.shape[0]
    assert tm % sub == 0
    w_o, norm2, w1, w2, final_norm = weights
    in_specs = [
        pl.BlockSpec((tm, D_MODEL), lambda i: (i, 0)),
        pl.BlockSpec((tm, MLA_WIDTH), lambda i: (i, 0)),
        pl.BlockSpec((tm, POOL_WIDTH), lambda i: (i, 0)),
        pl.BlockSpec((1, 6, D_MODEL), lambda i: (mod_row_fn(i), 0, 0)),
        _resident_spec(w_o.shape), _resident_spec(norm2.shape), _resident_spec(w1.shape),
        _resident_spec(w2.shape), _resident_spec(final_norm.shape),
    ]
    return pl.pallas_call(
        functools.partial(_post_kernel, tm=tm, sub=sub, ff_chunk=1024),
        grid=(n_tok // tm,),
        in_specs=in_specs,
        out_specs=pl.BlockSpec((tm, D_MODEL), lambda i: (i, 0)),
        out_shape=jax.ShapeDtypeStruct((n_tok, D_MODEL), F32),
        compiler_params=pltpu.CompilerParams(dimension_semantics=("parallel",),
                                             vmem_limit_bytes=VMEM_LIMIT),
        name="post",
    )(x2d, attn2d, mixed2d, mod3, w_o, norm2, w1, w2, final_norm)


def _pack_w_in(w_in):
    qa = w_in[:, :Q_LORA_RANK + KV_LORA_RANK]
    kr = w_in[:, KR_OFF:KR_OFF + QK_ROPE_DIM]
    u = w_in[:, KR_OFF + QK_ROPE_DIM:]
    pad = jnp.zeros((D_MODEL, LANES - QK_ROPE_DIM), w_in.dtype)
    return jnp.concatenate([qa, kr, pad, u], axis=1).astype(BF16)


def _pack_wq(w_qb, with_swapped):
    w = w_qb.reshape(Q_LORA_RANK, N_HEADS, QK_DIM)
    tail = ((0, 0), (0, 0), (0, HEAD_SLAB - QK_DIM))
    plain = jnp.pad(w, tail).reshape(Q_LORA_RANK, QK_WIDTH)
    if not with_swapped:
        return plain.astype(BF16)
    r = w[:, :, QK_NOPE_DIM:].reshape(Q_LORA_RANK, N_HEADS, 2, 2, N_FREQ)
    sw = jnp.stack([-r[:, :, :, 1, :], r[:, :, :, 0, :]], axis=3).reshape(Q_LORA_RANK, N_HEADS, QK_ROPE_DIM)
    sw = jnp.concatenate([jnp.zeros((Q_LORA_RANK, N_HEADS, QK_NOPE_DIM), w.dtype), sw], axis=2)
    swapped = jnp.pad(sw, tail).reshape(Q_LORA_RANK, QK_WIDTH)
    return jnp.concatenate([plain, swapped], axis=1).astype(BF16)


def _pack_wkv(w_kvb):
    w = w_kvb.reshape(KV_LORA_RANK, N_HEADS, QK_NOPE_DIM + V_HEAD_DIM)
    wk_nope = jnp.pad(w[:, :, :QK_NOPE_DIM], ((0, 0), (0, 0), (0, HEAD_SLAB - QK_NOPE_DIM)))
    sel = jnp.pad(jnp.eye(QK_ROPE_DIM, dtype=w_kvb.dtype),
                  ((0, 0), (QK_NOPE_DIM, HEAD_SLAB - QK_DIM)))
    sel = jnp.broadcast_to(sel[:, None, :], (QK_ROPE_DIM, N_HEADS, HEAD_SLAB))
    zero = jnp.zeros((LANES - QK_ROPE_DIM, N_HEADS, HEAD_SLAB), w_kvb.dtype)
    wk = jnp.concatenate([wk_nope, sel, zero], axis=0).reshape(KV_LORA_RANK + LANES, QK_WIDTH)
    wv = w[:, :, QK_NOPE_DIM:].reshape(KV_LORA_RANK, MLA_WIDTH)
    return wk.astype(BF16), wv.astype(BF16)


def _rope_tables(n_tokens):
    f = np.float32
    rows = n_tokens // GRID_W
    row = np.repeat(np.arange(rows, dtype=f), GRID_W)
    col = np.tile(np.arange(GRID_W, dtype=f), rows)
    inv_freq = (1.0 / (f(ROPE_THETA) ** (np.arange(N_FREQ, dtype=f) * f(2.0) / f(AXIS_ROPE)))).astype(f)
    cos_r, sin_r = np.cos(row[:, None] * inv_freq), np.sin(row[:, None] * inv_freq)
    cos_c, sin_c = np.cos(col[:, None] * inv_freq), np.sin(col[:, None] * inv_freq)
    z = np.zeros_like(cos_r)
    c32 = np.concatenate([cos_r, cos_r, cos_c, cos_c], axis=1)
    s32 = np.concatenate([sin_r, sin_r, sin_c, sin_c], axis=1)
    up32 = np.concatenate([z, sin_r, z, sin_c], axis=1)
    dn32 = np.concatenate([-sin_r, z, -sin_c, z], axis=1)

    def place(t32, off, fill):
        left = np.full((n_tokens, off), fill, f)
        right = np.zeros((n_tokens, LANES - off - QK_ROPE_DIM), f)
        return np.concatenate([left, t32.astype(f), right], axis=1)

    q_tabs = (place(c32 * f(Q_SCALE), QK_NOPE_DIM, Q_SCALE), place(s32 * f(Q_SCALE), QK_NOPE_DIM, 0.0))
    k_tabs = (place(c32, 0, 0.0), place(up32, 0, 0.0), place(dn32, 0, 0.0))
    return q_tabs + k_tabs


def kernel(x_prompt, x_sample, cache_ckv, cache_krope, c, c_ctx, w_ada, b_ada, norm1, w_in, q_norm,
           w_qb, kv_norm, w_kvb, w_pool, pool_scale, w_o, norm2, w1, w2, final_norm):
    batch, seq, _ = x_prompt.shape
    dec_batch, dec_seq, _ = x_sample.shape
    depth = w_in.shape[0]
    past = cache_ckv.shape[2]
    assert depth == 1, "single trunk layer"
    assert dec_batch + 1 <= MOD_ROWS

    c_all = jnp.concatenate(
        [c, c_ctx[None, :], jnp.zeros((MOD_ROWS - dec_batch - 1, D_MODEL), F32)], axis=0)
    mod3 = _mod_call(c_all, w_ada[0], b_ada[0][None, :]).reshape(MOD_ROWS, 6, D_MODEL)

    wk, wv = _pack_wkv(w_kvb[0])

    def s1_weights(with_swapped):
        return (norm1[0][None, :], _pack_w_in(w_in[0]), q_norm[0][None, :],
                _pack_wq(w_qb[0], with_swapped), kv_norm[0][None, :], wk, wv,
                w_pool[0].astype(BF16), pool_scale[0][None, :])

    xc = x_prompt.reshape(batch * seq, D_MODEL)
    ctx_row = lambda i: dec_batch
    qc, kc, vc, mc, ckv_new, kr_new = _stage1_call(xc, mod3, ctx_row, s1_weights(False), None, seq,
                                                   S1_TILE, S1_SUB, True)
    attn_c = _attn_call(qc.reshape(batch, seq, QK_WIDTH), kc.reshape(batch, seq, QK_WIDTH),
                        vc.reshape(batch, seq, MLA_WIDTH), None, seq, 1, N_HEADS)

    xl = x_sample.reshape(dec_batch * dec_seq, D_MODEL)
    lat_row = lambda i: i // (dec_seq // S1_TILE)
    ql, kl, vl, ml = _stage1_call(xl, mod3, lat_row, s1_weights(True), _rope_tables(dec_seq), dec_seq,
                                  S1_TILE, S1_SUB, False)
    kcache, vcache = _kv_up_call(cache_ckv[:, 0].reshape(dec_batch * past, KV_LORA_RANK),
                                 cache_krope[:, 0].reshape(dec_batch * past, QK_ROPE_DIM), wk, wv, 512)
    attn_l, w1_16, w2_16 = _attn_call(
        ql.reshape(dec_batch, dec_seq, QK_WIDTH), kl.reshape(dec_batch, dec_seq, QK_WIDTH),
        vl.reshape(dec_batch, dec_seq, MLA_WIDTH),
        (kcache.reshape(dec_batch, past, QK_WIDTH), vcache.reshape(dec_batch, past, MLA_WIDTH)),
        LAT_Q_TILE, LAT_Q_SUBS, N_HEADS, casts=(w1[0], w2[0]))

    post_weights = (w_o[0].astype(BF16), norm2[0][None, :], w1_16, w2_16, final_norm[None, :])
    y_prompt = _post_call(xc, attn_c.reshape(batch * seq, MLA_WIDTH), mc, mod3, ctx_row,
                          post_weights, POST_TILE, POST_SUB)
    post_row = lambda i: i // (dec_seq // POST_TILE)
    y_sample = _post_call(xl, attn_l.reshape(dec_batch * dec_seq, MLA_WIDTH), ml, mod3, post_row,
                          post_weights, POST_TILE, POST_SUB)

    return (y_prompt.reshape(batch, seq, D_MODEL), y_sample.reshape(dec_batch, dec_seq, D_MODEL),
            ckv_new.reshape(batch, 1, seq, KV_LORA_RANK), kr_new.reshape(batch, 1, seq, QK_ROPE_DIM))
```

```python
import functools
import math

import jax
import jax.numpy as jnp
import numpy as np
from jax import lax
from jax.experimental import pallas as pl
from jax.experimental.pallas import tpu as pltpu

D_MODEL = 1024
N_HEADS = 8
QK_NOPE_DIM = 64
QK_ROPE_DIM = 32
V_HEAD_DIM = 64
QK_DIM = QK_NOPE_DIM + QK_ROPE_DIM
Q_LORA_RANK = 256
KV_LORA_RANK = 128
MLA_WIDTH = N_HEADS * V_HEAD_DIM
POOL_WIDTH = D_MODEL - MLA_WIDTH
POOL_WINDOWS = (2, 4, 8, 16)
POOL_GROUP = POOL_WIDTH // len(POOL_WINDOWS)
D_FF = 4 * D_MODEL
GRID_W = 64
ROPE_THETA = 10000.0
AXIS_ROPE = QK_ROPE_DIM // 2
N_FREQ = AXIS_ROPE // 2
EPS = 1e-6
ATTN_SCALE = 1.0 / math.sqrt(QK_DIM)
Q_SCALE = ATTN_SCALE * math.log2(math.e)

LANES = 128
SUBLANES = 8
HEAD_SLAB = LANES
QK_WIDTH = N_HEADS * HEAD_SLAB
MOD_ROWS = 16
POOL_HALO = max(POOL_WINDOWS) // 2
assert POOL_HALO == SUBLANES
KR_OFF = Q_LORA_RANK + KV_LORA_RANK
U_OFF = KR_OFF + LANES
PROJ_COLS = U_OFF + POOL_WIDTH
VMEM_LIMIT = 56 * 1024 * 1024
S1_TILE = 1024
S1_SUB = 256
LAT_Q_TILE = 512
LAT_Q_SUBS = 1
POST_TILE = 1024
POST_SUB = 256
F32 = jnp.float32
BF16 = jnp.bfloat16


def _dot(a, b):
    return jnp.dot(a, b, preferred_element_type=F32)


def _rms(x, g):
    return x * lax.rsqrt(jnp.mean(x * x, axis=-1, keepdims=True) + EPS) * g


def _round_robin(phased):
    active = list(phased)
    while active:
        for gen in list(active):
            if next(gen, "done") == "done":
                active.remove(gen)


def _const_spec(shape):
    nd = len(shape)
    return pl.BlockSpec(shape, lambda i: (0,) * nd)


def _resident_spec(shape):
    nd = len(shape)
    return pl.BlockSpec(shape, lambda i: (0,) * nd, pipeline_mode=pl.Buffered(1))


def _mod_kernel(c_ref, w_ref, b_ref, o_ref):
    c = c_ref[...]
    s = (c * jax.nn.sigmoid(c)).astype(BF16)
    o_ref[...] = _dot(s, w_ref[...].astype(BF16)) + b_ref[...]


def _mod_call(c_all, w_ada, b_ada):
    tn = 1024
    n = w_ada.shape[1]
    return pl.pallas_call(
        _mod_kernel,
        grid=(n // tn,),
        in_specs=[
            pl.BlockSpec((MOD_ROWS, D_MODEL), lambda j: (0, 0)),
            pl.BlockSpec((D_MODEL, tn), lambda j: (0, j)),
            pl.BlockSpec((1, tn), lambda j: (0, j)),
        ],
        out_specs=pl.BlockSpec((MOD_ROWS, tn), lambda j: (0, j)),
        out_shape=jax.ShapeDtypeStruct((MOD_ROWS, n), F32),
        compiler_params=pltpu.CompilerParams(dimension_semantics=("arbitrary",)),
        name="mod",
    )(c_all, w_ada, b_ada)


def _rope_slab(slab, c, s_up, s_dn):
    return (slab * c + pltpu.roll(slab, N_FREQ, 1) * s_up
            + pltpu.roll(slab, LANES - N_FREQ, 1) * s_dn)


def _inv_counts(w, tm, at_seq_start, at_seq_end):
    half = w // 2
    r = lax.broadcasted_iota(jnp.int32, (SUBLANES, LANES), 0)
    first = jnp.where(at_seq_start, (r + half) - jnp.maximum(r - half, 0), w)
    last = jnp.where(at_seq_end, jnp.minimum(half, SUBLANES - r) + half, w)
    mid = jnp.full((tm - 2 * SUBLANES, LANES), 1.0 / w, F32)
    return jnp.concatenate([1.0 / first.astype(F32), mid, 1.0 / last.astype(F32)], axis=0)


def _pool_windows(e_ref, c2_ref, c4_ref, tm, at_seq_start, at_seq_end):
    rows = tm + 2 * POOL_HALO
    g = [slice(i * POOL_GROUP, (i + 1) * POOL_GROUP) for i in range(len(POOL_WINDOWS))]
    wide = slice(POOL_GROUP, POOL_WIDTH)
    wider = slice(2 * POOL_GROUP, POOL_WIDTH)
    zeros = jnp.zeros((SUBLANES, POOL_WIDTH), F32)
    c2_ref[rows:, :] = zeros
    c4_ref[rows:, :] = zeros
    c2_ref[0:rows, wide] = e_ref[0:rows, wide] + e_ref[1:rows + 1, wide]
    c4_ref[0:rows, wider] = c2_ref[0:rows, wider] + c2_ref[2:rows + 2, wider]
    c8 = c4_ref[0:rows, g[3]] + c4_ref[4:rows + 4, g[3]]
    sums = [
        e_ref[7:7 + tm, g[0]] + e_ref[8:8 + tm, g[0]],
        c2_ref[6:6 + tm, g[1]] + c2_ref[8:8 + tm, g[1]],
        c4_ref[4:4 + tm, g[2]] + c4_ref[8:8 + tm, g[2]],
        c8[0:tm] + c8[8:8 + tm],
    ]
    return [(sums[i] * _inv_counts(w, tm, at_seq_start, at_seq_end)
             - e_ref[POOL_HALO:POOL_HALO + tm, g[i]]).astype(BF16)
            for i, w in enumerate(POOL_WINDOWS)]


def _stage1_kernel(*refs, use_rope, halo, emit_cache, tm, sub, seq):
    it = iter(refs)
    x_ref = next(it)
    if halo:
        xp_ref, xn_ref = next(it), next(it)
    (mod_ref, norm1_ref, w_in_ref, qn_ref, wq_ref, kvn_ref, wk_ref, wv_ref, wp_ref,
     ps_ref) = (next(it) for _ in range(10))
    if use_rope:
        q_tabs = [next(it) for _ in range(2)]
        k_tabs = [next(it) for _ in range(3)]
    q_ref, k_ref, v_ref, m_ref = (next(it) for _ in range(4))
    if emit_cache:
        ckv_ref, kr_ref = next(it), next(it)
    e_ref, c2_ref, c4_ref = next(it), next(it), next(it)

    n_sub = tm // sub

    def sub_tile(j):
        r0 = j * sub
        rows = slice(r0, r0 + sub)
        pos0 = (pl.program_id(0) * tm + r0) % seq
        at_seq_start = pos0 == 0
        at_seq_end = pos0 + sub == seq
        e, c2, c4 = e_ref.at[j], c2_ref.at[j], c4_ref.at[j]

        if halo:
            before = xp_ref[...] if j == 0 else x_ref[r0 - POOL_HALO:r0, :]
            after = xn_ref[...] if j == n_sub - 1 else x_ref[r0 + sub:r0 + sub + POOL_HALO, :]
            x = jnp.concatenate([before, x_ref[rows, :], after], axis=0)
        else:
            x = x_ref[rows, :]
        gain = norm1_ref[...] * (1.0 + mod_ref[0, 1:2, :])
        h = (_rms(x, gain) + mod_ref[0, 0:1, :]).astype(BF16)
        yield
        proj = _dot(h, w_in_ref[...])
        yield
        u = proj[:, U_OFF:]
        zeros = jnp.zeros((POOL_HALO, POOL_WIDTH), F32)
        if halo:
            e[0:POOL_HALO, :] = jnp.where(at_seq_start, 0.0, u[0:POOL_HALO])
            e[POOL_HALO:POOL_HALO + sub, :] = u[POOL_HALO:POOL_HALO + sub]
            e[POOL_HALO + sub:2 * POOL_HALO + sub, :] = jnp.where(at_seq_end, 0.0, u[POOL_HALO + sub:])
            proj = proj[POOL_HALO:POOL_HALO + sub]
        else:
            e[0:POOL_HALO, :] = zeros
            e[POOL_HALO:POOL_HALO + sub, :] = u
            e[POOL_HALO + sub:2 * POOL_HALO + sub, :] = zeros
        e[2 * POOL_HALO + sub:, :] = zeros
        pooled = _pool_windows(e, c2, c4, sub, at_seq_start, at_seq_end)
        qa = _rms(proj[:, :Q_LORA_RANK], qn_ref[...]).astype(BF16)
        yield
        for g in range(len(POOL_WINDOWS)):
            cols = slice(g * POOL_GROUP, (g + 1) * POOL_GROUP)
            m_ref[rows, cols] = (_dot(pooled[g], wp_ref[g]) * ps_ref[:, cols]).astype(BF16)
        q = _dot(qa, wq_ref[...])
        yield
        if use_rope:
            pos = pl.multiple_of(pos0, sub)
            cq, sq = (r[pl.ds(pos, sub), :] for r in q_tabs)
        for hd in range(N_HEADS):
            slab = q[:, hd * HEAD_SLAB:(hd + 1) * HEAD_SLAB]
            if use_rope:
                swapped = q[:, QK_WIDTH + hd * HEAD_SLAB:QK_WIDTH + (hd + 1) * HEAD_SLAB]
                slab = slab * cq + swapped * sq
            else:
                slab = slab * Q_SCALE
            q_ref[rows, hd * HEAD_SLAB:(hd + 1) * HEAD_SLAB] = slab.astype(BF16)
        ckv = _rms(proj[:, Q_LORA_RANK:KR_OFF], kvn_ref[...])
        kr = proj[:, KR_OFF:U_OFF]
        if emit_cache:
            ckv_ref[rows, :] = ckv
            kr_ref[rows, :] = kr[:, :QK_ROPE_DIM]
        if use_rope:
            kr = _rope_slab(kr, *(r[pl.ds(pos, sub), :] for r in k_tabs))
        ckv16 = ckv.astype(BF16)
        kin = jnp.concatenate([ckv16, kr.astype(BF16)], axis=1)
        yield
        k_ref[rows, :] = _dot(kin, wk_ref[...]).astype(BF16)
        v_ref[rows, :] = _dot(ckv16, wv_ref[...]).astype(BF16)

    _round_robin([sub_tile(j) for j in range(n_sub)])


def _stage1_call(x2d, mod3, mod_row_fn, weights, rope_tabs, seq, tm, sub, emit_cache):
    n_tok = x2d.shape[0]
    assert tm % sub == 0 and seq % sub == 0 and sub >= 2 * SUBLANES
    assert seq % tm == 0 or tm % seq == 0
    use_rope = rope_tabs is not None
    halo = seq > sub
    assert halo or sub == seq
    halo_blocks = n_tok // POOL_HALO
    per_tile = tm // POOL_HALO
    in_specs = [pl.BlockSpec((tm, D_MODEL), lambda i: (i, 0))]
    args = [x2d]
    if halo:
        in_specs += [
            pl.BlockSpec((POOL_HALO, D_MODEL), lambda i: (jnp.maximum(i * per_tile - 1, 0), 0)),
            pl.BlockSpec((POOL_HALO, D_MODEL),
                         lambda i: (jnp.minimum((i + 1) * per_tile, halo_blocks - 1), 0)),
        ]
        args += [x2d, x2d]
    in_specs += [pl.BlockSpec((1, 6, D_MODEL), lambda i: (mod_row_fn(i), 0, 0))]
    in_specs += [_const_spec(w.shape) for w in weights]
    args += [mod3] + list(weights)
    if use_rope:
        in_specs += [_resident_spec((seq, LANES))] * len(rope_tabs)
        args += list(rope_tabs)
    out_specs = [
        pl.BlockSpec((tm, QK_WIDTH), lambda i: (i, 0)),
        pl.BlockSpec((tm, QK_WIDTH), lambda i: (i, 0)),
        pl.BlockSpec((tm, MLA_WIDTH), lambda i: (i, 0)),
        pl.BlockSpec((tm, POOL_WIDTH), lambda i: (i, 0)),
    ]
    out_shape = [
        jax.ShapeDtypeStruct((n_tok, QK_WIDTH), BF16),
        jax.ShapeDtypeStruct((n_tok, QK_WIDTH), BF16),
        jax.ShapeDtypeStruct((n_tok, MLA_WIDTH), BF16),
        jax.ShapeDtypeStruct((n_tok, POOL_WIDTH), BF16),
    ]
    if emit_cache:
        out_specs += [pl.BlockSpec((tm, KV_LORA_RANK), lambda i: (i, 0)),
                      pl.BlockSpec((tm, QK_ROPE_DIM), lambda i: (i, 0))]
        out_shape += [jax.ShapeDtypeStruct((n_tok, KV_LORA_RANK), F32),
                      jax.ShapeDtypeStruct((n_tok, QK_ROPE_DIM), F32)]
    stencil_rows = sub + 3 * POOL_HALO
    return pl.pallas_call(
        functools.partial(_stage1_kernel, use_rope=use_rope, halo=halo, emit_cache=emit_cache,
                          tm=tm, sub=sub, seq=seq),
        grid=(n_tok // tm,),
        in_specs=in_specs,
        out_specs=out_specs,
        out_shape=out_shape,
        scratch_shapes=[pltpu.VMEM((tm // sub, stencil_rows, POOL_WIDTH), F32)] * 3,
        compiler_params=pltpu.CompilerParams(dimension_semantics=("parallel",),
                                             vmem_limit_bytes=VMEM_LIMIT),
        name="stage1_rope" if use_rope else "stage1_ctx",
    )(*args)


def _kv_up_kernel(ckv_ref, kr_ref, wk_ref, wv_ref, k_ref, v_ref):
    ckv16 = ckv_ref[...].astype(BF16)
    kr = kr_ref[...].astype(BF16)
    pad = jnp.zeros((kr.shape[0], LANES - QK_ROPE_DIM), BF16)
    kin = jnp.concatenate([ckv16, kr, pad], axis=1)
    k_ref[...] = _dot(kin, wk_ref[...]).astype(BF16)
    v_ref[...] = _dot(ckv16, wv_ref[...]).astype(BF16)


def _kv_up_call(ckv2d, kr2d, wk, wv, tm):
    n_tok = ckv2d.shape[0]
    return pl.pallas_call(
        _kv_up_kernel,
        grid=(n_tok // tm,),
        in_specs=[
            pl.BlockSpec((tm, KV_LORA_RANK), lambda i: (i, 0)),
            pl.BlockSpec((tm, QK_ROPE_DIM), lambda i: (i, 0)),
            _const_spec(wk.shape), _const_spec(wv.shape),
        ],
        out_specs=[pl.BlockSpec((tm, QK_WIDTH), lambda i: (i, 0)),
                   pl.BlockSpec((tm, MLA_WIDTH), lambda i: (i, 0))],
        out_shape=[jax.ShapeDtypeStruct((n_tok, QK_WIDTH), BF16),
                   jax.ShapeDtypeStruct((n_tok, MLA_WIDTH), BF16)],
        compiler_params=pltpu.CompilerParams(dimension_semantics=("parallel",)),
        name="kv_up",
    )(ckv2d, kr2d, wk, wv)


def _qk(qh, kh):
    return lax.dot_general(qh, kh, (((1,), (1,)), ((), ())), preferred_element_type=F32)


def _attn_kernel(*refs, heads, tq, q_subs, with_cache, n_casts):
    n_in = 5 if with_cache else 3
    if with_cache:
        q_ref, k_ref, v_ref, kc_ref, vc_ref = refs[:n_in]
    else:
        q_ref, k_ref, v_ref = refs[:n_in]
    o_ref = refs[n_in + n_casts]
    for src, dst in zip(refs[n_in:n_in + n_casts], refs[n_in + n_casts + 1:]):
        dst[...] = src[...].astype(BF16)

    def scores(qi, hd):
        qs = slice(hd * HEAD_SLAB, (hd + 1) * HEAD_SLAB)
        qh = q_ref[0, qi * tq:(qi + 1) * tq, qs]
        s = _qk(qh, k_ref[0, :, qs])
        sc = _qk(qh, kc_ref[0, :, qs]) if with_cache else None
        return s, sc

    def values(ref, hd):
        pair = ref[0, :, (hd // 2) * HEAD_SLAB:(hd // 2 + 1) * HEAD_SLAB]
        own = (lax.broadcasted_iota(jnp.int32, pair.shape, 1) < V_HEAD_DIM) == (hd % 2 == 0)
        return jnp.where(own, pair, jnp.ones_like(pair))

    units = [(qi, hd) for qi in range(q_subs) for hd in range(heads)]
    nxt = scores(*units[0])
    for idx, (qi, hd) in enumerate(units):
        s, sc = nxt
        if idx + 1 < len(units):
            nxt = scores(*units[idx + 1])
        m = jnp.max(s, axis=-1, keepdims=True)
        if with_cache:
            m = jnp.maximum(m, jnp.max(sc, axis=-1, keepdims=True))
        o = _dot(jnp.exp2(s - m).astype(BF16), values(v_ref, hd))
        if with_cache:
            o = o + _dot(jnp.exp2(sc - m).astype(BF16), values(vc_ref, hd))
        if hd % 2 == 0:
            o_even = o
        else:
            low = lax.broadcasted_iota(jnp.int32, o.shape, 1) < V_HEAD_DIM
            num = jnp.where(low, o_even, o)
            den = pltpu.roll(jnp.where(low, o, o_even), V_HEAD_DIM, 1)
            o_ref[0, qi * tq:(qi + 1) * tq, (hd - 1) * V_HEAD_DIM:(hd + 1) * V_HEAD_DIM] = (
                num * (1.0 / den)).astype(BF16)


def _attn_call(q3, k3, v3, cache, tq, q_subs, heads, casts=()):
    b, s, _ = q3.shape
    nk = k3.shape[1]
    assert heads % 2 == 0 and N_HEADS % heads == 0 and s % (tq * q_subs) == 0
    groups = N_HEADS // heads
    rows = tq * q_subs
    steps_i = s // rows
    n_steps = b * groups * steps_i
    qw, vw = heads * HEAD_SLAB, heads * V_HEAD_DIM
    in_specs = [
        pl.BlockSpec((1, rows, qw), lambda bi, g, i: (bi, i, g)),
        pl.BlockSpec((1, nk, qw), lambda bi, g, i: (bi, 0, g)),
        pl.BlockSpec((1, nk, vw), lambda bi, g, i: (bi, 0, g)),
    ]
    args = [q3, k3, v3]
    if cache is not None:
        kc3, vc3 = cache
        nc = kc3.shape[1]
        in_specs += [pl.BlockSpec((1, nc, qw), lambda bi, g, i: (bi, 0, g)),
                     pl.BlockSpec((1, nc, vw), lambda bi, g, i: (bi, 0, g))]
        args += [kc3, vc3]
    out_specs = [pl.BlockSpec((1, rows, vw), lambda bi, g, i: (bi, i, g))]
    out_shape = [jax.ShapeDtypeStruct((b, s, MLA_WIDTH), BF16)]
    for w in casts:
        assert w.shape[0] % (n_steps * 2 * SUBLANES) == 0
        blk = (w.shape[0] // n_steps, w.shape[1])
        spec = pl.BlockSpec(blk, lambda bi, g, i: ((bi * groups + g) * steps_i + i, 0))
        in_specs.append(spec)
        out_specs.append(spec)
        args.append(w)
        out_shape.append(jax.ShapeDtypeStruct(w.shape, BF16))
    outs = pl.pallas_call(
        functools.partial(_attn_kernel, heads=heads, tq=tq, q_subs=q_subs, with_cache=cache is not None,
                          n_casts=len(casts)),
        grid=(b, groups, steps_i),
        in_specs=in_specs,
        out_specs=out_specs,
        out_shape=out_shape,
        compiler_params=pltpu.CompilerParams(
            dimension_semantics=("parallel", "parallel", "parallel"),
            vmem_limit_bytes=VMEM_LIMIT),
        name="attn_lat" if cache is not None else "attn_ctx",
    )(*args)
    return outs if casts else outs[0]


def _post_kernel(x_ref, a_ref, m_ref, mod_ref, wo_ref, norm2_ref, w1_ref, w2_ref, fn_ref, o_ref,
                 *, tm, sub, ff_chunk):
    def sub_tile(j):
        rows = slice(j * sub, (j + 1) * sub)
        cat = jnp.concatenate([a_ref[rows, :], m_ref[rows, :]], axis=1)
        y = x_ref[rows, :] + mod_ref[0, 2:3, :] * _dot(cat, wo_ref[...])
        gain = norm2_ref[...] * (1.0 + mod_ref[0, 4:5, :])
        h2 = (_rms(y, gain) + mod_ref[0, 3:4, :]).astype(BF16)
        yield
        acc = jnp.zeros((sub, D_MODEL), F32)
        for c in range(D_FF // ff_chunk):
            a = _dot(h2, w1_ref[:, c * ff_chunk:(c + 1) * ff_chunk])
            a = jnp.square(jnp.maximum(a, 0.0)).astype(BF16)
            acc = acc + _dot(a, w2_ref[c * ff_chunk:(c + 1) * ff_chunk, :])
            yield
        x2 = y + mod_ref[0, 5:6, :] * acc
        o_ref[rows, :] = _rms(x2, fn_ref[...])

    _round_robin([sub_tile(j) for j in range(tm // sub)])


def _post_call(x2d, attn2d, mixed2d, mod3, mod_row_fn, weights, tm, sub):
    n_tok = x2d.shape[0]
    assert tm % sub == 0
    w_o, norm2, w1, w2, final_norm = weights
    in_specs = [
        pl.BlockSpec((tm, D_MODEL), lambda i: (i, 0)),
        pl.BlockSpec((tm, MLA_WIDTH), lambda i: (i, 0)),
        pl.BlockSpec((tm, POOL_WIDTH), lambda i: (i, 0)),
        pl.BlockSpec((1, 6, D_MODEL), lambda i: (mod_row_fn(i), 0, 0)),
        _resident_spec(w_o.shape), _resident_spec(norm2.shape), _resident_spec(w1.shape),
        _resident_spec(w2.shape), _resident_spec(final_norm.shape),
    ]
    return pl.pallas_call(
        functools.partial(_post_kernel, tm=tm, sub=sub, ff_chunk=1024),
        grid=(n_tok // tm,),
        in_specs=in_specs,
        out_specs=pl.BlockSpec((tm, D_MODEL), lambda i: (i, 0)),
        out_shape=jax.ShapeDtypeStruct((n_tok, D_MODEL), F32),
        compiler_params=pltpu.CompilerParams(dimension_semantics=("parallel",),
                                             vmem_limit_bytes=VMEM_LIMIT),
        name="post",
    )(x2d, attn2d, mixed2d, mod3, w_o, norm2, w1, w2, final_norm)


def _pack_w_in(w_in):
    qa = w_in[:, :Q_LORA_RANK + KV_LORA_RANK]
    kr = w_in[:, KR_OFF:KR_OFF + QK_ROPE_DIM]
    u = w_in[:, KR_OFF + QK_ROPE_DIM:]
    pad = jnp.zeros((D_MODEL, LANES - QK_ROPE_DIM), w_in.dtype)
    return jnp.concatenate([qa, kr, pad, u], axis=1).astype(BF16)


def _pack_wq(w_qb, with_swapped):
    w = w_qb.reshape(Q_LORA_RANK, N_HEADS, QK_DIM)
    tail = ((0, 0), (0, 0), (0, HEAD_SLAB - QK_DIM))
    plain = jnp.pad(w, tail).reshape(Q_LORA_RANK, QK_WIDTH)
    if not with_swapped:
        return plain.astype(BF16)
    r = w[:, :, QK_NOPE_DIM:].reshape(Q_LORA_RANK, N_HEADS, 2, 2, N_FREQ)
    sw = jnp.stack([-r[:, :, :, 1, :], r[:, :, :, 0, :]], axis=3).reshape(Q_LORA_RANK, N_HEADS, QK_ROPE_DIM)
    sw = jnp.concatenate([jnp.zeros((Q_LORA_RANK, N_HEADS, QK_NOPE_DIM), w.dtype), sw], axis=2)
    swapped = jnp.pad(sw, tail).reshape(Q_LORA_RANK, QK_WIDTH)
    return jnp.concatenate([plain, swapped], axis=1).astype(BF16)


def _pack_wkv(w_kvb):
    w = w_kvb.reshape(KV_LORA_RANK, N_HEADS, QK_NOPE_DIM + V_HEAD_DIM)
    wk_nope = jnp.pad(w[:, :, :QK_NOPE_DIM], ((0, 0), (0, 0), (0, HEAD_SLAB - QK_NOPE_DIM)))
    sel = jnp.pad(jnp.eye(QK_ROPE_DIM, dtype=w_kvb.dtype),
                  ((0, 0), (QK_NOPE_DIM, HEAD_SLAB - QK_DIM)))
    sel = jnp.broadcast_to(sel[:, None, :], (QK_ROPE_DIM, N_HEADS, HEAD_SLAB))
    zero = jnp.zeros((LANES - QK_ROPE_DIM, N_HEADS, HEAD_SLAB), w_kvb.dtype)
    wk = jnp.concatenate([wk_nope, sel, zero], axis=0).reshape(KV_LORA_RANK + LANES, QK_WIDTH)
    wv = w[:, :, QK_NOPE_DIM:].reshape(KV_LORA_RANK, MLA_WIDTH)
    return wk.astype(BF16), wv.astype(BF16)


def _rope_tables(n_tokens):
    f = np.float32
    rows = n_tokens // GRID_W
    row = np.repeat(np.arange(rows, dtype=f), GRID_W)
    col = np.tile(np.arange(GRID_W, dtype=f), rows)
    inv_freq = (1.0 / (f(ROPE_THETA) ** (np.arange(N_FREQ, dtype=f) * f(2.0) / f(AXIS_ROPE)))).astype(f)
    cos_r, sin_r = np.cos(row[:, None] * inv_freq), np.sin(row[:, None] * inv_freq)
    cos_c, sin_c = np.cos(col[:, None] * inv_freq), np.sin(col[:, None] * inv_freq)
    z = np.zeros_like(cos_r)
    c32 = np.concatenate([cos_r, cos_r, cos_c, cos_c], axis=1)
    s32 = np.concatenate([sin_r, sin_r, sin_c, sin_c], axis=1)
    up32 = np.concatenate([z, sin_r, z, sin_c], axis=1)
    dn32 = np.concatenate([-sin_r, z, -sin_c, z], axis=1)

    def place(t32, off, fill):
        left = np.full((n_tokens, off), fill, f)
        right = np.zeros((n_tokens, LANES - off - QK_ROPE_DIM), f)
        return np.concatenate([left, t32.astype(f), right], axis=1)

    q_tabs = (place(c32 * f(Q_SCALE), QK_NOPE_DIM, Q_SCALE), place(s32 * f(Q_SCALE), QK_NOPE_DIM, 0.0))
    k_tabs = (place(c32, 0, 0.0), place(up32, 0, 0.0), place(dn32, 0, 0.0))
    return q_tabs + k_tabs


def kernel(x_prompt, x_sample, cache_ckv, cache_krope, c, c_ctx, w_ada, b_ada, norm1, w_in, q_norm,
           w_qb, kv_norm, w_kvb, w_pool, pool_scale, w_o, norm2, w1, w2, final_norm):
    batch, seq, _ = x_prompt.shape
    dec_batch, dec_seq, _ = x_sample.shape
    depth = w_in.shape[0]
    past = cache_ckv.shape[2]
    assert depth == 1, "single trunk layer"
    assert dec_batch + 1 <= MOD_ROWS

    c_all = jnp.concatenate(
        [c, c_ctx[None, :], jnp.zeros((MOD_ROWS - dec_batch - 1, D_MODEL), F32)], axis=0)
    mod3 = _mod_call(c_all, w_ada[0], b_ada[0][None, :]).reshape(MOD_ROWS, 6, D_MODEL)

    wk, wv = _pack_wkv(w_kvb[0])

    def s1_weights(with_swapped):
        return (norm1[0][None, :], _pack_w_in(w_in[0]), q_norm[0][None, :],
                _pack_wq(w_qb[0], with_swapped), kv_norm[0][None, :], wk, wv,
                w_pool[0].astype(BF16), pool_scale[0][None, :])

    xc = x_prompt.reshape(batch * seq, D_MODEL)
    ctx_row = lambda i: dec_batch
    qc, kc, vc, mc, ckv_new, kr_new = _stage1_call(xc, mod3, ctx_row, s1_weights(False), None, seq,
                                                   S1_TILE, S1_SUB, True)
    attn_c = _attn_call(qc.reshape(batch, seq, QK_WIDTH), kc.reshape(batch, seq, QK_WIDTH),
                        vc.reshape(batch, seq, MLA_WIDTH), None, seq, 1, N_HEADS)

    xl = x_sample.reshape(dec_batch * dec_seq, D_MODEL)
    lat_row = lambda i: i // (dec_seq // S1_TILE)
    ql, kl, vl, ml = _stage1_call(xl, mod3, lat_row, s1_weights(True), _rope_tables(dec_seq), dec_seq,
                                  S1_TILE, S1_SUB, False)
    kcache, vcache = _kv_up_call(cache_ckv[:, 0].reshape(dec_batch * past, KV_LORA_RANK),
                                 cache_krope[:, 0].reshape(dec_batch * past, QK_ROPE_DIM), wk, wv, 512)
    attn_l, w1_16, w2_16 = _attn_call(
        ql.reshape(dec_batch, dec_seq, QK_WIDTH), kl.reshape(dec_batch, dec_seq, QK_WIDTH),
        vl.reshape(dec_batch, dec_seq, MLA_WIDTH),
        (kcache.reshape(dec_batch, past, QK_WIDTH), vcache.reshape(dec_batch, past, MLA_WIDTH)),
        LAT_Q_TILE, LAT_Q_SUBS, N_HEADS, casts=(w1[0], w2[0]))

    post_weights = (w_o[0].astype(BF16), norm2[0][None, :], w1_16, w2_16, final_norm[None, :])
    y_prompt = _post_call(xc, attn_c.reshape(batch * seq, MLA_WIDTH), mc, mod3, ctx_row,
                          post_weights, POST_TILE, POST_SUB)
    post_row = lambda i: i // (dec_seq // POST_TILE)
    y_sample = _post_call(xl, attn_l.reshape(dec_batch * dec_seq, MLA_WIDTH), ml, mod3, post_row,
                          post_weights, POST_TILE, POST_SUB)

    return (y_prompt.reshape(batch, seq, D_MODEL), y_sample.reshape(dec_batch, dec_seq, D_MODEL),
            ckv_new.reshape(batch, 1, seq, KV_LORA_RANK), kr_new.reshape(batch, 1, seq, QK_ROPE_DIM))
```

```python
import functools
import math

import jax
import jax.numpy as jnp
import numpy as np
from jax import lax
from jax.experimental import pallas as pl
from jax.experimental.pallas import tpu as pltpu

D_MODEL = 1024
N_HEADS = 8
QK_NOPE_DIM = 64
QK_ROPE_DIM = 32
V_HEAD_DIM = 64
QK_DIM = QK_NOPE_DIM + QK_ROPE_DIM
Q_LORA_RANK = 256
KV_LORA_RANK = 128
MLA_WIDTH = N_HEADS * V_HEAD_DIM
POOL_WIDTH = D_MODEL - MLA_WIDTH
POOL_WINDOWS = (2, 4, 8, 16)
POOL_GROUP = POOL_WIDTH // len(POOL_WINDOWS)
D_FF = 4 * D_MODEL
GRID_W = 64
ROPE_THETA = 10000.0
AXIS_ROPE = QK_ROPE_DIM // 2
N_FREQ = AXIS_ROPE // 2
EPS = 1e-6
ATTN_SCALE = 1.0 / math.sqrt(QK_DIM)
Q_SCALE = ATTN_SCALE * math.log2(math.e)

LANES = 128
SUBLANES = 8
HEAD_SLAB = LANES
QK_WIDTH = N_HEADS * HEAD_SLAB
MOD_ROWS = 16
POOL_HALO = max(POOL_WINDOWS) // 2
assert POOL_HALO == SUBLANES
KR_OFF = Q_LORA_RANK + KV_LORA_RANK
U_OFF = KR_OFF + LANES
PROJ_COLS = U_OFF + POOL_WIDTH
VMEM_LIMIT = 56 * 1024 * 1024
S1_TILE = 1024
S1_SUB = 256
LAT_Q_TILE = 512
LAT_Q_SUBS = 1
POST_TILE = 1024
POST_SUB = 256
F32 = jnp.float32
BF16 = jnp.bfloat16


def _dot(a, b):
    return jnp.dot(a, b, preferred_element_type=F32)


def _rms(x, g):
    return x * lax.rsqrt(jnp.mean(x * x, axis=-1, keepdims=True) + EPS) * g


def _round_robin(phased):
    active = list(phased)
    while active:
        for gen in list(active):
            if next(gen, "done") == "done":
                active.remove(gen)


def _const_spec(shape):
    nd = len(shape)
    return pl.BlockSpec(shape, lambda i: (0,) * nd)


def _resident_spec(shape):
    nd = len(shape)
    return pl.BlockSpec(shape, lambda i: (0,) * nd, pipeline_mode=pl.Buffered(1))


def _mod_kernel(c_ref, w_ref, b_ref, o_ref):
    c = c_ref[...]
    s = (c * jax.nn.sigmoid(c)).astype(BF16)
    o_ref[...] = _dot(s, w_ref[...].astype(BF16)) + b_ref[...]


def _mod_call(c_all, w_ada, b_ada):
    tn = 1024
    n = w_ada.shape[1]
    return pl.pallas_call(
        _mod_kernel,
        grid=(n // tn,),
        in_specs=[
            pl.BlockSpec((MOD_ROWS, D_MODEL), lambda j: (0, 0)),
            pl.BlockSpec((D_MODEL, tn), lambda j: (0, j)),
            pl.BlockSpec((1, tn), lambda j: (0, j)),
        ],
        out_specs=pl.BlockSpec((MOD_ROWS, tn), lambda j: (0, j)),
        out_shape=jax.ShapeDtypeStruct((MOD_ROWS, n), F32),
        compiler_params=pltpu.CompilerParams(dimension_semantics=("arbitrary",)),
        name="mod",
    )(c_all, w_ada, b_ada)


def _rope_slab(slab, c, s_up, s_dn):
    return (slab * c + pltpu.roll(slab, N_FREQ, 1) * s_up
            + pltpu.roll(slab, LANES - N_FREQ, 1) * s_dn)


def _inv_counts(w, tm, at_seq_start, at_seq_end):
    half = w // 2
    r = lax.broadcasted_iota(jnp.int32, (SUBLANES, LANES), 0)
    first = jnp.where(at_seq_start, (r + half) - jnp.maximum(r - half, 0), w)
    last = jnp.where(at_seq_end, jnp.minimum(half, SUBLANES - r) + half, w)
    mid = jnp.full((tm - 2 * SUBLANES, LANES), 1.0 / w, F32)
    return jnp.concatenate([1.0 / first.astype(F32), mid, 1.0 / last.astype(F32)], axis=0)


def _pool_windows(e_ref, c2_ref, c4_ref, tm, at_seq_start, at_seq_end):
    rows = tm + 2 * POOL_HALO
    g = [slice(i * POOL_GROUP, (i + 1) * POOL_GROUP) for i in range(len(POOL_WINDOWS))]
    wide = slice(POOL_GROUP, POOL_WIDTH)
    wider = slice(2 * POOL_GROUP, POOL_WIDTH)
    zeros = jnp.zeros((SUBLANES, POOL_WIDTH), F32)
    c2_ref[rows:, :] = zeros
    c4_ref[rows:, :] = zeros
    c2_ref[0:rows, wide] = e_ref[0:rows, wide] + e_ref[1:rows + 1, wide]
    c4_ref[0:rows, wider] = c2_ref[0:rows, wider] + c2_ref[2:rows + 2, wider]
    c8 = c4_ref[0:rows, g[3]] + c4_ref[4:rows + 4, g[3]]
    sums = [
        e_ref[7:7 + tm, g[0]] + e_ref[8:8 + tm, g[0]],
        c2_ref[6:6 + tm, g[1]] + c2_ref[8:8 + tm, g[1]],
        c4_ref[4:4 + tm, g[2]] + c4_ref[8:8 + tm, g[2]],
        c8[0:tm] + c8[8:8 + tm],
    ]
    return [(sums[i] * _inv_counts(w, tm, at_seq_start, at_seq_end)
             - e_ref[POOL_HALO:POOL_HALO + tm, g[i]]).astype(BF16)
            for i, w in enumerate(POOL_WINDOWS)]


def _stage1_kernel(*refs, use_rope, halo, emit_cache, self_attend, tm, sub, seq):
    it = iter(refs)
    x_ref = next(it)
    if halo:
        xp_ref, xn_ref = next(it), next(it)
    (mod_ref, norm1_ref, w_in_ref, qn_ref, wq_ref, kvn_ref, wk_ref, wv_ref, wp_ref,
     ps_ref) = (next(it) for _ in range(10))
    if use_rope:
        q_tabs = [next(it) for _ in range(2)]
        k_tabs = [next(it) for _ in range(3)]
    if self_attend:
        a_ref, m_ref = next(it), next(it)
    else:
        q_ref, k_ref, v_ref, m_ref = (next(it) for _ in range(4))
    if emit_cache:
        ckv_ref, kr_ref = next(it), next(it)
    e_ref, c2_ref, c4_ref = next(it), next(it), next(it)

    n_sub = tm // sub

    def sub_tile(j):
        r0 = j * sub
        rows = slice(r0, r0 + sub)
        pos0 = (pl.program_id(0) * tm + r0) % seq
        at_seq_start = pos0 == 0
        at_seq_end = pos0 + sub == seq
        e, c2, c4 = e_ref.at[j], c2_ref.at[j], c4_ref.at[j]

        if halo:
            before = xp_ref[...] if j == 0 else x_ref[r0 - POOL_HALO:r0, :]
            after = xn_ref[...] if j == n_sub - 1 else x_ref[r0 + sub:r0 + sub + POOL_HALO, :]
            x = jnp.concatenate([before, x_ref[rows, :], after], axis=0)
        else:
            x = x_ref[rows, :]
        gain = norm1_ref[...] * (1.0 + mod_ref[0, 1:2, :])
        h = (_rms(x, gain) + mod_ref[0, 0:1, :]).astype(BF16)
        yield
        proj = _dot(h, w_in_ref[...])
        yield
        u = proj[:, U_OFF:]
        zeros = jnp.zeros((POOL_HALO, POOL_WIDTH), F32)
        if halo:
            e[0:POOL_HALO, :] = jnp.where(at_seq_start, 0.0, u[0:POOL_HALO])
            e[POOL_HALO:POOL_HALO + sub, :] = u[POOL_HALO:POOL_HALO + sub]
            e[POOL_HALO + sub:2 * POOL_HALO + sub, :] = jnp.where(at_seq_end, 0.0, u[POOL_HALO + sub:])
            proj = proj[POOL_HALO:POOL_HALO + sub]
        else:
            e[0:POOL_HALO, :] = zeros
            e[POOL_HALO:POOL_HALO + sub, :] = u
            e[POOL_HALO + sub:2 * POOL_HALO + sub, :] = zeros
        e[2 * POOL_HALO + sub:, :] = zeros
        pooled = _pool_windows(e, c2, c4, sub, at_seq_start, at_seq_end)
        qa = _rms(proj[:, :Q_LORA_RANK], qn_ref[...]).astype(BF16)
        yield
        for g in range(len(POOL_WINDOWS)):
            cols = slice(g * POOL_GROUP, (g + 1) * POOL_GROUP)
            m_ref[rows, cols] = (_dot(pooled[g], wp_ref[g]) * ps_ref[:, cols]).astype(BF16)
        q = _dot(qa, wq_ref[...])
        yield
        if use_rope:
            pos = pl.multiple_of(pos0, sub)
            cq, sq = (r[pl.ds(pos, sub), :] for r in q_tabs)
        q_heads = []
        for hd in range(N_HEADS):
            slab = q[:, hd * HEAD_SLAB:(hd + 1) * HEAD_SLAB]
            if use_rope:
                swapped = q[:, QK_WIDTH + hd * HEAD_SLAB:QK_WIDTH + (hd + 1) * HEAD_SLAB]
                slab = slab * cq + swapped * sq
            else:
                slab = slab * Q_SCALE
            if self_attend:
                q_heads.append(slab.astype(BF16))
            else:
                q_ref[rows, hd * HEAD_SLAB:(hd + 1) * HEAD_SLAB] = slab.astype(BF16)
        ckv = _rms(proj[:, Q_LORA_RANK:KR_OFF], kvn_ref[...])
        kr = proj[:, KR_OFF:U_OFF]
        if emit_cache:
            ckv_ref[rows, :] = ckv
            kr_ref[rows, :] = kr[:, :QK_ROPE_DIM]
        if use_rope:
            kr = _rope_slab(kr, *(r[pl.ds(pos, sub), :] for r in k_tabs))
        ckv16 = ckv.astype(BF16)
        kin = jnp.concatenate([ckv16, kr.astype(BF16)], axis=1)
        yield
        keys = _dot(kin, wk_ref[...]).astype(BF16)
        vals = _dot(ckv16, wv_ref[...]).astype(BF16)
        if not self_attend:
            k_ref[rows, :] = keys
            v_ref[rows, :] = vals
            return
        yield

        def write(unit, out):
            a_ref[rows, (unit[0] - 1) * V_HEAD_DIM:(unit[0] + 1) * V_HEAD_DIM] = out

        _attention(
            [(hd,) for hd in range(N_HEADS)],
            q_of=lambda u: q_heads[u[0]],
            keys_of=[lambda u: keys[:, u[0] * HEAD_SLAB:(u[0] + 1) * HEAD_SLAB]],
            value_pairs_of=[lambda u: vals[:, (u[0] // 2) * HEAD_SLAB:(u[0] // 2 + 1) * HEAD_SLAB]],
            write_pair=write)

    _round_robin([sub_tile(j) for j in range(n_sub)])


def _stage1_call(x2d, mod3, mod_row_fn, weights, rope_tabs, seq, tm, sub, emit_cache, self_attend):
    n_tok = x2d.shape[0]
    assert not self_attend or sub == seq
    assert tm % sub == 0 and seq % sub == 0 and sub >= 2 * SUBLANES
    assert seq % tm == 0 or tm % seq == 0
    use_rope = rope_tabs is not None
    halo = seq > sub
    assert halo or sub == seq
    halo_blocks = n_tok // POOL_HALO
    per_tile = tm // POOL_HALO
    in_specs = [pl.BlockSpec((tm, D_MODEL), lambda i: (i, 0))]
    args = [x2d]
    if halo:
        in_specs += [
            pl.BlockSpec((POOL_HALO, D_MODEL), lambda i: (jnp.maximum(i * per_tile - 1, 0), 0)),
            pl.BlockSpec((POOL_HALO, D_MODEL),
                         lambda i: (jnp.minimum((i + 1) * per_tile, halo_blocks - 1), 0)),
        ]
        args += [x2d, x2d]
    in_specs += [pl.BlockSpec((1, 6, D_MODEL), lambda i: (mod_row_fn(i), 0, 0))]
    in_specs += [_const_spec(w.shape) for w in weights]
    args += [mod3] + list(weights)
    if use_rope:
        in_specs += [_resident_spec((seq, LANES))] * len(rope_tabs)
        args += list(rope_tabs)
    widths = [MLA_WIDTH] if self_attend else [QK_WIDTH, QK_WIDTH, MLA_WIDTH]
    widths.append(POOL_WIDTH)
    out_specs = [pl.BlockSpec((tm, w), lambda i: (i, 0)) for w in widths]
    out_shape = [jax.ShapeDtypeStruct((n_tok, w), BF16) for w in widths]
    if emit_cache:
        out_specs += [pl.BlockSpec((tm, KV_LORA_RANK), lambda i: (i, 0)),
                      pl.BlockSpec((tm, QK_ROPE_DIM), lambda i: (i, 0))]
        out_shape += [jax.ShapeDtypeStruct((n_tok, KV_LORA_RANK), F32),
                      jax.ShapeDtypeStruct((n_tok, QK_ROPE_DIM), F32)]
    stencil_rows = sub + 3 * POOL_HALO
    return pl.pallas_call(
        functools.partial(_stage1_kernel, use_rope=use_rope, halo=halo, emit_cache=emit_cache,
                          self_attend=self_attend, tm=tm, sub=sub, seq=seq),
        grid=(n_tok // tm,),
        in_specs=in_specs,
        out_specs=out_specs,
        out_shape=out_shape,
        scratch_shapes=[pltpu.VMEM((tm // sub, stencil_rows, POOL_WIDTH), F32)] * 3,
        compiler_params=pltpu.CompilerParams(dimension_semantics=("parallel",),
                                             vmem_limit_bytes=VMEM_LIMIT),
        name="stage1_rope" if use_rope else "stage1_ctx",
    )(*args)


def _kv_up_kernel(ckv_ref, kr_ref, wk_ref, wv_ref, k_ref, v_ref):
    ckv16 = ckv_ref[...].astype(BF16)
    kr = kr_ref[...].astype(BF16)
    pad = jnp.zeros((kr.shape[0], LANES - QK_ROPE_DIM), BF16)
    kin = jnp.concatenate([ckv16, kr, pad], axis=1)
    k_ref[...] = _dot(kin, wk_ref[...]).astype(BF16)
    v_ref[...] = _dot(ckv16, wv_ref[...]).astype(BF16)


def _kv_up_call(ckv2d, kr2d, wk, wv, tm):
    n_tok = ckv2d.shape[0]
    return pl.pallas_call(
        _kv_up_kernel,
        grid=(n_tok // tm,),
        in_specs=[
            pl.BlockSpec((tm, KV_LORA_RANK), lambda i: (i, 0)),
            pl.BlockSpec((tm, QK_ROPE_DIM), lambda i: (i, 0)),
            _const_spec(wk.shape), _const_spec(wv.shape),
        ],
        out_specs=[pl.BlockSpec((tm, QK_WIDTH), lambda i: (i, 0)),
                   pl.BlockSpec((tm, MLA_WIDTH), lambda i: (i, 0))],
        out_shape=[jax.ShapeDtypeStruct((n_tok, QK_WIDTH), BF16),
                   jax.ShapeDtypeStruct((n_tok, MLA_WIDTH), BF16)],
        compiler_params=pltpu.CompilerParams(dimension_semantics=("parallel",)),
        name="kv_up",
    )(ckv2d, kr2d, wk, wv)


def _qk(qh, kh):
    return lax.dot_general(qh, kh, (((1,), (1,)), ((), ())), preferred_element_type=F32)


def _attention(units, q_of, keys_of, value_pairs_of, write_pair):
    def scores(unit):
        qh = q_of(unit)
        return [_qk(qh, k(unit)) for k in keys_of]

    def values(get, unit):
        vp = get(unit)
        own = (lax.broadcasted_iota(jnp.int32, vp.shape, 1) < V_HEAD_DIM) == (unit[-1] % 2 == 0)
        return jnp.where(own, vp, jnp.ones_like(vp))

    nxt = scores(units[0])
    for idx, unit in enumerate(units):
        blocks = nxt
        if idx + 1 < len(units):
            nxt = scores(units[idx + 1])
        m = functools.reduce(jnp.maximum, [jnp.max(s, axis=-1, keepdims=True) for s in blocks])
        o = functools.reduce(jnp.add, [_dot(jnp.exp2(s - m).astype(BF16), values(v, unit))
                                       for s, v in zip(blocks, value_pairs_of)])
        if unit[-1] % 2 == 0:
            o_even = o
        else:
            low = lax.broadcasted_iota(jnp.int32, o.shape, 1) < V_HEAD_DIM
            num = jnp.where(low, o_even, o)
            den = pltpu.roll(jnp.where(low, o, o_even), V_HEAD_DIM, 1)
            write_pair(unit, (num * (1.0 / den)).astype(BF16))


def _attn_kernel(*refs, heads, tq, q_subs, with_cache, n_casts):
    n_in = 5 if with_cache else 3
    if with_cache:
        q_ref, k_ref, v_ref, kc_ref, vc_ref = refs[:n_in]
    else:
        q_ref, k_ref, v_ref = refs[:n_in]
    o_ref = refs[n_in + n_casts]
    for src, dst in zip(refs[n_in:n_in + n_casts], refs[n_in + n_casts + 1:]):
        dst[...] = src[...].astype(BF16)

    slab = lambda hd: slice(hd * HEAD_SLAB, (hd + 1) * HEAD_SLAB)
    pair = lambda hd: slice((hd // 2) * HEAD_SLAB, (hd // 2 + 1) * HEAD_SLAB)
    units = [(qi, hd) for qi in range(q_subs) for hd in range(heads)]

    def write(unit, out):
        qi, hd = unit
        o_ref[0, qi * tq:(qi + 1) * tq, (hd - 1) * V_HEAD_DIM:(hd + 1) * V_HEAD_DIM] = out

    _attention(
        units,
        q_of=lambda u: q_ref[0, u[0] * tq:(u[0] + 1) * tq, slab(u[1])],
        keys_of=[lambda u: k_ref[0, :, slab(u[1])]] + ([lambda u: kc_ref[0, :, slab(u[1])]] if with_cache else []),
        value_pairs_of=[lambda u: v_ref[0, :, pair(u[1])]] + ([lambda u: vc_ref[0, :, pair(u[1])]] if with_cache else []),
        write_pair=write)


def _attn_call(q3, k3, v3, cache, tq, q_subs, heads, casts=()):
    b, s, _ = q3.shape
    nk = k3.shape[1]
    assert heads % 2 == 0 and N_HEADS % heads == 0 and s % (tq * q_subs) == 0
    groups = N_HEADS // heads
    rows = tq * q_subs
    steps_i = s // rows
    n_steps = b * groups * steps_i
    qw, vw = heads * HEAD_SLAB, heads * V_HEAD_DIM
    in_specs = [
        pl.BlockSpec((1, rows, qw), lambda bi, g, i: (bi, i, g)),
        pl.BlockSpec((1, nk, qw), lambda bi, g, i: (bi, 0, g)),
        pl.BlockSpec((1, nk, vw), lambda bi, g, i: (bi, 0, g)),
    ]
    args = [q3, k3, v3]
    if cache is not None:
        kc3, vc3 = cache
        nc = kc3.shape[1]
        in_specs += [pl.BlockSpec((1, nc, qw), lambda bi, g, i: (bi, 0, g)),
                     pl.BlockSpec((1, nc, vw), lambda bi, g, i: (bi, 0, g))]
        args += [kc3, vc3]
    out_specs = [pl.BlockSpec((1, rows, vw), lambda bi, g, i: (bi, i, g))]
    out_shape = [jax.ShapeDtypeStruct((b, s, MLA_WIDTH), BF16)]
    for w in casts:
        assert w.shape[0] % (n_steps * 2 * SUBLANES) == 0
        blk = (w.shape[0] // n_steps, w.shape[1])
        spec = pl.BlockSpec(blk, lambda bi, g, i: ((bi * groups + g) * steps_i + i, 0))
        in_specs.append(spec)
        out_specs.append(spec)
        args.append(w)
        out_shape.append(jax.ShapeDtypeStruct(w.shape, BF16))
    outs = pl.pallas_call(
        functools.partial(_attn_kernel, heads=heads, tq=tq, q_subs=q_subs, with_cache=cache is not None,
                          n_casts=len(casts)),
        grid=(b, groups, steps_i),
        in_specs=in_specs,
        out_specs=out_specs,
        out_shape=out_shape,
        compiler_params=pltpu.CompilerParams(
            dimension_semantics=("parallel", "parallel", "parallel"),
            vmem_limit_bytes=VMEM_LIMIT),
        name="attn_lat" if cache is not None else "attn_ctx",
    )(*args)
    return outs if casts else outs[0]


def _post_kernel(x_ref, a_ref, m_ref, mod_ref, wo_ref, norm2_ref, w1_ref, w2_ref, fn_ref, o_ref,
                 *, tm, sub, ff_chunk):
    def sub_tile(j):
        rows = slice(j * sub, (j + 1) * sub)
        cat = jnp.concatenate([a_ref[rows, :], m_ref[rows, :]], axis=1)
        y = x_ref[rows, :] + mod_ref[0, 2:3, :] * _dot(cat, wo_ref[...])
        gain = norm2_ref[...] * (1.0 + mod_ref[0, 4:5, :])
        h2 = (_rms(y, gain) + mod_ref[0, 3:4, :]).astype(BF16)
        yield
        acc = jnp.zeros((sub, D_MODEL), F32)
        for c in range(D_FF // ff_chunk):
            a = _dot(h2, w1_ref[:, c * ff_chunk:(c + 1) * ff_chunk])
            a = jnp.square(jnp.maximum(a, 0.0)).astype(BF16)
            acc = acc + _dot(a, w2_ref[c * ff_chunk:(c + 1) * ff_chunk, :])
            yield
        x2 = y + mod_ref[0, 5:6, :] * acc
        o_ref[rows, :] = _rms(x2, fn_ref[...])

    _round_robin([sub_tile(j) for j in range(tm // sub)])


def _post_call(x2d, attn2d, mixed2d, mod3, mod_row_fn, weights, tm, sub):
    n_tok = x2d.shape[0]
    assert tm % sub == 0
    w_o, norm2, w1, w2, final_norm = weights
    in_specs = [
        pl.BlockSpec((tm, D_MODEL), lambda i: (i, 0)),
        pl.BlockSpec((tm, MLA_WIDTH), lambda i: (i, 0)),
        pl.BlockSpec((tm, POOL_WIDTH), lambda i: (i, 0)),
        pl.BlockSpec((1, 6, D_MODEL), lambda i: (mod_row_fn(i), 0, 0)),
        _resident_spec(w_o.shape), _resident_spec(norm2.shape), _resident_spec(w1.shape),
        _resident_spec(w2.shape), _resident_spec(final_norm.shape),
    ]
    return pl.pallas_call(
        functools.partial(_post_kernel, tm=tm, sub=sub, ff_chunk=1024),
        grid=(n_tok // tm,),
        in_specs=in_specs,
        out_specs=pl.BlockSpec((tm, D_MODEL), lambda i: (i, 0)),
        out_shape=jax.ShapeDtypeStruct((n_tok, D_MODEL), F32),
        compiler_params=pltpu.CompilerParams(dimension_semantics=("parallel",),
                                             vmem_limit_bytes=VMEM_LIMIT),
        name="post",
    )(x2d, attn2d, mixed2d, mod3, w_o, norm2, w1, w2, final_norm)


def _pack_w_in(w_in):
    qa = w_in[:, :Q_LORA_RANK + KV_LORA_RANK]
    kr = w_in[:, KR_OFF:KR_OFF + QK_ROPE_DIM]
    u = w_in[:, KR_OFF + QK_ROPE_DIM:]
    pad = jnp.zeros((D_MODEL, LANES - QK_ROPE_DIM), w_in.dtype)
    return jnp.concatenate([qa, kr, pad, u], axis=1).astype(BF16)


def _pack_wq(w_qb, with_swapped):
    w = w_qb.reshape(Q_LORA_RANK, N_HEADS, QK_DIM)
    tail = ((0, 0), (0, 0), (0, HEAD_SLAB - QK_DIM))
    plain = jnp.pad(w, tail).reshape(Q_LORA_RANK, QK_WIDTH)
    if not with_swapped:
        return plain.astype(BF16)
    r = w[:, :, QK_NOPE_DIM:].reshape(Q_LORA_RANK, N_HEADS, 2, 2, N_FREQ)
    sw = jnp.stack([-r[:, :, :, 1, :], r[:, :, :, 0, :]], axis=3).reshape(Q_LORA_RANK, N_HEADS, QK_ROPE_DIM)
    sw = jnp.concatenate([jnp.zeros((Q_LORA_RANK, N_HEADS, QK_NOPE_DIM), w.dtype), sw], axis=2)
    swapped = jnp.pad(sw, tail).reshape(Q_LORA_RANK, QK_WIDTH)
    return jnp.concatenate([plain, swapped], axis=1).astype(BF16)


def _pack_wkv(w_kvb):
    w = w_kvb.reshape(KV_LORA_RANK, N_HEADS, QK_NOPE_DIM + V_HEAD_DIM)
    wk_nope = jnp.pad(w[:, :, :QK_NOPE_DIM], ((0, 0), (0, 0), (0, HEAD_SLAB - QK_NOPE_DIM)))
    sel = jnp.pad(jnp.eye(QK_ROPE_DIM, dtype=w_kvb.dtype),
                  ((0, 0), (QK_NOPE_DIM, HEAD_SLAB - QK_DIM)))
    sel = jnp.broadcast_to(sel[:, None, :], (QK_ROPE_DIM, N_HEADS, HEAD_SLAB))
    zero = jnp.zeros((LANES - QK_ROPE_DIM, N_HEADS, HEAD_SLAB), w_kvb.dtype)
    wk = jnp.concatenate([wk_nope, sel, zero], axis=0).reshape(KV_LORA_RANK + LANES, QK_WIDTH)
    wv = w[:, :, QK_NOPE_DIM:].reshape(KV_LORA_RANK, MLA_WIDTH)
    return wk.astype(BF16), wv.astype(BF16)


def _rope_tables(n_tokens):
    f = np.float32
    rows = n_tokens // GRID_W
    row = np.repeat(np.arange(rows, dtype=f), GRID_W)
    col = np.tile(np.arange(GRID_W, dtype=f), rows)
    inv_freq = (1.0 / (f(ROPE_THETA) ** (np.arange(N_FREQ, dtype=f) * f(2.0) / f(AXIS_ROPE)))).astype(f)
    cos_r, sin_r = np.cos(row[:, None] * inv_freq), np.sin(row[:, None] * inv_freq)
    cos_c, sin_c = np.cos(col[:, None] * inv_freq), np.sin(col[:, None] * inv_freq)
    z = np.zeros_like(cos_r)
    c32 = np.concatenate([cos_r, cos_r, cos_c, cos_c], axis=1)
    s32 = np.concatenate([sin_r, sin_r, sin_c, sin_c], axis=1)
    up32 = np.concatenate([z, sin_r, z, sin_c], axis=1)
    dn32 = np.concatenate([-sin_r, z, -sin_c, z], axis=1)

    def place(t32, off, fill):
        left = np.full((n_tokens, off), fill, f)
        right = np.zeros((n_tokens, LANES - off - QK_ROPE_DIM), f)
        return np.concatenate([left, t32.astype(f), right], axis=1)

    q_tabs = (place(c32 * f(Q_SCALE), QK_NOPE_DIM, Q_SCALE), place(s32 * f(Q_SCALE), QK_NOPE_DIM, 0.0))
    k_tabs = (place(c32, 0, 0.0), place(up32, 0, 0.0), place(dn32, 0, 0.0))
    return q_tabs + k_tabs


def kernel(x_prompt, x_sample, cache_ckv, cache_krope, c, c_ctx, w_ada, b_ada, norm1, w_in, q_norm,
           w_qb, kv_norm, w_kvb, w_pool, pool_scale, w_o, norm2, w1, w2, final_norm):
    batch, seq, _ = x_prompt.shape
    dec_batch, dec_seq, _ = x_sample.shape
    depth = w_in.shape[0]
    past = cache_ckv.shape[2]
    assert depth == 1, "single trunk layer"
    assert dec_batch + 1 <= MOD_ROWS

    c_all = jnp.concatenate(
        [c, c_ctx[None, :], jnp.zeros((MOD_ROWS - dec_batch - 1, D_MODEL), F32)], axis=0)
    mod3 = _mod_call(c_all, w_ada[0], b_ada[0][None, :]).reshape(MOD_ROWS, 6, D_MODEL)

    wk, wv = _pack_wkv(w_kvb[0])

    def s1_weights(with_swapped):
        return (norm1[0][None, :], _pack_w_in(w_in[0]), q_norm[0][None, :],
                _pack_wq(w_qb[0], with_swapped), kv_norm[0][None, :], wk, wv,
                w_pool[0].astype(BF16), pool_scale[0][None, :])

    xc = x_prompt.reshape(batch * seq, D_MODEL)
    ctx_row = lambda i: dec_batch
    attn_c, mc, ckv_new, kr_new = _stage1_call(xc, mod3, ctx_row, s1_weights(False), None, seq,
                                               S1_TILE, seq, True, True)

    xl = x_sample.reshape(dec_batch * dec_seq, D_MODEL)
    lat_row = lambda i: i // (dec_seq // S1_TILE)
    ql, kl, vl, ml = _stage1_call(xl, mod3, lat_row, s1_weights(True), _rope_tables(dec_seq), dec_seq,
                                  S1_TILE, S1_SUB, False, False)
    kcache, vcache = _kv_up_call(cache_ckv[:, 0].reshape(dec_batch * past, KV_LORA_RANK),
                                 cache_krope[:, 0].reshape(dec_batch * past, QK_ROPE_DIM), wk, wv, 512)
    attn_l, w1_16, w2_16 = _attn_call(
        ql.reshape(dec_batch, dec_seq, QK_WIDTH), kl.reshape(dec_batch, dec_seq, QK_WIDTH),
        vl.reshape(dec_batch, dec_seq, MLA_WIDTH),
        (kcache.reshape(dec_batch, past, QK_WIDTH), vcache.reshape(dec_batch, past, MLA_WIDTH)),
        LAT_Q_TILE, LAT_Q_SUBS, N_HEADS, casts=(w1[0], w2[0]))

    post_weights = (w_o[0].astype(BF16), norm2[0][None, :], w1_16, w2_16, final_norm[None, :])
    y_prompt = _post_call(xc, attn_c, mc, mod3, ctx_row,
                          post_weights, POST_TILE, POST_SUB)
    post_row = lambda i: i // (dec_seq // POST_TILE)
    y_sample = _post_call(xl, attn_l.reshape(dec_batch * dec_seq, MLA_WIDTH), ml, mod3, post_row,
                          post_weights, POST_TILE, POST_SUB)

    return (y_prompt.reshape(batch, seq, D_MODEL), y_sample.reshape(dec_batch, dec_seq, D_MODEL),
            ckv_new.reshape(batch, 1, seq, KV_LORA_RANK), kr_new.reshape(batch, 1, seq, QK_ROPE_DIM))
```

```python
import functools
import math

import jax
import jax.numpy as jnp
import numpy as np
from jax import lax
from jax.experimental import pallas as pl
from jax.experimental.pallas import tpu as pltpu

D_MODEL = 1024
N_HEADS = 8
QK_NOPE_DIM = 64
QK_ROPE_DIM = 32
V_HEAD_DIM = 64
QK_DIM = QK_NOPE_DIM + QK_ROPE_DIM
Q_LORA_RANK = 256
KV_LORA_RANK = 128
MLA_WIDTH = N_HEADS * V_HEAD_DIM
POOL_WIDTH = D_MODEL - MLA_WIDTH
POOL_WINDOWS = (2, 4, 8, 16)
POOL_GROUP = POOL_WIDTH // len(POOL_WINDOWS)
D_FF = 4 * D_MODEL
GRID_W = 64
ROPE_THETA = 10000.0
AXIS_ROPE = QK_ROPE_DIM // 2
N_FREQ = AXIS_ROPE // 2
EPS = 1e-6
ATTN_SCALE = 1.0 / math.sqrt(QK_DIM)
Q_SCALE = ATTN_SCALE * math.log2(math.e)

LANES = 128
SUBLANES = 8
HEAD_SLAB = LANES
QK_WIDTH = N_HEADS * HEAD_SLAB
MOD_ROWS = 16
POOL_HALO = max(POOL_WINDOWS) // 2
assert POOL_HALO == SUBLANES
KR_OFF = Q_LORA_RANK + KV_LORA_RANK
U_OFF = KR_OFF + LANES
PROJ_COLS = U_OFF + POOL_WIDTH
VMEM_LIMIT = 56 * 1024 * 1024
S1_TILE = 1024
S1_SUB = 256
LAT_Q_TILE = 512
POST_TILE = 1024
POST_SUB = 256

F32 = jnp.float32
BF16 = jnp.bfloat16


def _dot(a, b):
    return jnp.dot(a, b, preferred_element_type=F32)


def _rms(x, g):
    return x * lax.rsqrt(jnp.mean(x * x, axis=-1, keepdims=True) + EPS) * g


def _round_robin(phased):
    active = list(phased)
    while active:
        for gen in list(active):
            if next(gen, "done") == "done":
                active.remove(gen)


def _const_spec(shape):
    nd = len(shape)
    return pl.BlockSpec(shape, lambda i: (0,) * nd)


def _resident_spec(shape):
    nd = len(shape)
    return pl.BlockSpec(shape, lambda i: (0,) * nd, pipeline_mode=pl.Buffered(1))


def _mod_kernel(c_ref, w_ref, b_ref, o_ref):
    c = c_ref[...]
    s = (c * jax.nn.sigmoid(c)).astype(BF16)
    o_ref[...] = _dot(s, w_ref[...].astype(BF16)) + b_ref[...]


def _mod_call(c_all, w_ada, b_ada):
    tn = 1024
    n = w_ada.shape[1]
    return pl.pallas_call(
        _mod_kernel,
        grid=(n // tn,),
        in_specs=[
            pl.BlockSpec((MOD_ROWS, D_MODEL), lambda j: (0, 0)),
            pl.BlockSpec((D_MODEL, tn), lambda j: (0, j)),
            pl.BlockSpec((1, tn), lambda j: (0, j)),
        ],
        out_specs=pl.BlockSpec((MOD_ROWS, tn), lambda j: (0, j)),
        out_shape=jax.ShapeDtypeStruct((MOD_ROWS, n), F32),
        compiler_params=pltpu.CompilerParams(dimension_semantics=("arbitrary",)),
        name="mod",
    )(c_all, w_ada, b_ada)


def _rope_slab(slab, c, s_up, s_dn):
    return (slab * c + pltpu.roll(slab, N_FREQ, 1) * s_up
            + pltpu.roll(slab, LANES - N_FREQ, 1) * s_dn)


def _inv_counts(w, tm, at_seq_start, at_seq_end):
    half = w // 2
    r = lax.broadcasted_iota(jnp.int32, (SUBLANES, LANES), 0)
    first = jnp.where(at_seq_start, (r + half) - jnp.maximum(r - half, 0), w)
    last = jnp.where(at_seq_end, jnp.minimum(half, SUBLANES - r) + half, w)
    mid = jnp.full((tm - 2 * SUBLANES, LANES), 1.0 / w, F32)
    return jnp.concatenate([1.0 / first.astype(F32), mid, 1.0 / last.astype(F32)], axis=0)


def _pool_windows(e_ref, c2_ref, c4_ref, tm, at_seq_start, at_seq_end):
    rows = tm + 2 * POOL_HALO
    g = [slice(i * POOL_GROUP, (i + 1) * POOL_GROUP) for i in range(len(POOL_WINDOWS))]
    wide = slice(POOL_GROUP, POOL_WIDTH)
    wider = slice(2 * POOL_GROUP, POOL_WIDTH)
    zeros = jnp.zeros((SUBLANES, POOL_WIDTH), F32)
    c2_ref[rows:, :] = zeros
    c4_ref[rows:, :] = zeros
    c2_ref[0:rows, wide] = e_ref[0:rows, wide] + e_ref[1:rows + 1, wide]
    c4_ref[0:rows, wider] = c2_ref[0:rows, wider] + c2_ref[2:rows + 2, wider]
    c8 = c4_ref[0:rows, g[3]] + c4_ref[4:rows + 4, g[3]]
    sums = [
        e_ref[7:7 + tm, g[0]] + e_ref[8:8 + tm, g[0]],
        c2_ref[6:6 + tm, g[1]] + c2_ref[8:8 + tm, g[1]],
        c4_ref[4:4 + tm, g[2]] + c4_ref[8:8 + tm, g[2]],
        c8[0:tm] + c8[8:8 + tm],
    ]
    return [(sums[i] * _inv_counts(w, tm, at_seq_start, at_seq_end)
             - e_ref[POOL_HALO:POOL_HALO + tm, g[i]]).astype(BF16)
            for i, w in enumerate(POOL_WINDOWS)]


def _stage1_kernel(*refs, use_rope, halo, emit_cache, self_attend, tm, sub, seq):
    it = iter(refs)
    x_ref = next(it)
    if halo:
        xp_ref, xn_ref = next(it), next(it)
    (mod_ref, norm1_ref, w_in_ref, qn_ref, wq_ref, kvn_ref, wk_ref, wv_ref, wp_ref,
     ps_ref) = (next(it) for _ in range(10))
    if use_rope:
        q_tabs = [next(it) for _ in range(2)]
        k_tabs = [next(it) for _ in range(3)]
    if self_attend:
        a_ref, m_ref = next(it), next(it)
    else:
        q_ref, k_ref, v_ref, m_ref = (next(it) for _ in range(4))
    if emit_cache:
        ckv_ref, kr_ref = next(it), next(it)
    e_ref, c2_ref, c4_ref = next(it), next(it), next(it)

    n_sub = tm // sub

    def sub_tile(j):
        r0 = j * sub
        rows = slice(r0, r0 + sub)
        pos0 = (pl.program_id(0) * tm + r0) % seq
        at_seq_start = pos0 == 0
        at_seq_end = pos0 + sub == seq
        e, c2, c4 = e_ref.at[j], c2_ref.at[j], c4_ref.at[j]

        if halo:
            before = xp_ref[...] if j == 0 else x_ref[r0 - POOL_HALO:r0, :]
            after = xn_ref[...] if j == n_sub - 1 else x_ref[r0 + sub:r0 + sub + POOL_HALO, :]
            x = jnp.concatenate([before, x_ref[rows, :], after], axis=0)
        else:
            x = x_ref[rows, :]
        gain = norm1_ref[...] * (1.0 + mod_ref[0, 1:2, :])
        h = (_rms(x, gain) + mod_ref[0, 0:1, :]).astype(BF16)
        yield
        proj = _dot(h, w_in_ref[...])
        yield
        u = proj[:, U_OFF:]
        zeros = jnp.zeros((POOL_HALO, POOL_WIDTH), F32)
        if halo:
            e[0:POOL_HALO, :] = jnp.where(at_seq_start, 0.0, u[0:POOL_HALO])
            e[POOL_HALO:POOL_HALO + sub, :] = u[POOL_HALO:POOL_HALO + sub]
            e[POOL_HALO + sub:2 * POOL_HALO + sub, :] = jnp.where(at_seq_end, 0.0, u[POOL_HALO + sub:])
            proj = proj[POOL_HALO:POOL_HALO + sub]
        else:
            e[0:POOL_HALO, :] = zeros
            e[POOL_HALO:POOL_HALO + sub, :] = u
            e[POOL_HALO + sub:2 * POOL_HALO + sub, :] = zeros
        e[2 * POOL_HALO + sub:, :] = zeros
        pooled = _pool_windows(e, c2, c4, sub, at_seq_start, at_seq_end)
        qa = _rms(proj[:, :Q_LORA_RANK], qn_ref[...]).astype(BF16)
        yield
        for g in range(len(POOL_WINDOWS)):
            cols = slice(g * POOL_GROUP, (g + 1) * POOL_GROUP)
            m_ref[rows, cols] = (_dot(pooled[g], wp_ref[g]) * ps_ref[:, cols]).astype(BF16)
        q = _dot(qa, wq_ref[...])
        yield
        if use_rope:
            pos = pl.multiple_of(pos0, sub)
            cq, sq = (r[pl.ds(pos, sub), :] for r in q_tabs)
        q_heads = []
        for hd in range(N_HEADS):
            slab = q[:, hd * HEAD_SLAB:(hd + 1) * HEAD_SLAB]
            if use_rope:
                swapped = q[:, QK_WIDTH + hd * HEAD_SLAB:QK_WIDTH + (hd + 1) * HEAD_SLAB]
                slab = slab * cq + swapped * sq
            else:
                slab = slab * Q_SCALE
            if self_attend:
                q_heads.append(slab.astype(BF16))
            else:
                q_ref[rows, hd * HEAD_SLAB:(hd + 1) * HEAD_SLAB] = slab.astype(BF16)
        ckv = _rms(proj[:, Q_LORA_RANK:KR_OFF], kvn_ref[...])
        kr = proj[:, KR_OFF:U_OFF]
        if emit_cache:
            ckv_ref[rows, :] = ckv
            kr_ref[rows, :] = kr[:, :QK_ROPE_DIM]
        if use_rope:
            kr = _rope_slab(kr, *(r[pl.ds(pos, sub), :] for r in k_tabs))
        ckv16 = ckv.astype(BF16)
        kin = jnp.concatenate([ckv16, kr.astype(BF16)], axis=1)
        yield
        keys = _dot(kin, wk_ref[...]).astype(BF16)
        vals = _dot(ckv16, wv_ref[...]).astype(BF16)
        if not self_attend:
            k_ref[rows, :] = keys
            v_ref[rows, :] = vals
            return
        yield

        def write(unit, out):
            a_ref[rows, (unit[0] - 1) * V_HEAD_DIM:(unit[0] + 1) * V_HEAD_DIM] = out

        _attention(
            [(hd,) for hd in range(N_HEADS)],
            q_of=lambda u: q_heads[u[0]],
            keys_of=[lambda u: keys[:, u[0] * HEAD_SLAB:(u[0] + 1) * HEAD_SLAB]],
            value_pairs_of=[lambda u: vals[:, (u[0] // 2) * HEAD_SLAB:(u[0] // 2 + 1) * HEAD_SLAB]],
            write_pair=write)

    _round_robin([sub_tile(j) for j in range(n_sub)])


def _stage1_call(x2d, mod3, mod_row_fn, weights, rope_tabs, seq, tm, sub, emit_cache, self_attend):
    n_tok = x2d.shape[0]
    assert not self_attend or sub == seq
    assert tm % sub == 0 and seq % sub == 0 and sub >= 2 * SUBLANES
    assert seq % tm == 0 or tm % seq == 0
    use_rope = rope_tabs is not None
    halo = seq > sub
    assert halo or sub == seq
    halo_blocks = n_tok // POOL_HALO
    per_tile = tm // POOL_HALO
    in_specs = [pl.BlockSpec((tm, D_MODEL), lambda i: (i, 0))]
    args = [x2d]
    if halo:
        in_specs += [
            pl.BlockSpec((POOL_HALO, D_MODEL), lambda i: (jnp.maximum(i * per_tile - 1, 0), 0)),
            pl.BlockSpec((POOL_HALO, D_MODEL),
                         lambda i: (jnp.minimum((i + 1) * per_tile, halo_blocks - 1), 0)),
        ]
        args += [x2d, x2d]
    in_specs += [pl.BlockSpec((1, 6, D_MODEL), lambda i: (mod_row_fn(i), 0, 0))]
    in_specs += [_const_spec(w.shape) for w in weights]
    args += [mod3] + list(weights)
    if use_rope:
        in_specs += [_resident_spec((seq, LANES))] * len(rope_tabs)
        args += list(rope_tabs)
    widths = [MLA_WIDTH] if self_attend else [QK_WIDTH, QK_WIDTH, MLA_WIDTH]
    widths.append(POOL_WIDTH)
    out_specs = [pl.BlockSpec((tm, w), lambda i: (i, 0)) for w in widths]
    out_shape = [jax.ShapeDtypeStruct((n_tok, w), BF16) for w in widths]
    if emit_cache:
        out_specs += [pl.BlockSpec((tm, KV_LORA_RANK), lambda i: (i, 0)),
                      pl.BlockSpec((tm, QK_ROPE_DIM), lambda i: (i, 0))]
        out_shape += [jax.ShapeDtypeStruct((n_tok, KV_LORA_RANK), F32),
                      jax.ShapeDtypeStruct((n_tok, QK_ROPE_DIM), F32)]
    stencil_rows = sub + 3 * POOL_HALO
    return pl.pallas_call(
        functools.partial(_stage1_kernel, use_rope=use_rope, halo=halo, emit_cache=emit_cache,
                          self_attend=self_attend, tm=tm, sub=sub, seq=seq),
        grid=(n_tok // tm,),
        in_specs=in_specs,
        out_specs=out_specs,
        out_shape=out_shape,
        scratch_shapes=[pltpu.VMEM((tm // sub, stencil_rows, POOL_WIDTH), F32)] * 3,
        compiler_params=pltpu.CompilerParams(dimension_semantics=("parallel",),
                                             vmem_limit_bytes=VMEM_LIMIT),
        name="stage1_rope" if use_rope else "stage1_ctx",
    )(*args)


def _qk(qh, kh):
    return lax.dot_general(qh, kh, (((1,), (1,)), ((), ())), preferred_element_type=F32)


def _attention(units, q_of, keys_of, value_pairs_of, write_pair):
    def scores(unit):
        qh = q_of(unit)
        return [_qk(qh, k(unit)) for k in keys_of]

    def values(get, unit):
        vp = get(unit)
        own = (lax.broadcasted_iota(jnp.int32, vp.shape, 1) < V_HEAD_DIM) == (unit[-1] % 2 == 0)
        return jnp.where(own, vp, jnp.ones_like(vp))

    nxt = scores(units[0])
    for idx, unit in enumerate(units):
        blocks = nxt
        if idx + 1 < len(units):
            nxt = scores(units[idx + 1])
        m = functools.reduce(jnp.maximum, [jnp.max(s, axis=-1, keepdims=True) for s in blocks])
        o = functools.reduce(jnp.add, [_dot(jnp.exp2(s - m).astype(BF16), values(v, unit))
                                       for s, v in zip(blocks, value_pairs_of)])
        if unit[-1] % 2 == 0:
            o_even = o
        else:
            low = lax.broadcasted_iota(jnp.int32, o.shape, 1) < V_HEAD_DIM
            num = jnp.where(low, o_even, o)
            den = pltpu.roll(jnp.where(low, o, o_even), V_HEAD_DIM, 1)
            write_pair(unit, (num * (1.0 / den)).astype(BF16))


def _latent_attn_kernel(*refs, n_casts):
    q_ref, k_ref, v_ref, cckv_ref, ckr_ref, wk_ref, wv_ref = refs[:7]
    o_ref = refs[7 + n_casts]
    kc_ref, vc_ref = refs[-2:]
    for src, dst in zip(refs[7:7 + n_casts], refs[8 + n_casts:-2]):
        dst[...] = src[...].astype(BF16)

    @pl.when(pl.program_id(1) == 0)
    def _():
        ckv16 = cckv_ref[0].astype(BF16)
        kr = ckr_ref[0].astype(BF16)
        pad = jnp.zeros((kr.shape[0], LANES - QK_ROPE_DIM), BF16)
        kc_ref[...] = _dot(jnp.concatenate([ckv16, kr, pad], axis=1), wk_ref[...]).astype(BF16)
        vc_ref[...] = _dot(ckv16, wv_ref[...]).astype(BF16)

    slab = lambda hd: slice(hd * HEAD_SLAB, (hd + 1) * HEAD_SLAB)
    pair = lambda hd: slice((hd // 2) * HEAD_SLAB, (hd // 2 + 1) * HEAD_SLAB)

    def write(unit, out):
        o_ref[0, :, (unit[0] - 1) * V_HEAD_DIM:(unit[0] + 1) * V_HEAD_DIM] = out

    _attention([(hd,) for hd in range(N_HEADS)],
               q_of=lambda u: q_ref[0, :, slab(u[0])],
               keys_of=[lambda u: k_ref[0, :, slab(u[0])], lambda u: kc_ref[:, slab(u[0])]],
               value_pairs_of=[lambda u: v_ref[0, :, pair(u[0])], lambda u: vc_ref[:, pair(u[0])]],
               write_pair=write)


def _latent_attn_call(q3, k3, v3, cache_ckv3, cache_kr3, wk, wv, tq, casts):
    b, s, _ = q3.shape
    nk, nc = k3.shape[1], cache_ckv3.shape[1]
    steps_i = s // tq
    n_steps = b * steps_i
    in_specs = [
        pl.BlockSpec((1, tq, QK_WIDTH), lambda bi, i: (bi, i, 0)),
        pl.BlockSpec((1, nk, QK_WIDTH), lambda bi, i: (bi, 0, 0)),
        pl.BlockSpec((1, nk, MLA_WIDTH), lambda bi, i: (bi, 0, 0)),
        pl.BlockSpec((1, nc, KV_LORA_RANK), lambda bi, i: (bi, 0, 0)),
        pl.BlockSpec((1, nc, QK_ROPE_DIM), lambda bi, i: (bi, 0, 0)),
        pl.BlockSpec(wk.shape, lambda bi, i: (0, 0)),
        pl.BlockSpec(wv.shape, lambda bi, i: (0, 0)),
    ]
    args = [q3, k3, v3, cache_ckv3, cache_kr3, wk, wv]
    out_specs = [pl.BlockSpec((1, tq, MLA_WIDTH), lambda bi, i: (bi, i, 0))]
    out_shape = [jax.ShapeDtypeStruct((b, s, MLA_WIDTH), BF16)]
    for w in casts:
        assert w.shape[0] % (n_steps * 2 * SUBLANES) == 0
        blk = (w.shape[0] // n_steps, w.shape[1])
        spec = pl.BlockSpec(blk, lambda bi, i: (bi * steps_i + i, 0))
        in_specs.append(spec)
        out_specs.append(spec)
        args.append(w)
        out_shape.append(jax.ShapeDtypeStruct(w.shape, BF16))
    return pl.pallas_call(
        functools.partial(_latent_attn_kernel, n_casts=len(casts)),
        grid=(b, steps_i),
        in_specs=in_specs,
        out_specs=out_specs,
        out_shape=out_shape,
        scratch_shapes=[pltpu.VMEM((nc, QK_WIDTH), BF16), pltpu.VMEM((nc, MLA_WIDTH), BF16)],
        compiler_params=pltpu.CompilerParams(
            dimension_semantics=("parallel", "arbitrary"),
            vmem_limit_bytes=VMEM_LIMIT),
        name="attn_lat",
    )(*args)


def _post_kernel(x_ref, a_ref, m_ref, mod_ref, wo_ref, norm2_ref, w1_ref, w2_ref, fn_ref, o_ref,
                 *, tm, sub, ff_chunk):
    def sub_tile(j):
        rows = slice(j * sub, (j + 1) * sub)
        cat = jnp.concatenate([a_ref[rows, :], m_ref[rows, :]], axis=1)
        y = x_ref[rows, :] + mod_ref[0, 2:3, :] * _dot(cat, wo_ref[...])
        gain = norm2_ref[...] * (1.0 + mod_ref[0, 4:5, :])
        h2 = (_rms(y, gain) + mod_ref[0, 3:4, :]).astype(BF16)
        yield
        acc = jnp.zeros((sub, D_MODEL), F32)
        for c in range(D_FF // ff_chunk):
            a = _dot(h2, w1_ref[:, c * ff_chunk:(c + 1) * ff_chunk])
            a = jnp.square(jnp.maximum(a, 0.0)).astype(BF16)
            acc = acc + _dot(a, w2_ref[c * ff_chunk:(c + 1) * ff_chunk, :])
            yield
        x2 = y + mod_ref[0, 5:6, :] * acc
        o_ref[rows, :] = _rms(x2, fn_ref[...])

    _round_robin([sub_tile(j) for j in range(tm // sub)])


def _post_call(x2d, attn2d, mixed2d, mod3, mod_row_fn, weights, tm, sub):
    n_tok = x2d.shape[0]
    assert tm % sub == 0
    w_o, norm2, w1, w2, final_norm = weights
    in_specs = [
        pl.BlockSpec((tm, D_MODEL), lambda i: (i, 0)),
        pl.BlockSpec((tm, MLA_WIDTH), lambda i: (i, 0)),
        pl.BlockSpec((tm, POOL_WIDTH), lambda i: (i, 0)),
        pl.BlockSpec((1, 6, D_MODEL), lambda i: (mod_row_fn(i), 0, 0)),
        _resident_spec(w_o.shape), _resident_spec(norm2.shape), _resident_spec(w1.shape),
        _resident_spec(w2.shape), _resident_spec(final_norm.shape),
    ]
    return pl.pallas_call(
        functools.partial(_post_kernel, tm=tm, sub=sub, ff_chunk=1024),
        grid=(n_tok // tm,),
        in_specs=in_specs,
        out_specs=pl.BlockSpec((tm, D_MODEL), lambda i: (i, 0)),
        out_shape=jax.ShapeDtypeStruct((n_tok, D_MODEL), F32),
        compiler_params=pltpu.CompilerParams(dimension_semantics=("parallel",),
                                             vmem_limit_bytes=VMEM_LIMIT),
        name="post",
    )(x2d, attn2d, mixed2d, mod3, w_o, norm2, w1, w2, final_norm)


def _pack_w_in(w_in):
    qa = w_in[:, :Q_LORA_RANK + KV_LORA_RANK]
    kr = w_in[:, KR_OFF:KR_OFF + QK_ROPE_DIM]
    u = w_in[:, KR_OFF + QK_ROPE_DIM:]
    pad = jnp.zeros((D_MODEL, LANES - QK_ROPE_DIM), w_in.dtype)
    return jnp.concatenate([qa, kr, pad, u], axis=1).astype(BF16)


def _pack_wq(w_qb, with_swapped):
    w = w_qb.reshape(Q_LORA_RANK, N_HEADS, QK_DIM)
    tail = ((0, 0), (0, 0), (0, HEAD_SLAB - QK_DIM))
    plain = jnp.pad(w, tail).reshape(Q_LORA_RANK, QK_WIDTH)
    if not with_swapped:
        return plain.astype(BF16)
    r = w[:, :, QK_NOPE_DIM:].reshape(Q_LORA_RANK, N_HEADS, 2, 2, N_FREQ)
    sw = jnp.stack([-r[:, :, :, 1, :], r[:, :, :, 0, :]], axis=3).reshape(Q_LORA_RANK, N_HEADS, QK_ROPE_DIM)
    sw = jnp.concatenate([jnp.zeros((Q_LORA_RANK, N_HEADS, QK_NOPE_DIM), w.dtype), sw], axis=2)
    swapped = jnp.pad(sw, tail).reshape(Q_LORA_RANK, QK_WIDTH)
    return jnp.concatenate([plain, swapped], axis=1).astype(BF16)


def _pack_wkv(w_kvb):
    w = w_kvb.reshape(KV_LORA_RANK, N_HEADS, QK_NOPE_DIM + V_HEAD_DIM)
    wk_nope = jnp.pad(w[:, :, :QK_NOPE_DIM], ((0, 0), (0, 0), (0, HEAD_SLAB - QK_NOPE_DIM)))
    sel = jnp.pad(jnp.eye(QK_ROPE_DIM, dtype=w_kvb.dtype),
                  ((0, 0), (QK_NOPE_DIM, HEAD_SLAB - QK_DIM)))
    sel = jnp.broadcast_to(sel[:, None, :], (QK_ROPE_DIM, N_HEADS, HEAD_SLAB))
    zero = jnp.zeros((LANES - QK_ROPE_DIM, N_HEADS, HEAD_SLAB), w_kvb.dtype)
    wk = jnp.concatenate([wk_nope, sel, zero], axis=0).reshape(KV_LORA_RANK + LANES, QK_WIDTH)
    wv = w[:, :, QK_NOPE_DIM:].reshape(KV_LORA_RANK, MLA_WIDTH)
    return wk.astype(BF16), wv.astype(BF16)


def _rope_tables(n_tokens):
    f = np.float32
    rows = n_tokens // GRID_W
    row = np.repeat(np.arange(rows, dtype=f), GRID_W)
    col = np.tile(np.arange(GRID_W, dtype=f), rows)
    inv_freq = (1.0 / (f(ROPE_THETA) ** (np.arange(N_FREQ, dtype=f) * f(2.0) / f(AXIS_ROPE)))).astype(f)
    cos_r, sin_r = np.cos(row[:, None] * inv_freq), np.sin(row[:, None] * inv_freq)
    cos_c, sin_c = np.cos(col[:, None] * inv_freq), np.sin(col[:, None] * inv_freq)
    z = np.zeros_like(cos_r)
    c32 = np.concatenate([cos_r, cos_r, cos_c, cos_c], axis=1)
    s32 = np.concatenate([sin_r, sin_r, sin_c, sin_c], axis=1)
    up32 = np.concatenate([z, sin_r, z, sin_c], axis=1)
    dn32 = np.concatenate([-sin_r, z, -sin_c, z], axis=1)

    def place(t32, off, fill):
        left = np.full((n_tokens, off), fill, f)
        right = np.zeros((n_tokens, LANES - off - QK_ROPE_DIM), f)
        return np.concatenate([left, t32.astype(f), right], axis=1)

    q_tabs = (place(c32 * f(Q_SCALE), QK_NOPE_DIM, Q_SCALE), place(s32 * f(Q_SCALE), QK_NOPE_DIM, 0.0))
    k_tabs = (place(c32, 0, 0.0), place(up32, 0, 0.0), place(dn32, 0, 0.0))
    return q_tabs + k_tabs


def kernel(x_prompt, x_sample, cache_ckv, cache_krope, c, c_ctx, w_ada, b_ada, norm1, w_in, q_norm,
           w_qb, kv_norm, w_kvb, w_pool, pool_scale, w_o, norm2, w1, w2, final_norm):
    batch, seq, _ = x_prompt.shape
    dec_batch, dec_seq, _ = x_sample.shape
    depth = w_in.shape[0]
    past = cache_ckv.shape[2]
    assert depth == 1, "single trunk layer"
    assert dec_batch + 1 <= MOD_ROWS

    c_all = jnp.concatenate(
        [c, c_ctx[None, :], jnp.zeros((MOD_ROWS - dec_batch - 1, D_MODEL), F32)], axis=0)
    mod3 = _mod_call(c_all, w_ada[0], b_ada[0][None, :]).reshape(MOD_ROWS, 6, D_MODEL)

    wk, wv = _pack_wkv(w_kvb[0])

    def s1_weights(with_swapped):
        return (norm1[0][None, :], _pack_w_in(w_in[0]), q_norm[0][None, :],
                _pack_wq(w_qb[0], with_swapped), kv_norm[0][None, :], wk, wv,
                w_pool[0].astype(BF16), pool_scale[0][None, :])

    xc = x_prompt.reshape(batch * seq, D_MODEL)
    ctx_row = lambda i: dec_batch
    attn_c, mc, ckv_new, kr_new = _stage1_call(xc, mod3, ctx_row, s1_weights(False), None, seq,
                                               S1_TILE, seq, True, True)

    xl = x_sample.reshape(dec_batch * dec_seq, D_MODEL)
    lat_row = lambda i: i // (dec_seq // S1_TILE)
    ql, kl, vl, ml = _stage1_call(xl, mod3, lat_row, s1_weights(True), _rope_tables(dec_seq), dec_seq,
                                  S1_TILE, S1_SUB, False, False)
    attn_l, w1_16, w2_16 = _latent_attn_call(
        ql.reshape(dec_batch, dec_seq, QK_WIDTH), kl.reshape(dec_batch, dec_seq, QK_WIDTH),
        vl.reshape(dec_batch, dec_seq, MLA_WIDTH),
        cache_ckv.reshape(dec_batch, past, KV_LORA_RANK), cache_krope.reshape(dec_batch, past, QK_ROPE_DIM),
        wk, wv, LAT_Q_TILE, casts=(w1[0], w2[0]))

    post_weights = (w_o[0].astype(BF16), norm2[0][None, :], w1_16, w2_16, final_norm[None, :])
    y_prompt = _post_call(xc, attn_c, mc, mod3, ctx_row,
                          post_weights, POST_TILE, POST_SUB)
    post_row = lambda i: i // (dec_seq // POST_TILE)
    y_sample = _post_call(xl, attn_l.reshape(dec_batch * dec_seq, MLA_WIDTH), ml, mod3, post_row,
                          post_weights, POST_TILE, POST_SUB)

    return (y_prompt.reshape(batch, seq, D_MODEL), y_sample.reshape(dec_batch, dec_seq, D_MODEL),
            ckv_new.reshape(batch, 1, seq, KV_LORA_RANK), kr_new.reshape(batch, 1, seq, QK_ROPE_DIM))
```

```python
import functools
import math

import jax
import jax.numpy as jnp
import numpy as np
from jax import lax
from jax.experimental import pallas as pl
from jax.experimental.pallas import tpu as pltpu

D_MODEL = 1024
N_HEADS = 8
QK_NOPE_DIM = 64
QK_ROPE_DIM = 32
V_HEAD_DIM = 64
QK_DIM = QK_NOPE_DIM + QK_ROPE_DIM
Q_LORA_RANK = 256
KV_LORA_RANK = 128
MLA_WIDTH = N_HEADS * V_HEAD_DIM
POOL_WIDTH = D_MODEL - MLA_WIDTH
POOL_WINDOWS = (2, 4, 8, 16)
POOL_GROUP = POOL_WIDTH // len(POOL_WINDOWS)
D_FF = 4 * D_MODEL
GRID_W = 64
ROPE_THETA = 10000.0
AXIS_ROPE = QK_ROPE_DIM // 2
N_FREQ = AXIS_ROPE // 2
EPS = 1e-6
ATTN_SCALE = 1.0 / math.sqrt(QK_DIM)
Q_SCALE = ATTN_SCALE * math.log2(math.e)

LANES = 128
SUBLANES = 8
HEAD_SLAB = LANES
QK_WIDTH = N_HEADS * HEAD_SLAB
MOD_ROWS = 16
POOL_HALO = max(POOL_WINDOWS) // 2
assert POOL_HALO == SUBLANES
KR_OFF = Q_LORA_RANK + KV_LORA_RANK
U_OFF = KR_OFF + LANES
PROJ_COLS = U_OFF + POOL_WIDTH
VMEM_LIMIT = 56 * 1024 * 1024
S1_TILE = 1024
S1_SUB = 256
LAT_Q_TILE = 1024
LAT_Q_SUB = 512
POST_TILE = 1024
POST_SUB = 256

F32 = jnp.float32
BF16 = jnp.bfloat16


def _dot(a, b):
    return jnp.dot(a, b, preferred_element_type=F32)


def _rms(x, g):
    return x * lax.rsqrt(jnp.mean(x * x, axis=-1, keepdims=True) + EPS) * g


def _round_robin(phased):
    active = list(phased)
    while active:
        for gen in list(active):
            if next(gen, "done") == "done":
                active.remove(gen)


def _const_spec(shape):
    nd = len(shape)
    return pl.BlockSpec(shape, lambda i: (0,) * nd)


def _resident_spec(shape):
    nd = len(shape)
    return pl.BlockSpec(shape, lambda i: (0,) * nd, pipeline_mode=pl.Buffered(1))


def _mod_kernel(c_ref, w_ref, b_ref, o_ref):
    c = c_ref[...]
    s = (c * jax.nn.sigmoid(c)).astype(BF16)
    o_ref[...] = _dot(s, w_ref[...].astype(BF16)) + b_ref[...]


def _mod_call(c_all, w_ada, b_ada):
    tn = 1024
    n = w_ada.shape[1]
    return pl.pallas_call(
        _mod_kernel,
        grid=(n // tn,),
        in_specs=[
            pl.BlockSpec((MOD_ROWS, D_MODEL), lambda j: (0, 0)),
            pl.BlockSpec((D_MODEL, tn), lambda j: (0, j)),
            pl.BlockSpec((1, tn), lambda j: (0, j)),
        ],
        out_specs=pl.BlockSpec((MOD_ROWS, tn), lambda j: (0, j)),
        out_shape=jax.ShapeDtypeStruct((MOD_ROWS, n), F32),
        compiler_params=pltpu.CompilerParams(dimension_semantics=("arbitrary",)),
        name="mod",
    )(c_all, w_ada, b_ada)


def _rope_slab(slab, c, s_up, s_dn):
    return (slab * c + pltpu.roll(slab, N_FREQ, 1) * s_up
            + pltpu.roll(slab, LANES - N_FREQ, 1) * s_dn)


def _inv_counts(w, tm, at_seq_start, at_seq_end):
    half = w // 2
    r = lax.broadcasted_iota(jnp.int32, (SUBLANES, LANES), 0)
    first = jnp.where(at_seq_start, (r + half) - jnp.maximum(r - half, 0), w)
    last = jnp.where(at_seq_end, jnp.minimum(half, SUBLANES - r) + half, w)
    mid = jnp.full((tm - 2 * SUBLANES, LANES), 1.0 / w, F32)
    return jnp.concatenate([1.0 / first.astype(F32), mid, 1.0 / last.astype(F32)], axis=0)


def _pool_windows(e_ref, c2_ref, c4_ref, tm, at_seq_start, at_seq_end):
    rows = tm + 2 * POOL_HALO
    g = [slice(i * POOL_GROUP, (i + 1) * POOL_GROUP) for i in range(len(POOL_WINDOWS))]
    wide = slice(POOL_GROUP, POOL_WIDTH)
    wider = slice(2 * POOL_GROUP, POOL_WIDTH)
    zeros = jnp.zeros((SUBLANES, POOL_WIDTH), F32)
    c2_ref[rows:, :] = zeros
    c4_ref[rows:, :] = zeros
    c2_ref[0:rows, wide] = e_ref[0:rows, wide] + e_ref[1:rows + 1, wide]
    c4_ref[0:rows, wider] = c2_ref[0:rows, wider] + c2_ref[2:rows + 2, wider]
    c8 = c4_ref[0:rows, g[3]] + c4_ref[4:rows + 4, g[3]]
    sums = [
        e_ref[7:7 + tm, g[0]] + e_ref[8:8 + tm, g[0]],
        c2_ref[6:6 + tm, g[1]] + c2_ref[8:8 + tm, g[1]],
        c4_ref[4:4 + tm, g[2]] + c4_ref[8:8 + tm, g[2]],
        c8[0:tm] + c8[8:8 + tm],
    ]
    return [(sums[i] * _inv_counts(w, tm, at_seq_start, at_seq_end)
             - e_ref[POOL_HALO:POOL_HALO + tm, g[i]]).astype(BF16)
            for i, w in enumerate(POOL_WINDOWS)]


def _stage1_kernel(*refs, use_rope, halo, emit_cache, self_attend, tm, sub, seq):
    it = iter(refs)
    x_ref = next(it)
    if halo:
        xp_ref, xn_ref = next(it), next(it)
    (mod_ref, norm1_ref, w_in_ref, qn_ref, wq_ref, kvn_ref, wk_ref, wv_ref, wp_ref,
     ps_ref) = (next(it) for _ in range(10))
    if use_rope:
        q_tabs = [next(it) for _ in range(2)]
        k_tabs = [next(it) for _ in range(3)]
    if self_attend:
        a_ref, m_ref = next(it), next(it)
    else:
        q_ref, k_ref, v_ref, m_ref = (next(it) for _ in range(4))
    if emit_cache:
        ckv_ref, kr_ref = next(it), next(it)
    e_ref, c2_ref, c4_ref = next(it), next(it), next(it)

    n_sub = tm // sub

    def sub_tile(j):
        r0 = j * sub
        rows = slice(r0, r0 + sub)
        pos0 = (pl.program_id(0) * tm + r0) % seq
        at_seq_start = pos0 == 0
        at_seq_end = pos0 + sub == seq
        e, c2, c4 = e_ref.at[j], c2_ref.at[j], c4_ref.at[j]

        if halo:
            before = xp_ref[...] if j == 0 else x_ref[r0 - POOL_HALO:r0, :]
            after = xn_ref[...] if j == n_sub - 1 else x_ref[r0 + sub:r0 + sub + POOL_HALO, :]
            x = jnp.concatenate([before, x_ref[rows, :], after], axis=0)
        else:
            x = x_ref[rows, :]
        gain = norm1_ref[...] * (1.0 + mod_ref[0, 1:2, :])
        h = (_rms(x, gain) + mod_ref[0, 0:1, :]).astype(BF16)
        yield
        proj = _dot(h, w_in_ref[...])
        yield
        u = proj[:, U_OFF:]
        zeros = jnp.zeros((POOL_HALO, POOL_WIDTH), F32)
        if halo:
            e[0:POOL_HALO, :] = jnp.where(at_seq_start, 0.0, u[0:POOL_HALO])
            e[POOL_HALO:POOL_HALO + sub, :] = u[POOL_HALO:POOL_HALO + sub]
            e[POOL_HALO + sub:2 * POOL_HALO + sub, :] = jnp.where(at_seq_end, 0.0, u[POOL_HALO + sub:])
            proj = proj[POOL_HALO:POOL_HALO + sub]
        else:
            e[0:POOL_HALO, :] = zeros
            e[POOL_HALO:POOL_HALO + sub, :] = u
            e[POOL_HALO + sub:2 * POOL_HALO + sub, :] = zeros
        e[2 * POOL_HALO + sub:, :] = zeros
        pooled = _pool_windows(e, c2, c4, sub, at_seq_start, at_seq_end)
        qa = _rms(proj[:, :Q_LORA_RANK], qn_ref[...]).astype(BF16)
        yield
        for g in range(len(POOL_WINDOWS)):
            cols = slice(g * POOL_GROUP, (g + 1) * POOL_GROUP)
            m_ref[rows, cols] = (_dot(pooled[g], wp_ref[g]) * ps_ref[:, cols]).astype(BF16)
        q = _dot(qa, wq_ref[...])
        yield
        if use_rope:
            pos = pl.multiple_of(pos0, sub)
            cq, sq = (r[pl.ds(pos, sub), :] for r in q_tabs)
        q_heads = []
        for hd in range(N_HEADS):
            slab = q[:, hd * HEAD_SLAB:(hd + 1) * HEAD_SLAB]
            if use_rope:
                swapped = q[:, QK_WIDTH + hd * HEAD_SLAB:QK_WIDTH + (hd + 1) * HEAD_SLAB]
                slab = slab * cq + swapped * sq
            else:
                slab = slab * Q_SCALE
            if self_attend:
                q_heads.append(slab.astype(BF16))
            else:
                q_ref[rows, hd * HEAD_SLAB:(hd + 1) * HEAD_SLAB] = slab.astype(BF16)
        ckv = _rms(proj[:, Q_LORA_RANK:KR_OFF], kvn_ref[...])
        kr = proj[:, KR_OFF:U_OFF]
        if emit_cache:
            ckv_ref[rows, :] = ckv
            kr_ref[rows, :] = kr[:, :QK_ROPE_DIM]
        if use_rope:
            kr = _rope_slab(kr, *(r[pl.ds(pos, sub), :] for r in k_tabs))
        ckv16 = ckv.astype(BF16)
        kin = jnp.concatenate([ckv16, kr.astype(BF16)], axis=1)
        yield
        keys = _dot(kin, wk_ref[...]).astype(BF16)
        vals = _dot(ckv16, wv_ref[...]).astype(BF16)
        if not self_attend:
            k_ref[rows, :] = keys
            v_ref[rows, :] = vals
            return
        yield

        def write(unit, out):
            a_ref[rows, (unit[0] - 1) * V_HEAD_DIM:(unit[0] + 1) * V_HEAD_DIM] = out

        _attention(
            [(hd,) for hd in range(N_HEADS)],
            q_of=lambda u: q_heads[u[0]],
            keys_of=[lambda u: keys[:, u[0] * HEAD_SLAB:(u[0] + 1) * HEAD_SLAB]],
            value_pairs_of=[lambda u: vals[:, (u[0] // 2) * HEAD_SLAB:(u[0] // 2 + 1) * HEAD_SLAB]],
            write_pair=write)

    _round_robin([sub_tile(j) for j in range(n_sub)])


def _stage1_call(x2d, mod3, mod_row_fn, weights, rope_tabs, seq, tm, sub, emit_cache, self_attend):
    n_tok = x2d.shape[0]
    assert not self_attend or sub == seq
    assert tm % sub == 0 and seq % sub == 0 and sub >= 2 * SUBLANES
    assert seq % tm == 0 or tm % seq == 0
    use_rope = rope_tabs is not None
    halo = seq > sub
    assert halo or sub == seq
    halo_blocks = n_tok // POOL_HALO
    per_tile = tm // POOL_HALO
    in_specs = [pl.BlockSpec((tm, D_MODEL), lambda i: (i, 0))]
    args = [x2d]
    if halo:
        in_specs += [
            pl.BlockSpec((POOL_HALO, D_MODEL), lambda i: (jnp.maximum(i * per_tile - 1, 0), 0)),
            pl.BlockSpec((POOL_HALO, D_MODEL),
                         lambda i: (jnp.minimum((i + 1) * per_tile, halo_blocks - 1), 0)),
        ]
        args += [x2d, x2d]
    in_specs += [pl.BlockSpec((1, 6, D_MODEL), lambda i: (mod_row_fn(i), 0, 0))]
    in_specs += [_const_spec(w.shape) for w in weights]
    args += [mod3] + list(weights)
    if use_rope:
        in_specs += [_resident_spec((seq, LANES))] * len(rope_tabs)
        args += list(rope_tabs)
    widths = [MLA_WIDTH] if self_attend else [QK_WIDTH, QK_WIDTH, MLA_WIDTH]
    widths.append(POOL_WIDTH)
    out_specs = [pl.BlockSpec((tm, w), lambda i: (i, 0)) for w in widths]
    out_shape = [jax.ShapeDtypeStruct((n_tok, w), BF16) for w in widths]
    if emit_cache:
        out_specs += [pl.BlockSpec((tm, KV_LORA_RANK), lambda i: (i, 0)),
                      pl.BlockSpec((tm, QK_ROPE_DIM), lambda i: (i, 0))]
        out_shape += [jax.ShapeDtypeStruct((n_tok, KV_LORA_RANK), F32),
                      jax.ShapeDtypeStruct((n_tok, QK_ROPE_DIM), F32)]
    stencil_rows = sub + 3 * POOL_HALO
    return pl.pallas_call(
        functools.partial(_stage1_kernel, use_rope=use_rope, halo=halo, emit_cache=emit_cache,
                          self_attend=self_attend, tm=tm, sub=sub, seq=seq),
        grid=(n_tok // tm,),
        in_specs=in_specs,
        out_specs=out_specs,
        out_shape=out_shape,
        scratch_shapes=[pltpu.VMEM((tm // sub, stencil_rows, POOL_WIDTH), F32)] * 3,
        compiler_params=pltpu.CompilerParams(dimension_semantics=("parallel",),
                                             vmem_limit_bytes=VMEM_LIMIT),
        name="stage1_rope" if use_rope else "stage1_ctx",
    )(*args)


def _qk(qh, kh):
    return lax.dot_general(qh, kh, (((1,), (1,)), ((), ())), preferred_element_type=F32)


def _attention(units, q_of, keys_of, value_pairs_of, write_pair):
    def scores(unit):
        qh = q_of(unit)
        return [_qk(qh, k(unit)) for k in keys_of]

    def values(get, unit):
        vp = get(unit)
        own = (lax.broadcasted_iota(jnp.int32, vp.shape, 1) < V_HEAD_DIM) == (unit[-1] % 2 == 0)
        return jnp.where(own, vp, jnp.ones_like(vp))

    nxt = scores(units[0])
    for idx, unit in enumerate(units):
        blocks = nxt
        if idx + 1 < len(units):
            nxt = scores(units[idx + 1])
        m = functools.reduce(jnp.maximum, [jnp.max(s, axis=-1, keepdims=True) for s in blocks])
        o = functools.reduce(jnp.add, [_dot(jnp.exp2(s - m).astype(BF16), values(v, unit))
                                       for s, v in zip(blocks, value_pairs_of)])
        if unit[-1] % 2 == 0:
            o_even = o
        else:
            low = lax.broadcasted_iota(jnp.int32, o.shape, 1) < V_HEAD_DIM
            num = jnp.where(low, o_even, o)
            den = pltpu.roll(jnp.where(low, o, o_even), V_HEAD_DIM, 1)
            write_pair(unit, (num * (1.0 / den)).astype(BF16))


def _latent_attn_kernel(*refs, n_casts, sub_rows):
    q_ref, k_ref, v_ref, cckv_ref, ckr_ref, wk_ref, wv_ref = refs[:7]
    o_ref = refs[7 + n_casts]
    kc_ref, vc_ref = refs[-2:]
    for src, dst in zip(refs[7:7 + n_casts], refs[8 + n_casts:-2]):
        dst[...] = src[...].astype(BF16)

    @pl.when(pl.program_id(1) == 0)
    def _():
        ckv16 = cckv_ref[0].astype(BF16)
        kr = ckr_ref[0].astype(BF16)
        pad = jnp.zeros((kr.shape[0], LANES - QK_ROPE_DIM), BF16)
        kc_ref[...] = _dot(jnp.concatenate([ckv16, kr, pad], axis=1), wk_ref[...]).astype(BF16)
        vc_ref[...] = _dot(ckv16, wv_ref[...]).astype(BF16)

    slab = lambda hd: slice(hd * HEAD_SLAB, (hd + 1) * HEAD_SLAB)
    pair = lambda hd: slice((hd // 2) * HEAD_SLAB, (hd // 2 + 1) * HEAD_SLAB)

    rows = lambda qi: slice(qi * sub_rows, (qi + 1) * sub_rows)

    def write(unit, out):
        qi, hd = unit
        o_ref[0, rows(qi), (hd - 1) * V_HEAD_DIM:(hd + 1) * V_HEAD_DIM] = out

    _attention([(qi, hd) for qi in range(q_ref.shape[1] // sub_rows) for hd in range(N_HEADS)],
               q_of=lambda u: q_ref[0, rows(u[0]), slab(u[1])],
               keys_of=[lambda u: k_ref[0, :, slab(u[1])], lambda u: kc_ref[:, slab(u[1])]],
               value_pairs_of=[lambda u: v_ref[0, :, pair(u[1])], lambda u: vc_ref[:, pair(u[1])]],
               write_pair=write)


def _latent_attn_call(q3, k3, v3, cache_ckv3, cache_kr3, wk, wv, tq, sub_rows, casts):
    b, s, _ = q3.shape
    nk, nc = k3.shape[1], cache_ckv3.shape[1]
    steps_i = s // tq
    n_steps = b * steps_i
    in_specs = [
        pl.BlockSpec((1, tq, QK_WIDTH), lambda bi, i: (bi, i, 0)),
        pl.BlockSpec((1, nk, QK_WIDTH), lambda bi, i: (bi, 0, 0)),
        pl.BlockSpec((1, nk, MLA_WIDTH), lambda bi, i: (bi, 0, 0)),
        pl.BlockSpec((1, nc, KV_LORA_RANK), lambda bi, i: (bi, 0, 0)),
        pl.BlockSpec((1, nc, QK_ROPE_DIM), lambda bi, i: (bi, 0, 0)),
        pl.BlockSpec(wk.shape, lambda bi, i: (0, 0)),
        pl.BlockSpec(wv.shape, lambda bi, i: (0, 0)),
    ]
    args = [q3, k3, v3, cache_ckv3, cache_kr3, wk, wv]
    out_specs = [pl.BlockSpec((1, tq, MLA_WIDTH), lambda bi, i: (bi, i, 0))]
    out_shape = [jax.ShapeDtypeStruct((b, s, MLA_WIDTH), BF16)]
    for w in casts:
        assert w.shape[0] % (n_steps * 2 * SUBLANES) == 0
        blk = (w.shape[0] // n_steps, w.shape[1])
        spec = pl.BlockSpec(blk, lambda bi, i: (bi * steps_i + i, 0))
        in_specs.append(spec)
        out_specs.append(spec)
        args.append(w)
        out_shape.append(jax.ShapeDtypeStruct(w.shape, BF16))
    return pl.pallas_call(
        functools.partial(_latent_attn_kernel, n_casts=len(casts), sub_rows=sub_rows),
        grid=(b, steps_i),
        in_specs=in_specs,
        out_specs=out_specs,
        out_shape=out_shape,
        scratch_shapes=[pltpu.VMEM((nc, QK_WIDTH), BF16), pltpu.VMEM((nc, MLA_WIDTH), BF16)],
        compiler_params=pltpu.CompilerParams(
            dimension_semantics=("parallel", "arbitrary"),
            vmem_limit_bytes=VMEM_LIMIT),
        name="attn_lat",
    )(*args)


def _post_kernel(x_ref, a_ref, m_ref, mod_ref, wo_ref, norm2_ref, w1_ref, w2_ref, fn_ref, o_ref,
                 *, tm, sub, ff_chunk):
    def sub_tile(j):
        rows = slice(j * sub, (j + 1) * sub)
        cat = jnp.concatenate([a_ref[rows, :], m_ref[rows, :]], axis=1)
        y = x_ref[rows, :] + mod_ref[0, 2:3, :] * _dot(cat, wo_ref[...])
        gain = norm2_ref[...] * (1.0 + mod_ref[0, 4:5, :])
        h2 = (_rms(y, gain) + mod_ref[0, 3:4, :]).astype(BF16)
        yield
        acc = jnp.zeros((sub, D_MODEL), F32)
        for c in range(D_FF // ff_chunk):
            a = _dot(h2, w1_ref[:, c * ff_chunk:(c + 1) * ff_chunk])
            a = jnp.square(jnp.maximum(a, 0.0)).astype(BF16)
            acc = acc + _dot(a, w2_ref[c * ff_chunk:(c + 1) * ff_chunk, :])
            yield
        x2 = y + mod_ref[0, 5:6, :] * acc
        o_ref[rows, :] = _rms(x2, fn_ref[...])

    _round_robin([sub_tile(j) for j in range(tm // sub)])


def _post_call(x2d, attn2d, mixed2d, mod3, mod_row_fn, weights, tm, sub):
    n_tok = x2d.shape[0]
    assert tm % sub == 0
    w_o, norm2, w1, w2, final_norm = weights
    in_specs = [
        pl.BlockSpec((tm, D_MODEL), lambda i: (i, 0)),
        pl.BlockSpec((tm, MLA_WIDTH), lambda i: (i, 0)),
        pl.BlockSpec((tm, POOL_WIDTH), lambda i: (i, 0)),
        pl.BlockSpec((1, 6, D_MODEL), lambda i: (mod_row_fn(i), 0, 0)),
        _resident_spec(w_o.shape), _resident_spec(norm2.shape), _resident_spec(w1.shape),
        _resident_spec(w2.shape), _resident_spec(final_norm.shape),
    ]
    return pl.pallas_call(
        functools.partial(_post_kernel, tm=tm, sub=sub, ff_chunk=1024),
        grid=(n_tok // tm,),
        in_specs=in_specs,
        out_specs=pl.BlockSpec((tm, D_MODEL), lambda i: (i, 0)),
        out_shape=jax.ShapeDtypeStruct((n_tok, D_MODEL), F32),
        compiler_params=pltpu.CompilerParams(dimension_semantics=("parallel",),
                                             vmem_limit_bytes=VMEM_LIMIT),
        name="post",
    )(x2d, attn2d, mixed2d, mod3, w_o, norm2, w1, w2, final_norm)


def _pack_w_in(w_in):
    qa = w_in[:, :Q_LORA_RANK + KV_LORA_RANK]
    kr = w_in[:, KR_OFF:KR_OFF + QK_ROPE_DIM]
    u = w_in[:, KR_OFF + QK_ROPE_DIM:]
    pad = jnp.zeros((D_MODEL, LANES - QK_ROPE_DIM), w_in.dtype)
    return jnp.concatenate([qa, kr, pad, u], axis=1).astype(BF16)


def _pack_wq(w_qb, with_swapped):
    w = w_qb.reshape(Q_LORA_RANK, N_HEADS, QK_DIM)
    tail = ((0, 0), (0, 0), (0, HEAD_SLAB - QK_DIM))
    plain = jnp.pad(w, tail).reshape(Q_LORA_RANK, QK_WIDTH)
    if not with_swapped:
        return plain.astype(BF16)
    r = w[:, :, QK_NOPE_DIM:].reshape(Q_LORA_RANK, N_HEADS, 2, 2, N_FREQ)
    sw = jnp.stack([-r[:, :, :, 1, :], r[:, :, :, 0, :]], axis=3).reshape(Q_LORA_RANK, N_HEADS, QK_ROPE_DIM)
    sw = jnp.concatenate([jnp.zeros((Q_LORA_RANK, N_HEADS, QK_NOPE_DIM), w.dtype), sw], axis=2)
    swapped = jnp.pad(sw, tail).reshape(Q_LORA_RANK, QK_WIDTH)
    return jnp.concatenate([plain, swapped], axis=1).astype(BF16)


def _pack_wkv(w_kvb):
    w = w_kvb.reshape(KV_LORA_RANK, N_HEADS, QK_NOPE_DIM + V_HEAD_DIM)
    wk_nope = jnp.pad(w[:, :, :QK_NOPE_DIM], ((0, 0), (0, 0), (0, HEAD_SLAB - QK_NOPE_DIM)))
    sel = jnp.pad(jnp.eye(QK_ROPE_DIM, dtype=w_kvb.dtype),
                  ((0, 0), (QK_NOPE_DIM, HEAD_SLAB - QK_DIM)))
    sel = jnp.broadcast_to(sel[:, None, :], (QK_ROPE_DIM, N_HEADS, HEAD_SLAB))
    zero = jnp.zeros((LANES - QK_ROPE_DIM, N_HEADS, HEAD_SLAB), w_kvb.dtype)
    wk = jnp.concatenate([wk_nope, sel, zero], axis=0).reshape(KV_LORA_RANK + LANES, QK_WIDTH)
    wv = w[:, :, QK_NOPE_DIM:].reshape(KV_LORA_RANK, MLA_WIDTH)
    return wk.astype(BF16), wv.astype(BF16)


def _rope_tables(n_tokens):
    f = np.float32
    rows = n_tokens // GRID_W
    row = np.repeat(np.arange(rows, dtype=f), GRID_W)
    col = np.tile(np.arange(GRID_W, dtype=f), rows)
    inv_freq = (1.0 / (f(ROPE_THETA) ** (np.arange(N_FREQ, dtype=f) * f(2.0) / f(AXIS_ROPE)))).astype(f)
    cos_r, sin_r = np.cos(row[:, None] * inv_freq), np.sin(row[:, None] * inv_freq)
    cos_c, sin_c = np.cos(col[:, None] * inv_freq), np.sin(col[:, None] * inv_freq)
    z = np.zeros_like(cos_r)
    c32 = np.concatenate([cos_r, cos_r, cos_c, cos_c], axis=1)
    s32 = np.concatenate([sin_r, sin_r, sin_c, sin_c], axis=1)
    up32 = np.concatenate([z, sin_r, z, sin_c], axis=1)
    dn32 = np.concatenate([-sin_r, z, -sin_c, z], axis=1)

    def place(t32, off, fill):
        left = np.full((n_tokens, off), fill, f)
        right = np.zeros((n_tokens, LANES - off - QK_ROPE_DIM), f)
        return np.concatenate([left, t32.astype(f), right], axis=1)

    q_tabs = (place(c32 * f(Q_SCALE), QK_NOPE_DIM, Q_SCALE), place(s32 * f(Q_SCALE), QK_NOPE_DIM, 0.0))
    k_tabs = (place(c32, 0, 0.0), place(up32, 0, 0.0), place(dn32, 0, 0.0))
    return q_tabs + k_tabs


def kernel(x_prompt, x_sample, cache_ckv, cache_krope, c, c_ctx, w_ada, b_ada, norm1, w_in, q_norm,
           w_qb, kv_norm, w_kvb, w_pool, pool_scale, w_o, norm2, w1, w2, final_norm):
    batch, seq, _ = x_prompt.shape
    dec_batch, dec_seq, _ = x_sample.shape
    depth = w_in.shape[0]
    past = cache_ckv.shape[2]
    assert depth == 1, "single trunk layer"
    assert dec_batch + 1 <= MOD_ROWS

    c_all = jnp.concatenate(
        [c, c_ctx[None, :], jnp.zeros((MOD_ROWS - dec_batch - 1, D_MODEL), F32)], axis=0)
    mod3 = _mod_call(c_all, w_ada[0], b_ada[0][None, :]).reshape(MOD_ROWS, 6, D_MODEL)

    wk, wv = _pack_wkv(w_kvb[0])

    def s1_weights(with_swapped):
        return (norm1[0][None, :], _pack_w_in(w_in[0]), q_norm[0][None, :],
                _pack_wq(w_qb[0], with_swapped), kv_norm[0][None, :], wk, wv,
                w_pool[0].astype(BF16), pool_scale[0][None, :])

    xc = x_prompt.reshape(batch * seq, D_MODEL)
    ctx_row = lambda i: dec_batch
    attn_c, mc, ckv_new, kr_new = _stage1_call(xc, mod3, ctx_row, s1_weights(False), None, seq,
                                               S1_TILE, seq, True, True)

    xl = x_sample.reshape(dec_batch * dec_seq, D_MODEL)
    lat_row = lambda i: i // (dec_seq // S1_TILE)
    ql, kl, vl, ml = _stage1_call(xl, mod3, lat_row, s1_weights(True), _rope_tables(dec_seq), dec_seq,
                                  S1_TILE, S1_SUB, False, False)
    attn_l, w1_16, w2_16 = _latent_attn_call(
        ql.reshape(dec_batch, dec_seq, QK_WIDTH), kl.reshape(dec_batch, dec_seq, QK_WIDTH),
        vl.reshape(dec_batch, dec_seq, MLA_WIDTH),
        cache_ckv.reshape(dec_batch, past, KV_LORA_RANK), cache_krope.reshape(dec_batch, past, QK_ROPE_DIM),
        wk, wv, LAT_Q_TILE, LAT_Q_SUB, casts=(w1[0], w2[0]))

    post_weights = (w_o[0].astype(BF16), norm2[0][None, :], w1_16, w2_16, final_norm[None, :])
    y_prompt = _post_call(xc, attn_c, mc, mod3, ctx_row,
                          post_weights, POST_TILE, POST_SUB)
    post_row = lambda i: i // (dec_seq // POST_TILE)
    y_sample = _post_call(xl, attn_l.reshape(dec_batch * dec_seq, MLA_WIDTH), ml, mod3, post_row,
                          post_weights, POST_TILE, POST_SUB)

    return (y_prompt.reshape(batch, seq, D_MODEL), y_sample.reshape(dec_batch, dec_seq, D_MODEL),
            ckv_new.reshape(batch, 1, seq, KV_LORA_RANK), kr_new.reshape(batch, 1, seq, QK_ROPE_DIM))
```

```python
import functools
import math

import jax
import jax.numpy as jnp
import numpy as np
from jax import lax
from jax.experimental import pallas as pl
from jax.experimental.pallas import tpu as pltpu

D_MODEL = 1024
N_HEADS = 8
QK_NOPE_DIM = 64
QK_ROPE_DIM = 32
V_HEAD_DIM = 64
QK_DIM = QK_NOPE_DIM + QK_ROPE_DIM
Q_LORA_RANK = 256
KV_LORA_RANK = 128
MLA_WIDTH = N_HEADS * V_HEAD_DIM
POOL_WIDTH = D_MODEL - MLA_WIDTH
POOL_WINDOWS = (2, 4, 8, 16)
POOL_GROUP = POOL_WIDTH // len(POOL_WINDOWS)
D_FF = 4 * D_MODEL
GRID_W = 64
ROPE_THETA = 10000.0
AXIS_ROPE = QK_ROPE_DIM // 2
N_FREQ = AXIS_ROPE // 2
EPS = 1e-6
ATTN_SCALE = 1.0 / math.sqrt(QK_DIM)
Q_SCALE = ATTN_SCALE * math.log2(math.e)

LANES = 128
SUBLANES = 8
HEAD_SLAB = LANES
QK_WIDTH = N_HEADS * HEAD_SLAB
MOD_ROWS = 16
POOL_HALO = max(POOL_WINDOWS) // 2
assert POOL_HALO == SUBLANES
KR_OFF = Q_LORA_RANK + KV_LORA_RANK
U_OFF = KR_OFF + LANES
PROJ_COLS = U_OFF + POOL_WIDTH
VMEM_LIMIT = 56 * 1024 * 1024
S1_TILE = 1024
S1_SUB = 256
LAT_Q_TILE = 512
POST_TILE = 1024
POST_SUB = 256

F32 = jnp.float32
BF16 = jnp.bfloat16


def _dot(a, b):
    return jnp.dot(a, b, preferred_element_type=F32)


def _rms(x, g):
    return x * lax.rsqrt(jnp.mean(x * x, axis=-1, keepdims=True) + EPS) * g


def _round_robin(phased):
    active = list(phased)
    while active:
        for gen in list(active):
            if next(gen, "done") == "done":
                active.remove(gen)


def _const_spec(shape):
    nd = len(shape)
    return pl.BlockSpec(shape, lambda i: (0,) * nd)


def _resident_spec(shape):
    nd = len(shape)
    return pl.BlockSpec(shape, lambda i: (0,) * nd, pipeline_mode=pl.Buffered(1))


def _mod_kernel(c_ref, w_ref, b_ref, o_ref):
    c = c_ref[...]
    s = (c * jax.nn.sigmoid(c)).astype(BF16)
    o_ref[...] = _dot(s, w_ref[...].astype(BF16)) + b_ref[...]


def _mod_call(c_all, w_ada, b_ada):
    tn = 1024
    n = w_ada.shape[1]
    return pl.pallas_call(
        _mod_kernel,
        grid=(n // tn,),
        in_specs=[
            pl.BlockSpec((MOD_ROWS, D_MODEL), lambda j: (0, 0)),
            pl.BlockSpec((D_MODEL, tn), lambda j: (0, j)),
            pl.BlockSpec((1, tn), lambda j: (0, j)),
        ],
        out_specs=pl.BlockSpec((MOD_ROWS, tn), lambda j: (0, j)),
        out_shape=jax.ShapeDtypeStruct((MOD_ROWS, n), F32),
        compiler_params=pltpu.CompilerParams(dimension_semantics=("arbitrary",)),
        name="mod",
    )(c_all, w_ada, b_ada)


def _rope_slab(slab, c, s_up, s_dn):
    return (slab * c + pltpu.roll(slab, N_FREQ, 1) * s_up
            + pltpu.roll(slab, LANES - N_FREQ, 1) * s_dn)


def _inv_counts(w, tm, at_seq_start, at_seq_end):
    half = w // 2
    r = lax.broadcasted_iota(jnp.int32, (SUBLANES, LANES), 0)
    first = jnp.where(at_seq_start, (r + half) - jnp.maximum(r - half, 0), w)
    last = jnp.where(at_seq_end, jnp.minimum(half, SUBLANES - r) + half, w)
    mid = jnp.full((tm - 2 * SUBLANES, LANES), 1.0 / w, F32)
    return jnp.concatenate([1.0 / first.astype(F32), mid, 1.0 / last.astype(F32)], axis=0)


def _pool_windows(e_ref, c2_ref, c4_ref, tm, at_seq_start, at_seq_end):
    rows = tm + 2 * POOL_HALO
    g = [slice(i * POOL_GROUP, (i + 1) * POOL_GROUP) for i in range(len(POOL_WINDOWS))]
    wide = slice(POOL_GROUP, POOL_WIDTH)
    wider = slice(2 * POOL_GROUP, POOL_WIDTH)
    zeros = jnp.zeros((SUBLANES, POOL_WIDTH), F32)
    c2_ref[rows:, :] = zeros
    c4_ref[rows:, :] = zeros
    c2_ref[0:rows, wide] = e_ref[0:rows, wide] + e_ref[1:rows + 1, wide]
    c4_ref[0:rows, wider] = c2_ref[0:rows, wider] + c2_ref[2:rows + 2, wider]
    c8 = c4_ref[0:rows, g[3]] + c4_ref[4:rows + 4, g[3]]
    sums = [
        e_ref[7:7 + tm, g[0]] + e_ref[8:8 + tm, g[0]],
        c2_ref[6:6 + tm, g[1]] + c2_ref[8:8 + tm, g[1]],
        c4_ref[4:4 + tm, g[2]] + c4_ref[8:8 + tm, g[2]],
        c8[0:tm] + c8[8:8 + tm],
    ]
    return [(sums[i] * _inv_counts(w, tm, at_seq_start, at_seq_end)
             - e_ref[POOL_HALO:POOL_HALO + tm, g[i]]).astype(BF16)
            for i, w in enumerate(POOL_WINDOWS)]


def _stage1_kernel(*refs, use_rope, halo, emit_cache, self_attend, tm, sub, seq):
    it = iter(refs)
    x_ref = next(it)
    if halo:
        xp_ref, xn_ref = next(it), next(it)
    (mod_ref, norm1_ref, w_in_ref, qn_ref, wq_ref, kvn_ref, wk_ref, wv_ref, wp_ref,
     ps_ref) = (next(it) for _ in range(10))
    if use_rope:
        q_tabs = [next(it) for _ in range(2)]
        k_tabs = [next(it) for _ in range(3)]
    if self_attend:
        a_ref, m_ref = next(it), next(it)
    else:
        q_ref, k_ref, v_ref, m_ref = (next(it) for _ in range(4))
    if emit_cache:
        ckv_ref, kr_ref = next(it), next(it)
    e_ref, c2_ref, c4_ref = next(it), next(it), next(it)

    n_sub = tm // sub

    def sub_tile(j):
        r0 = j * sub
        rows = slice(r0, r0 + sub)
        pos0 = (pl.program_id(0) * tm + r0) % seq
        at_seq_start = pos0 == 0
        at_seq_end = pos0 + sub == seq
        e, c2, c4 = e_ref.at[j], c2_ref.at[j], c4_ref.at[j]

        if halo:
            before = xp_ref[...] if j == 0 else x_ref[r0 - POOL_HALO:r0, :]
            after = xn_ref[...] if j == n_sub - 1 else x_ref[r0 + sub:r0 + sub + POOL_HALO, :]
            x = jnp.concatenate([before, x_ref[rows, :], after], axis=0)
        else:
            x = x_ref[rows, :]
        gain = norm1_ref[...] * (1.0 + mod_ref[0, 1:2, :])
        h = (_rms(x, gain) + mod_ref[0, 0:1, :]).astype(BF16)
        yield
        proj = _dot(h, w_in_ref[...])
        yield
        u = proj[:, U_OFF:]
        zeros = jnp.zeros((POOL_HALO, POOL_WIDTH), F32)
        if halo:
            e[0:POOL_HALO, :] = jnp.where(at_seq_start, 0.0, u[0:POOL_HALO])
            e[POOL_HALO:POOL_HALO + sub, :] = u[POOL_HALO:POOL_HALO + sub]
            e[POOL_HALO + sub:2 * POOL_HALO + sub, :] = jnp.where(at_seq_end, 0.0, u[POOL_HALO + sub:])
            proj = proj[POOL_HALO:POOL_HALO + sub]
        else:
            e[0:POOL_HALO, :] = zeros
            e[POOL_HALO:POOL_HALO + sub, :] = u
            e[POOL_HALO + sub:2 * POOL_HALO + sub, :] = zeros
        e[2 * POOL_HALO + sub:, :] = zeros
        pooled = _pool_windows(e, c2, c4, sub, at_seq_start, at_seq_end)
        qa = _rms(proj[:, :Q_LORA_RANK], qn_ref[...]).astype(BF16)
        yield
        for g in range(len(POOL_WINDOWS)):
            cols = slice(g * POOL_GROUP, (g + 1) * POOL_GROUP)
            m_ref[rows, cols] = (_dot(pooled[g], wp_ref[g]) * ps_ref[:, cols]).astype(BF16)
        q = _dot(qa, wq_ref[...])
        yield
        if use_rope:
            pos = pl.multiple_of(pos0, sub)
            cq, sq = (r[pl.ds(pos, sub), :] for r in q_tabs)
        q_heads = []
        for hd in range(N_HEADS):
            slab = q[:, hd * HEAD_SLAB:(hd + 1) * HEAD_SLAB]
            if use_rope:
                swapped = q[:, QK_WIDTH + hd * HEAD_SLAB:QK_WIDTH + (hd + 1) * HEAD_SLAB]
                slab = slab * cq + swapped * sq
            else:
                slab = slab * Q_SCALE
            if self_attend:
                q_heads.append(slab.astype(BF16))
            else:
                q_ref[rows, hd * HEAD_SLAB:(hd + 1) * HEAD_SLAB] = slab.astype(BF16)
        ckv = _rms(proj[:, Q_LORA_RANK:KR_OFF], kvn_ref[...])
        kr = proj[:, KR_OFF:U_OFF]
        if emit_cache:
            ckv_ref[rows, :] = ckv
            kr_ref[rows, :] = kr[:, :QK_ROPE_DIM]
        if use_rope:
            kr = _rope_slab(kr, *(r[pl.ds(pos, sub), :] for r in k_tabs))
        ckv16 = ckv.astype(BF16)
        kin = jnp.concatenate([ckv16, kr.astype(BF16)], axis=1)
        yield
        keys = _dot(kin, wk_ref[...]).astype(BF16)
        vals = _dot(ckv16, wv_ref[...]).astype(BF16)
        if not self_attend:
            k_ref[rows, :] = keys
            v_ref[rows, :] = vals
            return
        yield

        def write(unit, out):
            a_ref[rows, (unit[0] - 1) * V_HEAD_DIM:(unit[0] + 1) * V_HEAD_DIM] = out

        _attention(
            [(hd,) for hd in range(N_HEADS)],
            q_of=lambda u: q_heads[u[0]],
            keys_of=[lambda u: keys[:, u[0] * HEAD_SLAB:(u[0] + 1) * HEAD_SLAB]],
            value_pairs_of=[lambda u: vals[:, (u[0] // 2) * HEAD_SLAB:(u[0] // 2 + 1) * HEAD_SLAB]],
            write_pair=write)

    _round_robin([sub_tile(j) for j in range(n_sub)])


def _stage1_call(x2d, mod3, mod_row_fn, weights, rope_tabs, seq, tm, sub, emit_cache, self_attend):
    n_tok = x2d.shape[0]
    assert not self_attend or sub == seq
    assert tm % sub == 0 and seq % sub == 0 and sub >= 2 * SUBLANES
    assert seq % tm == 0 or tm % seq == 0
    use_rope = rope_tabs is not None
    halo = seq > sub
    assert halo or sub == seq
    halo_blocks = n_tok // POOL_HALO
    per_tile = tm // POOL_HALO
    in_specs = [pl.BlockSpec((tm, D_MODEL), lambda i: (i, 0))]
    args = [x2d]
    if halo:
        in_specs += [
            pl.BlockSpec((POOL_HALO, D_MODEL), lambda i: (jnp.maximum(i * per_tile - 1, 0), 0)),
            pl.BlockSpec((POOL_HALO, D_MODEL),
                         lambda i: (jnp.minimum((i + 1) * per_tile, halo_blocks - 1), 0)),
        ]
        args += [x2d, x2d]
    in_specs += [pl.BlockSpec((1, 6, D_MODEL), lambda i: (mod_row_fn(i), 0, 0))]
    in_specs += [_const_spec(w.shape) for w in weights]
    args += [mod3] + list(weights)
    if use_rope:
        in_specs += [_resident_spec((seq, LANES))] * len(rope_tabs)
        args += list(rope_tabs)
    widths = [MLA_WIDTH] if self_attend else [QK_WIDTH, QK_WIDTH, MLA_WIDTH]
    widths.append(POOL_WIDTH)
    out_specs = [pl.BlockSpec((tm, w), lambda i: (i, 0)) for w in widths]
    out_shape = [jax.ShapeDtypeStruct((n_tok, w), BF16) for w in widths]
    if emit_cache:
        out_specs += [pl.BlockSpec((tm, KV_LORA_RANK), lambda i: (i, 0)),
                      pl.BlockSpec((tm, QK_ROPE_DIM), lambda i: (i, 0))]
        out_shape += [jax.ShapeDtypeStruct((n_tok, KV_LORA_RANK), F32),
                      jax.ShapeDtypeStruct((n_tok, QK_ROPE_DIM), F32)]
    stencil_rows = sub + 3 * POOL_HALO
    return pl.pallas_call(
        functools.partial(_stage1_kernel, use_rope=use_rope, halo=halo, emit_cache=emit_cache,
                          self_attend=self_attend, tm=tm, sub=sub, seq=seq),
        grid=(n_tok // tm,),
        in_specs=in_specs,
        out_specs=out_specs,
        out_shape=out_shape,
        scratch_shapes=[pltpu.VMEM((tm // sub, stencil_rows, POOL_WIDTH), F32)] * 3,
        compiler_params=pltpu.CompilerParams(dimension_semantics=("parallel",),
                                             vmem_limit_bytes=VMEM_LIMIT),
        name="stage1_rope" if use_rope else "stage1_ctx",
    )(*args)


def _qk(qh, kh):
    return lax.dot_general(qh, kh, (((1,), (1,)), ((), ())), preferred_element_type=F32)


def _attention(units, q_of, keys_of, value_pairs_of, write_pair):
    def scores(unit):
        qh = q_of(unit)
        return [_qk(qh, k(unit)) for k in keys_of]

    def values(get, unit):
        vp = get(unit)
        own = (lax.broadcasted_iota(jnp.int32, vp.shape, 1) < V_HEAD_DIM) == (unit[-1] % 2 == 0)
        return jnp.where(own, vp, jnp.ones_like(vp))

    nxt = scores(units[0])
    for idx, unit in enumerate(units):
        blocks = nxt
        if idx + 1 < len(units):
            nxt = scores(units[idx + 1])
        m = functools.reduce(jnp.maximum, [jnp.max(s, axis=-1, keepdims=True) for s in blocks])
        o = functools.reduce(jnp.add, [_dot(jnp.exp2(s - m).astype(BF16), values(v, unit))
                                       for s, v in zip(blocks, value_pairs_of)])
        if unit[-1] % 2 == 0:
            o_even = o
        else:
            low = lax.broadcasted_iota(jnp.int32, o.shape, 1) < V_HEAD_DIM
            num = jnp.where(low, o_even, o)
            den = pltpu.roll(jnp.where(low, o, o_even), V_HEAD_DIM, 1)
            write_pair(unit, (num * (1.0 / den)).astype(BF16))


def _latent_attn_kernel(*refs, n_casts):
    q_ref, k_ref, v_ref, cckv_ref, ckr_ref, wk_ref, wv_ref = refs[:7]
    o_ref = refs[7 + n_casts]
    kc_ref, vc_ref = refs[-2:]
    for src, dst in zip(refs[7:7 + n_casts], refs[8 + n_casts:-2]):
        dst[...] = src[...].astype(BF16)

    @pl.when(pl.program_id(1) == 0)
    def _():
        ckv16 = cckv_ref[0].astype(BF16)
        kr = ckr_ref[0].astype(BF16)
        pad = jnp.zeros((kr.shape[0], LANES - QK_ROPE_DIM), BF16)
        kc_ref[...] = _dot(jnp.concatenate([ckv16, kr, pad], axis=1), wk_ref[...]).astype(BF16)
        vc_ref[...] = _dot(ckv16, wv_ref[...]).astype(BF16)

    slab = lambda hd: slice(hd * HEAD_SLAB, (hd + 1) * HEAD_SLAB)
    pair = lambda hd: slice((hd // 2) * HEAD_SLAB, (hd // 2 + 1) * HEAD_SLAB)

    def write(unit, out):
        o_ref[0, :, (unit[0] - 1) * V_HEAD_DIM:(unit[0] + 1) * V_HEAD_DIM] = out

    _attention([(hd,) for hd in range(N_HEADS)],
               q_of=lambda u: q_ref[0, :, slab(u[0])],
               keys_of=[lambda u: k_ref[0, :, slab(u[0])], lambda u: kc_ref[:, slab(u[0])]],
               value_pairs_of=[lambda u: v_ref[0, :, pair(u[0])], lambda u: vc_ref[:, pair(u[0])]],
               write_pair=write)


def _latent_attn_call(q3, k3, v3, cache_ckv3, cache_kr3, wk, wv, tq, casts):
    b, s, _ = q3.shape
    nk, nc = k3.shape[1], cache_ckv3.shape[1]
    steps_i = s // tq
    n_steps = b * steps_i
    in_specs = [
        pl.BlockSpec((1, tq, QK_WIDTH), lambda bi, i: (bi, i, 0)),
        pl.BlockSpec((1, nk, QK_WIDTH), lambda bi, i: (bi, 0, 0)),
        pl.BlockSpec((1, nk, MLA_WIDTH), lambda bi, i: (bi, 0, 0)),
        pl.BlockSpec((1, nc, KV_LORA_RANK), lambda bi, i: (bi, 0, 0)),
        pl.BlockSpec((1, nc, QK_ROPE_DIM), lambda bi, i: (bi, 0, 0)),
        pl.BlockSpec(wk.shape, lambda bi, i: (0, 0)),
        pl.BlockSpec(wv.shape, lambda bi, i: (0, 0)),
    ]
    args = [q3, k3, v3, cache_ckv3, cache_kr3, wk, wv]
    out_specs = [pl.BlockSpec((1, tq, MLA_WIDTH), lambda bi, i: (bi, i, 0))]
    out_shape = [jax.ShapeDtypeStruct((b, s, MLA_WIDTH), BF16)]
    for w in casts:
        assert w.shape[0] % (n_steps * 2 * SUBLANES) == 0
        blk = (w.shape[0] // n_steps, w.shape[1])
        spec = pl.BlockSpec(blk, lambda bi, i: (bi * steps_i + i, 0))
        in_specs.append(spec)
        out_specs.append(spec)
        args.append(w)
        out_shape.append(jax.ShapeDtypeStruct(w.shape, BF16))
    return pl.pallas_call(
        functools.partial(_latent_attn_kernel, n_casts=len(casts)),
        grid=(b, steps_i),
        in_specs=in_specs,
        out_specs=out_specs,
        out_shape=out_shape,
        scratch_shapes=[pltpu.VMEM((nc, QK_WIDTH), BF16), pltpu.VMEM((nc, MLA_WIDTH), BF16)],
        compiler_params=pltpu.CompilerParams(
            dimension_semantics=("parallel", "arbitrary"),
            vmem_limit_bytes=VMEM_LIMIT),
        name="attn_lat",
    )(*args)


def _post_kernel(x_ref, a_ref, m_ref, mod_ref, wo_ref, norm2_ref, w1_ref, w2_ref, fn_ref, o_ref,
                 *, tm, sub, ff_chunk):
    def sub_tile(j):
        rows = slice(j * sub, (j + 1) * sub)
        cat = jnp.concatenate([a_ref[rows, :], m_ref[rows, :]], axis=1)
        y = x_ref[rows, :] + mod_ref[0, 2:3, :] * _dot(cat, wo_ref[...])
        gain = norm2_ref[...] * (1.0 + mod_ref[0, 4:5, :])
        h2 = (_rms(y, gain) + mod_ref[0, 3:4, :]).astype(BF16)
        yield
        acc = jnp.zeros((sub, D_MODEL), F32)
        for c in range(D_FF // ff_chunk):
            a = _dot(h2, w1_ref[:, c * ff_chunk:(c + 1) * ff_chunk])
            a = jnp.square(jnp.maximum(a, 0.0)).astype(BF16)
            acc = acc + _dot(a, w2_ref[c * ff_chunk:(c + 1) * ff_chunk, :])
            yield
        x2 = y + mod_ref[0, 5:6, :] * acc
        o_ref[rows, :] = _rms(x2, fn_ref[...])

    _round_robin([sub_tile(j) for j in range(tm // sub)])


def _post_call(x2d, attn2d, mixed2d, mod3, mod_row_fn, weights, tm, sub):
    n_tok = x2d.shape[0]
    assert tm % sub == 0
    w_o, norm2, w1, w2, final_norm = weights
    in_specs = [
        pl.BlockSpec((tm, D_MODEL), lambda i: (i, 0)),
        pl.BlockSpec((tm, MLA_WIDTH), lambda i: (i, 0)),
        pl.BlockSpec((tm, POOL_WIDTH), lambda i: (i, 0)),
        pl.BlockSpec((1, 6, D_MODEL), lambda i: (mod_row_fn(i), 0, 0)),
        _resident_spec(w_o.shape), _resident_spec(norm2.shape), _resident_spec(w1.shape),
        _resident_spec(w2.shape), _resident_spec(final_norm.shape),
    ]
    return pl.pallas_call(
        functools.partial(_post_kernel, tm=tm, sub=sub, ff_chunk=1024),
        grid=(n_tok // tm,),
        in_specs=in_specs,
        out_specs=pl.BlockSpec((tm, D_MODEL), lambda i: (i, 0)),
        out_shape=jax.ShapeDtypeStruct((n_tok, D_MODEL), F32),
        compiler_params=pltpu.CompilerParams(dimension_semantics=("parallel",),
                                             vmem_limit_bytes=VMEM_LIMIT),
        name="post",
    )(x2d, attn2d, mixed2d, mod3, w_o, norm2, w1, w2, final_norm)


def _pack_w_in(w_in):
    qa = w_in[:, :Q_LORA_RANK + KV_LORA_RANK]
    kr = w_in[:, KR_OFF:KR_OFF + QK_ROPE_DIM]
    u = w_in[:, KR_OFF + QK_ROPE_DIM:]
    pad = jnp.zeros((D_MODEL, LANES - QK_ROPE_DIM), w_in.dtype)
    return jnp.concatenate([qa, kr, pad, u], axis=1).astype(BF16)


def _pack_wq(w_qb, with_swapped):
    w = w_qb.reshape(Q_LORA_RANK, N_HEADS, QK_DIM)
    tail = ((0, 0), (0, 0), (0, HEAD_SLAB - QK_DIM))
    plain = jnp.pad(w, tail).reshape(Q_LORA_RANK, QK_WIDTH)
    if not with_swapped:
        return plain.astype(BF16)
    r = w[:, :, QK_NOPE_DIM:].reshape(Q_LORA_RANK, N_HEADS, 2, 2, N_FREQ)
    sw = jnp.stack([-r[:, :, :, 1, :], r[:, :, :, 0, :]], axis=3).reshape(Q_LORA_RANK, N_HEADS, QK_ROPE_DIM)
    sw = jnp.concatenate([jnp.zeros((Q_LORA_RANK, N_HEADS, QK_NOPE_DIM), w.dtype), sw], axis=2)
    swapped = jnp.pad(sw, tail).reshape(Q_LORA_RANK, QK_WIDTH)
    return jnp.concatenate([plain, swapped], axis=1).astype(BF16)


def _pack_wkv(w_kvb):
    w = w_kvb.reshape(KV_LORA_RANK, N_HEADS, QK_NOPE_DIM + V_HEAD_DIM)
    wk_nope = jnp.pad(w[:, :, :QK_NOPE_DIM], ((0, 0), (0, 0), (0, HEAD_SLAB - QK_NOPE_DIM)))
    sel = jnp.pad(jnp.eye(QK_ROPE_DIM, dtype=w_kvb.dtype),
                  ((0, 0), (QK_NOPE_DIM, HEAD_SLAB - QK_DIM)))
    sel = jnp.broadcast_to(sel[:, None, :], (QK_ROPE_DIM, N_HEADS, HEAD_SLAB))
    zero = jnp.zeros((LANES - QK_ROPE_DIM, N_HEADS, HEAD_SLAB), w_kvb.dtype)
    wk = jnp.concatenate([wk_nope, sel, zero], axis=0).reshape(KV_LORA_RANK + LANES, QK_WIDTH)
    wv = w[:, :, QK_NOPE_DIM:].reshape(KV_LORA_RANK, MLA_WIDTH)
    return wk.astype(BF16), wv.astype(BF16)


def _rope_tables(n_tokens):
    f = np.float32
    rows = n_tokens // GRID_W
    row = np.repeat(np.arange(rows, dtype=f), GRID_W)
    col = np.tile(np.arange(GRID_W, dtype=f), rows)
    inv_freq = (1.0 / (f(ROPE_THETA) ** (np.arange(N_FREQ, dtype=f) * f(2.0) / f(AXIS_ROPE)))).astype(f)
    cos_r, sin_r = np.cos(row[:, None] * inv_freq), np.sin(row[:, None] * inv_freq)
    cos_c, sin_c = np.cos(col[:, None] * inv_freq), np.sin(col[:, None] * inv_freq)
    z = np.zeros_like(cos_r)
    c32 = np.concatenate([cos_r, cos_r, cos_c, cos_c], axis=1)
    s32 = np.concatenate([sin_r, sin_r, sin_c, sin_c], axis=1)
    up32 = np.concatenate([z, sin_r, z, sin_c], axis=1)
    dn32 = np.concatenate([-sin_r, z, -sin_c, z], axis=1)

    def place(t32, off, fill):
        left = np.full((n_tokens, off), fill, f)
        right = np.zeros((n_tokens, LANES - off - QK_ROPE_DIM), f)
        return np.concatenate([left, t32.astype(f), right], axis=1)

    q_tabs = (place(c32 * f(Q_SCALE), QK_NOPE_DIM, Q_SCALE), place(s32 * f(Q_SCALE), QK_NOPE_DIM, 0.0))
    k_tabs = (place(c32, 0, 0.0), place(up32, 0, 0.0), place(dn32, 0, 0.0))
    return q_tabs + k_tabs


def kernel(x_prompt, x_sample, cache_ckv, cache_krope, c, c_ctx, w_ada, b_ada, norm1, w_in, q_norm,
           w_qb, kv_norm, w_kvb, w_pool, pool_scale, w_o, norm2, w1, w2, final_norm):
    batch, seq, _ = x_prompt.shape
    dec_batch, dec_seq, _ = x_sample.shape
    depth = w_in.shape[0]
    past = cache_ckv.shape[2]
    assert depth == 1, "single trunk layer"
    assert dec_batch + 1 <= MOD_ROWS

    c_all = jnp.concatenate(
        [c, c_ctx[None, :], jnp.zeros((MOD_ROWS - dec_batch - 1, D_MODEL), F32)], axis=0)
    mod3 = _mod_call(c_all, w_ada[0], b_ada[0][None, :]).reshape(MOD_ROWS, 6, D_MODEL)

    wk, wv = _pack_wkv(w_kvb[0])

    def s1_weights(with_swapped):
        return (norm1[0][None, :], _pack_w_in(w_in[0]), q_norm[0][None, :],
                _pack_wq(w_qb[0], with_swapped), kv_norm[0][None, :], wk, wv,
                w_pool[0].astype(BF16), pool_scale[0][None, :])

    xc = x_prompt.reshape(batch * seq, D_MODEL)
    ctx_row = lambda i: dec_batch
    attn_c, mc, ckv_new, kr_new = _stage1_call(xc, mod3, ctx_row, s1_weights(False), None, seq,
                                               S1_TILE, seq, True, True)

    xl = x_sample.reshape(dec_batch * dec_seq, D_MODEL)
    lat_row = lambda i: i // (dec_seq // S1_TILE)
    ql, kl, vl, ml = _stage1_call(xl, mod3, lat_row, s1_weights(True), _rope_tables(dec_seq), dec_seq,
                                  S1_TILE, S1_SUB, False, False)
    attn_l, w1_16, w2_16 = _latent_attn_call(
        ql.reshape(dec_batch, dec_seq, QK_WIDTH), kl.reshape(dec_batch, dec_seq, QK_WIDTH),
        vl.reshape(dec_batch, dec_seq, MLA_WIDTH),
        cache_ckv.reshape(dec_batch, past, KV_LORA_RANK), cache_krope.reshape(dec_batch, past, QK_ROPE_DIM),
        wk, wv, LAT_Q_TILE, casts=(w1[0], w2[0]))

    post_weights = (w_o[0].astype(BF16), norm2[0][None, :], w1_16, w2_16, final_norm[None, :])
    y_prompt = _post_call(xc, attn_c, mc, mod3, ctx_row,
                          post_weights, POST_TILE, POST_SUB)
    post_row = lambda i: i // (dec_seq // POST_TILE)
    y_sample = _post_call(xl, attn_l.reshape(dec_batch * dec_seq, MLA_WIDTH), ml, mod3, post_row,
                          post_weights, POST_TILE, POST_SUB)

    return (y_prompt.reshape(batch, seq, D_MODEL), y_sample.reshape(dec_batch, dec_seq, D_MODEL),
            ckv_new.reshape(batch, 1, seq, KV_LORA_RANK), kr_new.reshape(batch, 1, seq, QK_ROPE_DIM))
```

```python
import functools
import math

import jax
import jax.numpy as jnp
import numpy as np
from jax import lax
from jax.experimental import pallas as pl
from jax.experimental.pallas import tpu as pltpu

D_MODEL = 1024
N_HEADS = 8
QK_NOPE_DIM = 64
QK_ROPE_DIM = 32
V_HEAD_DIM = 64
QK_DIM = QK_NOPE_DIM + QK_ROPE_DIM
Q_LORA_RANK = 256
KV_LORA_RANK = 128
MLA_WIDTH = N_HEADS * V_HEAD_DIM
POOL_WIDTH = D_MODEL - MLA_WIDTH
POOL_WINDOWS = (2, 4, 8, 16)
POOL_GROUP = POOL_WIDTH // len(POOL_WINDOWS)
D_FF = 4 * D_MODEL
GRID_W = 64
ROPE_THETA = 10000.0
AXIS_ROPE = QK_ROPE_DIM // 2
N_FREQ = AXIS_ROPE // 2
EPS = 1e-6
ATTN_SCALE = 1.0 / math.sqrt(QK_DIM)
Q_SCALE = ATTN_SCALE * math.log2(math.e)

LANES = 128
SUBLANES = 8
HEAD_SLAB = LANES
QK_WIDTH = N_HEADS * HEAD_SLAB
MOD_ROWS = 16
POOL_HALO = max(POOL_WINDOWS) // 2
assert POOL_HALO == SUBLANES
KR_OFF = Q_LORA_RANK + KV_LORA_RANK
U_OFF = KR_OFF + LANES
PROJ_COLS = U_OFF + POOL_WIDTH
VMEM_LIMIT = 56 * 1024 * 1024
S1_TILE = 1024
S1_SUB = 256
LAT_Q_TILE = 512
POST_TILE = 1024
POST_SUB = 256

F32 = jnp.float32
BF16 = jnp.bfloat16


def _dot(a, b):
    return jnp.dot(a, b, preferred_element_type=F32)


def _rms(x, g):
    return x * lax.rsqrt(jnp.mean(x * x, axis=-1, keepdims=True) + EPS) * g


def _round_robin(phased):
    active = list(phased)
    while active:
        for gen in list(active):
            if next(gen, "done") == "done":
                active.remove(gen)


def _const_spec(shape):
    nd = len(shape)
    return pl.BlockSpec(shape, lambda i: (0,) * nd)


def _resident_spec(shape):
    nd = len(shape)
    return pl.BlockSpec(shape, lambda i: (0,) * nd, pipeline_mode=pl.Buffered(1))


def _mod_kernel(c_ref, w_ref, b_ref, o_ref):
    c = c_ref[...]
    s = (c * jax.nn.sigmoid(c)).astype(BF16)
    o_ref[...] = _dot(s, w_ref[...].astype(BF16)) + b_ref[...]


def _mod_call(c_all, w_ada, b_ada):
    tn = 1024
    n = w_ada.shape[1]
    return pl.pallas_call(
        _mod_kernel,
        grid=(n // tn,),
        in_specs=[
            pl.BlockSpec((MOD_ROWS, D_MODEL), lambda j: (0, 0)),
            pl.BlockSpec((D_MODEL, tn), lambda j: (0, j)),
            pl.BlockSpec((1, tn), lambda j: (0, j)),
        ],
        out_specs=pl.BlockSpec((MOD_ROWS, tn), lambda j: (0, j)),
        out_shape=jax.ShapeDtypeStruct((MOD_ROWS, n), F32),
        compiler_params=pltpu.CompilerParams(dimension_semantics=("arbitrary",)),
        name="mod",
    )(c_all, w_ada, b_ada)


def _rope_slab(slab, c, s_up, s_dn):
    return (slab * c + pltpu.roll(slab, N_FREQ, 1) * s_up
            + pltpu.roll(slab, LANES - N_FREQ, 1) * s_dn)


def _inv_counts(w, tm, at_seq_start, at_seq_end):
    half = w // 2
    r = lax.broadcasted_iota(jnp.int32, (SUBLANES, LANES), 0)
    first = jnp.where(at_seq_start, (r + half) - jnp.maximum(r - half, 0), w)
    last = jnp.where(at_seq_end, jnp.minimum(half, SUBLANES - r) + half, w)
    mid = jnp.full((tm - 2 * SUBLANES, LANES), 1.0 / w, F32)
    return jnp.concatenate([1.0 / first.astype(F32), mid, 1.0 / last.astype(F32)], axis=0)


def _pool_windows(e_ref, c2_ref, c4_ref, tm, at_seq_start, at_seq_end):
    rows = tm + 2 * POOL_HALO
    g = [slice(i * POOL_GROUP, (i + 1) * POOL_GROUP) for i in range(len(POOL_WINDOWS))]
    wide = slice(POOL_GROUP, POOL_WIDTH)
    wider = slice(2 * POOL_GROUP, POOL_WIDTH)
    zeros = jnp.zeros((SUBLANES, POOL_WIDTH), F32)
    c2_ref[rows:, :] = zeros
    c4_ref[rows:, :] = zeros
    c2_ref[0:rows, wide] = e_ref[0:rows, wide] + e_ref[1:rows + 1, wide]
    c4_ref[0:rows, wider] = c2_ref[0:rows, wider] + c2_ref[2:rows + 2, wider]
    c8 = c4_ref[0:rows, g[3]] + c4_ref[4:rows + 4, g[3]]
    sums = [
        e_ref[7:7 + tm, g[0]] + e_ref[8:8 + tm, g[0]],
        c2_ref[6:6 + tm, g[1]] + c2_ref[8:8 + tm, g[1]],
        c4_ref[4:4 + tm, g[2]] + c4_ref[8:8 + tm, g[2]],
        c8[0:tm] + c8[8:8 + tm],
    ]
    return [(sums[i] * _inv_counts(w, tm, at_seq_start, at_seq_end)
             - e_ref[POOL_HALO:POOL_HALO + tm, g[i]]).astype(BF16)
            for i, w in enumerate(POOL_WINDOWS)]


def _stage1_kernel(*refs, use_rope, halo, emit_cache, self_attend, tm, sub, seq):
    it = iter(refs)
    x_ref = next(it)
    if halo:
        xp_ref, xn_ref = next(it), next(it)
    (mod_ref, norm1_ref, w_in_ref, qn_ref, wq_ref, kvn_ref, wk_ref, wv_ref, wp_ref,
     ps_ref) = (next(it) for _ in range(10))
    if use_rope:
        q_tabs = [next(it) for _ in range(2)]
        k_tabs = [next(it) for _ in range(3)]
    if self_attend:
        a_ref, m_ref = next(it), next(it)
    else:
        q_ref, k_ref, v_ref, m_ref = (next(it) for _ in range(4))
    if emit_cache:
        ckv_ref, kr_ref = next(it), next(it)
    e_ref, c2_ref, c4_ref = next(it), next(it), next(it)

    n_sub = tm // sub

    def sub_tile(j):
        r0 = j * sub
        rows = slice(r0, r0 + sub)
        pos0 = (pl.program_id(0) * tm + r0) % seq
        at_seq_start = pos0 == 0
        at_seq_end = pos0 + sub == seq
        e, c2, c4 = e_ref.at[j], c2_ref.at[j], c4_ref.at[j]

        if halo:
            before = xp_ref[...] if j == 0 else x_ref[r0 - POOL_HALO:r0, :]
            after = xn_ref[...] if j == n_sub - 1 else x_ref[r0 + sub:r0 + sub + POOL_HALO, :]
            x = jnp.concatenate([before, x_ref[rows, :], after], axis=0)
        else:
            x = x_ref[rows, :]
        gain = norm1_ref[...] * (1.0 + mod_ref[0, 1:2, :])
        h = (_rms(x, gain) + mod_ref[0, 0:1, :]).astype(BF16)
        yield
        proj = _dot(h, w_in_ref[...])
        yield
        u = proj[:, U_OFF:]
        zeros = jnp.zeros((POOL_HALO, POOL_WIDTH), F32)
        if halo:
            e[0:POOL_HALO, :] = jnp.where(at_seq_start, 0.0, u[0:POOL_HALO])
            e[POOL_HALO:POOL_HALO + sub, :] = u[POOL_HALO:POOL_HALO + sub]
            e[POOL_HALO + sub:2 * POOL_HALO + sub, :] = jnp.where(at_seq_end, 0.0, u[POOL_HALO + sub:])
            proj = proj[POOL_HALO:POOL_HALO + sub]
        else:
            e[0:POOL_HALO, :] = zeros
            e[POOL_HALO:POOL_HALO + sub, :] = u
            e[POOL_HALO + sub:2 * POOL_HALO + sub, :] = zeros
        e[2 * POOL_HALO + sub:, :] = zeros
        pooled = _pool_windows(e, c2, c4, sub, at_seq_start, at_seq_end)
        qa = _rms(proj[:, :Q_LORA_RANK], qn_ref[...]).astype(BF16)
        if use_rope:
            pos = pl.multiple_of(pos0, sub)
        ckv = _rms(proj[:, Q_LORA_RANK:KR_OFF], kvn_ref[...])
        kr = proj[:, KR_OFF:U_OFF]
        if emit_cache:
            ckv_ref[rows, :] = ckv
            kr_ref[rows, :] = kr[:, :QK_ROPE_DIM]
        if use_rope:
            kr = _rope_slab(kr, *(r[pl.ds(pos, sub), :] for r in k_tabs))
        ckv16 = ckv.astype(BF16)
        kin = jnp.concatenate([ckv16, kr.astype(BF16)], axis=1)
        yield
        q = _dot(qa, wq_ref[...])
        keys = _dot(kin, wk_ref[...]).astype(BF16)
        vals = _dot(ckv16, wv_ref[...]).astype(BF16)
        yield
        if use_rope:
            cq, sq = (r[pl.ds(pos, sub), :] for r in q_tabs)
        q_heads = []
        for hd in range(N_HEADS):
            slab = q[:, hd * HEAD_SLAB:(hd + 1) * HEAD_SLAB]
            if use_rope:
                swapped = q[:, QK_WIDTH + hd * HEAD_SLAB:QK_WIDTH + (hd + 1) * HEAD_SLAB]
                slab = slab * cq + swapped * sq
            else:
                slab = slab * Q_SCALE
            if self_attend:
                q_heads.append(slab.astype(BF16))
            else:
                q_ref[rows, hd * HEAD_SLAB:(hd + 1) * HEAD_SLAB] = slab.astype(BF16)
        yield
        for g in range(len(POOL_WINDOWS)):
            cols = slice(g * POOL_GROUP, (g + 1) * POOL_GROUP)
            m_ref[rows, cols] = (_dot(pooled[g], wp_ref[g]) * ps_ref[:, cols]).astype(BF16)
        if not self_attend:
            k_ref[rows, :] = keys
            v_ref[rows, :] = vals
            return
        yield

        def write(unit, out):
            a_ref[rows, (unit[0] - 1) * V_HEAD_DIM:(unit[0] + 1) * V_HEAD_DIM] = out

        _attention(
            [(hd,) for hd in range(N_HEADS)],
            q_of=lambda u: q_heads[u[0]],
            keys_of=[lambda u: keys[:, u[0] * HEAD_SLAB:(u[0] + 1) * HEAD_SLAB]],
            value_pairs_of=[lambda u: vals[:, (u[0] // 2) * HEAD_SLAB:(u[0] // 2 + 1) * HEAD_SLAB]],
            write_pair=write)

    _round_robin([sub_tile(j) for j in range(n_sub)])


def _stage1_call(x2d, mod3, mod_row_fn, weights, rope_tabs, seq, tm, sub, emit_cache, self_attend):
    n_tok = x2d.shape[0]
    assert not self_attend or sub == seq
    assert tm % sub == 0 and seq % sub == 0 and sub >= 2 * SUBLANES
    assert seq % tm == 0 or tm % seq == 0
    use_rope = rope_tabs is not None
    halo = seq > sub
    assert halo or sub == seq
    halo_blocks = n_tok // POOL_HALO
    per_tile = tm // POOL_HALO
    in_specs = [pl.BlockSpec((tm, D_MODEL), lambda i: (i, 0))]
    args = [x2d]
    if halo:
        in_specs += [
            pl.BlockSpec((POOL_HALO, D_MODEL), lambda i: (jnp.maximum(i * per_tile - 1, 0), 0)),
            pl.BlockSpec((POOL_HALO, D_MODEL),
                         lambda i: (jnp.minimum((i + 1) * per_tile, halo_blocks - 1), 0)),
        ]
        args += [x2d, x2d]
    in_specs += [pl.BlockSpec((1, 6, D_MODEL), lambda i: (mod_row_fn(i), 0, 0))]
    in_specs += [_const_spec(w.shape) for w in weights]
    args += [mod3] + list(weights)
    if use_rope:
        in_specs += [_resident_spec((seq, LANES))] * len(rope_tabs)
        args += list(rope_tabs)
    widths = [MLA_WIDTH] if self_attend else [QK_WIDTH, QK_WIDTH, MLA_WIDTH]
    widths.append(POOL_WIDTH)
    out_specs = [pl.BlockSpec((tm, w), lambda i: (i, 0)) for w in widths]
    out_shape = [jax.ShapeDtypeStruct((n_tok, w), BF16) for w in widths]
    if emit_cache:
        out_specs += [pl.BlockSpec((tm, KV_LORA_RANK), lambda i: (i, 0)),
                      pl.BlockSpec((tm, QK_ROPE_DIM), lambda i: (i, 0))]
        out_shape += [jax.ShapeDtypeStruct((n_tok, KV_LORA_RANK), F32),
                      jax.ShapeDtypeStruct((n_tok, QK_ROPE_DIM), F32)]
    stencil_rows = sub + 3 * POOL_HALO
    return pl.pallas_call(
        functools.partial(_stage1_kernel, use_rope=use_rope, halo=halo, emit_cache=emit_cache,
                          self_attend=self_attend, tm=tm, sub=sub, seq=seq),
        grid=(n_tok // tm,),
        in_specs=in_specs,
        out_specs=out_specs,
        out_shape=out_shape,
        scratch_shapes=[pltpu.VMEM((tm // sub, stencil_rows, POOL_WIDTH), F32)] * 3,
        compiler_params=pltpu.CompilerParams(dimension_semantics=("parallel",),
                                             vmem_limit_bytes=VMEM_LIMIT),
        name="stage1_rope" if use_rope else "stage1_ctx",
    )(*args)


def _qk(qh, kh):
    return lax.dot_general(qh, kh, (((1,), (1,)), ((), ())), preferred_element_type=F32)


def _attention(units, q_of, keys_of, value_pairs_of, write_pair):
    def scores(unit):
        qh = q_of(unit)
        return [_qk(qh, k(unit)) for k in keys_of]

    def values(get, unit):
        vp = get(unit)
        own = (lax.broadcasted_iota(jnp.int32, vp.shape, 1) < V_HEAD_DIM) == (unit[-1] % 2 == 0)
        return jnp.where(own, vp, jnp.ones_like(vp))

    nxt = scores(units[0])
    for idx, unit in enumerate(units):
        blocks = nxt
        if idx + 1 < len(units):
            nxt = scores(units[idx + 1])
        m = functools.reduce(jnp.maximum, [jnp.max(s, axis=-1, keepdims=True) for s in blocks])
        o = functools.reduce(jnp.add, [_dot(jnp.exp2(s - m).astype(BF16), values(v, unit))
                                       for s, v in zip(blocks, value_pairs_of)])
        if unit[-1] % 2 == 0:
            o_even = o
        else:
            low = lax.broadcasted_iota(jnp.int32, o.shape, 1) < V_HEAD_DIM
            num = jnp.where(low, o_even, o)
            den = pltpu.roll(jnp.where(low, o, o_even), V_HEAD_DIM, 1)
            write_pair(unit, (num * (1.0 / den)).astype(BF16))


def _latent_attn_kernel(*refs, n_casts):
    q_ref, k_ref, v_ref, cckv_ref, ckr_ref, wk_ref, wv_ref = refs[:7]
    o_ref = refs[7 + n_casts]
    kc_ref, vc_ref = refs[-2:]
    for src, dst in zip(refs[7:7 + n_casts], refs[8 + n_casts:-2]):
        dst[...] = src[...].astype(BF16)

    @pl.when(pl.program_id(1) == 0)
    def _():
        ckv16 = cckv_ref[0].astype(BF16)
        kr = ckr_ref[0].astype(BF16)
        pad = jnp.zeros((kr.shape[0], LANES - QK_ROPE_DIM), BF16)
        kc_ref[...] = _dot(jnp.concatenate([ckv16, kr, pad], axis=1), wk_ref[...]).astype(BF16)
        vc_ref[...] = _dot(ckv16, wv_ref[...]).astype(BF16)

    slab = lambda hd: slice(hd * HEAD_SLAB, (hd + 1) * HEAD_SLAB)
    pair = lambda hd: slice((hd // 2) * HEAD_SLAB, (hd // 2 + 1) * HEAD_SLAB)

    def write(unit, out):
        o_ref[0, :, (unit[0] - 1) * V_HEAD_DIM:(unit[0] + 1) * V_HEAD_DIM] = out

    _attention([(hd,) for hd in range(N_HEADS)],
               q_of=lambda u: q_ref[0, :, slab(u[0])],
               keys_of=[lambda u: k_ref[0, :, slab(u[0])], lambda u: kc_ref[:, slab(u[0])]],
               value_pairs_of=[lambda u: v_ref[0, :, pair(u[0])], lambda u: vc_ref[:, pair(u[0])]],
               write_pair=write)


def _latent_attn_call(q3, k3, v3, cache_ckv3, cache_kr3, wk, wv, tq, casts):
    b, s, _ = q3.shape
    nk, nc = k3.shape[1], cache_ckv3.shape[1]
    steps_i = s // tq
    n_steps = b * steps_i
    in_specs = [
        pl.BlockSpec((1, tq, QK_WIDTH), lambda bi, i: (bi, i, 0)),
        pl.BlockSpec((1, nk, QK_WIDTH), lambda bi, i: (bi, 0, 0)),
        pl.BlockSpec((1, nk, MLA_WIDTH), lambda bi, i: (bi, 0, 0)),
        pl.BlockSpec((1, nc, KV_LORA_RANK), lambda bi, i: (bi, 0, 0)),
        pl.BlockSpec((1, nc, QK_ROPE_DIM), lambda bi, i: (bi, 0, 0)),
        pl.BlockSpec(wk.shape, lambda bi, i: (0, 0)),
        pl.BlockSpec(wv.shape, lambda bi, i: (0, 0)),
    ]
    args = [q3, k3, v3, cache_ckv3, cache_kr3, wk, wv]
    out_specs = [pl.BlockSpec((1, tq, MLA_WIDTH), lambda bi, i: (bi, i, 0))]
    out_shape = [jax.ShapeDtypeStruct((b, s, MLA_WIDTH), BF16)]
    for w in casts:
        assert w.shape[0] % (n_steps * 2 * SUBLANES) == 0
        blk = (w.shape[0] // n_steps, w.shape[1])
        spec = pl.BlockSpec(blk, lambda bi, i: (bi * steps_i + i, 0))
        in_specs.append(spec)
        out_specs.append(spec)
        args.append(w)
        out_shape.append(jax.ShapeDtypeStruct(w.shape, BF16))
    return pl.pallas_call(
        functools.partial(_latent_attn_kernel, n_casts=len(casts)),
        grid=(b, steps_i),
        in_specs=in_specs,
        out_specs=out_specs,
        out_shape=out_shape,
        scratch_shapes=[pltpu.VMEM((nc, QK_WIDTH), BF16), pltpu.VMEM((nc, MLA_WIDTH), BF16)],
        compiler_params=pltpu.CompilerParams(
            dimension_semantics=("parallel", "arbitrary"),
            vmem_limit_bytes=VMEM_LIMIT),
        name="attn_lat",
    )(*args)


def _post_kernel(x_ref, a_ref, m_ref, mod_ref, wo_ref, norm2_ref, w1_ref, w2_ref, fn_ref, o_ref,
                 *, tm, sub, ff_chunk):
    def sub_tile(j):
        rows = slice(j * sub, (j + 1) * sub)
        cat = jnp.concatenate([a_ref[rows, :], m_ref[rows, :]], axis=1)
        y = x_ref[rows, :] + mod_ref[0, 2:3, :] * _dot(cat, wo_ref[...])
        gain = norm2_ref[...] * (1.0 + mod_ref[0, 4:5, :])
        h2 = (_rms(y, gain) + mod_ref[0, 3:4, :]).astype(BF16)
        yield
        acc = jnp.zeros((sub, D_MODEL), F32)
        for c in range(D_FF // ff_chunk):
            a = _dot(h2, w1_ref[:, c * ff_chunk:(c + 1) * ff_chunk])
            a = jnp.square(jnp.maximum(a, 0.0)).astype(BF16)
            acc = acc + _dot(a, w2_ref[c * ff_chunk:(c + 1) * ff_chunk, :])
            yield
        x2 = y + mod_ref[0, 5:6, :] * acc
        o_ref[rows, :] = _rms(x2, fn_ref[...])

    _round_robin([sub_tile(j) for j in range(tm // sub)])


def _post_call(x2d, attn2d, mixed2d, mod3, mod_row_fn, weights, tm, sub):
    n_tok = x2d.shape[0]
    assert tm % sub == 0
    w_o, norm2, w1, w2, final_norm = weights
    in_specs = [
        pl.BlockSpec((tm, D_MODEL), lambda i: (i, 0)),
        pl.BlockSpec((tm, MLA_WIDTH), lambda i: (i, 0)),
        pl.BlockSpec((tm, POOL_WIDTH), lambda i: (i, 0)),
        pl.BlockSpec((1, 6, D_MODEL), lambda i: (mod_row_fn(i), 0, 0)),
        _resident_spec(w_o.shape), _resident_spec(norm2.shape), _resident_spec(w1.shape),
        _resident_spec(w2.shape), _resident_spec(final_norm.shape),
    ]
    return pl.pallas_call(
        functools.partial(_post_kernel, tm=tm, sub=sub, ff_chunk=1024),
        grid=(n_tok // tm,),
        in_specs=in_specs,
        out_specs=pl.BlockSpec((tm, D_MODEL), lambda i: (i, 0)),
        out_shape=jax.ShapeDtypeStruct((n_tok, D_MODEL), F32),
        compiler_params=pltpu.CompilerParams(dimension_semantics=("parallel",),
                                             vmem_limit_bytes=VMEM_LIMIT),
        name="post",
    )(x2d, attn2d, mixed2d, mod3, w_o, norm2, w1, w2, final_norm)


def _pack_w_in(w_in):
    qa = w_in[:, :Q_LORA_RANK + KV_LORA_RANK]
    kr = w_in[:, KR_OFF:KR_OFF + QK_ROPE_DIM]
    u = w_in[:, KR_OFF + QK_ROPE_DIM:]
    pad = jnp.zeros((D_MODEL, LANES - QK_ROPE_DIM), w_in.dtype)
    return jnp.concatenate([qa, kr, pad, u], axis=1).astype(BF16)


def _pack_wq(w_qb, with_swapped):
    w = w_qb.reshape(Q_LORA_RANK, N_HEADS, QK_DIM)
    tail = ((0, 0), (0, 0), (0, HEAD_SLAB - QK_DIM))
    plain = jnp.pad(w, tail).reshape(Q_LORA_RANK, QK_WIDTH)
    if not with_swapped:
        return plain.astype(BF16)
    r = w[:, :, QK_NOPE_DIM:].reshape(Q_LORA_RANK, N_HEADS, 2, 2, N_FREQ)
    sw = jnp.stack([-r[:, :, :, 1, :], r[:, :, :, 0, :]], axis=3).reshape(Q_LORA_RANK, N_HEADS, QK_ROPE_DIM)
    sw = jnp.concatenate([jnp.zeros((Q_LORA_RANK, N_HEADS, QK_NOPE_DIM), w.dtype), sw], axis=2)
    swapped = jnp.pad(sw, tail).reshape(Q_LORA_RANK, QK_WIDTH)
    return jnp.concatenate([plain, swapped], axis=1).astype(BF16)


def _pack_wkv(w_kvb):
    w = w_kvb.reshape(KV_LORA_RANK, N_HEADS, QK_NOPE_DIM + V_HEAD_DIM)
    wk_nope = jnp.pad(w[:, :, :QK_NOPE_DIM], ((0, 0), (0, 0), (0, HEAD_SLAB - QK_NOPE_DIM)))
    sel = jnp.pad(jnp.eye(QK_ROPE_DIM, dtype=w_kvb.dtype),
                  ((0, 0), (QK_NOPE_DIM, HEAD_SLAB - QK_DIM)))
    sel = jnp.broadcast_to(sel[:, None, :], (QK_ROPE_DIM, N_HEADS, HEAD_SLAB))
    zero = jnp.zeros((LANES - QK_ROPE_DIM, N_HEADS, HEAD_SLAB), w_kvb.dtype)
    wk = jnp.concatenate([wk_nope, sel, zero], axis=0).reshape(KV_LORA_RANK + LANES, QK_WIDTH)
    wv = w[:, :, QK_NOPE_DIM:].reshape(KV_LORA_RANK, MLA_WIDTH)
    return wk.astype(BF16), wv.astype(BF16)


def _rope_tables(n_tokens):
    f = np.float32
    rows = n_tokens // GRID_W
    row = np.repeat(np.arange(rows, dtype=f), GRID_W)
    col = np.tile(np.arange(GRID_W, dtype=f), rows)
    inv_freq = (1.0 / (f(ROPE_THETA) ** (np.arange(N_FREQ, dtype=f) * f(2.0) / f(AXIS_ROPE)))).astype(f)
    cos_r, sin_r = np.cos(row[:, None] * inv_freq), np.sin(row[:, None] * inv_freq)
    cos_c, sin_c = np.cos(col[:, None] * inv_freq), np.sin(col[:, None] * inv_freq)
    z = np.zeros_like(cos_r)
    c32 = np.concatenate([cos_r, cos_r, cos_c, cos_c], axis=1)
    s32 = np.concatenate([sin_r, sin_r, sin_c, sin_c], axis=1)
    up32 = np.concatenate([z, sin_r, z, sin_c], axis=1)
    dn32 = np.concatenate([-sin_r, z, -sin_c, z], axis=1)

    def place(t32, off, fill):
        left = np.full((n_tokens, off), fill, f)
        right = np.zeros((n_tokens, LANES - off - QK_ROPE_DIM), f)
        return np.concatenate([left, t32.astype(f), right], axis=1)

    q_tabs = (place(c32 * f(Q_SCALE), QK_NOPE_DIM, Q_SCALE), place(s32 * f(Q_SCALE), QK_NOPE_DIM, 0.0))
    k_tabs = (place(c32, 0, 0.0), place(up32, 0, 0.0), place(dn32, 0, 0.0))
    return q_tabs + k_tabs


def kernel(x_prompt, x_sample, cache_ckv, cache_krope, c, c_ctx, w_ada, b_ada, norm1, w_in, q_norm,
           w_qb, kv_norm, w_kvb, w_pool, pool_scale, w_o, norm2, w1, w2, final_norm):
    batch, seq, _ = x_prompt.shape
    dec_batch, dec_seq, _ = x_sample.shape
    depth = w_in.shape[0]
    past = cache_ckv.shape[2]
    assert depth == 1, "single trunk layer"
    assert dec_batch + 1 <= MOD_ROWS

    c_all = jnp.concatenate(
        [c, c_ctx[None, :], jnp.zeros((MOD_ROWS - dec_batch - 1, D_MODEL), F32)], axis=0)
    mod3 = _mod_call(c_all, w_ada[0], b_ada[0][None, :]).reshape(MOD_ROWS, 6, D_MODEL)

    wk, wv = _pack_wkv(w_kvb[0])

    def s1_weights(with_swapped):
        return (norm1[0][None, :], _pack_w_in(w_in[0]), q_norm[0][None, :],
                _pack_wq(w_qb[0], with_swapped), kv_norm[0][None, :], wk, wv,
                w_pool[0].astype(BF16), pool_scale[0][None, :])

    xc = x_prompt.reshape(batch * seq, D_MODEL)
    ctx_row = lambda i: dec_batch
    attn_c, mc, ckv_new, kr_new = _stage1_call(xc, mod3, ctx_row, s1_weights(False), None, seq,
                                               S1_TILE, seq, True, True)

    xl = x_sample.reshape(dec_batch * dec_seq, D_MODEL)
    lat_row = lambda i: i // (dec_seq // S1_TILE)
    ql, kl, vl, ml = _stage1_call(xl, mod3, lat_row, s1_weights(True), _rope_tables(dec_seq), dec_seq,
                                  S1_TILE, S1_SUB, False, False)
    attn_l, w1_16, w2_16 = _latent_attn_call(
        ql.reshape(dec_batch, dec_seq, QK_WIDTH), kl.reshape(dec_batch, dec_seq, QK_WIDTH),
        vl.reshape(dec_batch, dec_seq, MLA_WIDTH),
        cache_ckv.reshape(dec_batch, past, KV_LORA_RANK), cache_krope.reshape(dec_batch, past, QK_ROPE_DIM),
        wk, wv, LAT_Q_TILE, casts=(w1[0], w2[0]))

    post_weights = (w_o[0].astype(BF16), norm2[0][None, :], w1_16, w2_16, final_norm[None, :])
    y_prompt = _post_call(xc, attn_c, mc, mod3, ctx_row,
                          post_weights, POST_TILE, POST_SUB)
    post_row = lambda i: i // (dec_seq // POST_TILE)
    y_sample = _post_call(xl, attn_l.reshape(dec_batch * dec_seq, MLA_WIDTH), ml, mod3, post_row,
                          post_weights, POST_TILE, POST_SUB)

    return (y_prompt.reshape(batch, seq, D_MODEL), y_sample.reshape(dec_batch, dec_seq, D_MODEL),
            ckv_new.reshape(batch, 1, seq, KV_LORA_RANK), kr_new.reshape(batch, 1, seq, QK_ROPE_DIM))
```

```python
import functools
import math

import jax
import jax.numpy as jnp
import numpy as np
from jax import lax
from jax.experimental import pallas as pl
from jax.experimental.pallas import tpu as pltpu

D_MODEL = 1024
N_HEADS = 8
QK_NOPE_DIM = 64
QK_ROPE_DIM = 32
V_HEAD_DIM = 64
QK_DIM = QK_NOPE_DIM + QK_ROPE_DIM
Q_LORA_RANK = 256
KV_LORA_RANK = 128
MLA_WIDTH = N_HEADS * V_HEAD_DIM
POOL_WIDTH = D_MODEL - MLA_WIDTH
POOL_WINDOWS = (2, 4, 8, 16)
POOL_GROUP = POOL_WIDTH // len(POOL_WINDOWS)
D_FF = 4 * D_MODEL
GRID_W = 64
ROPE_THETA = 10000.0
AXIS_ROPE = QK_ROPE_DIM // 2
N_FREQ = AXIS_ROPE // 2
EPS = 1e-6
ATTN_SCALE = 1.0 / math.sqrt(QK_DIM)
Q_SCALE = ATTN_SCALE * math.log2(math.e)

LANES = 128
SUBLANES = 8
HEAD_SLAB = LANES
QK_WIDTH = N_HEADS * HEAD_SLAB
MOD_ROWS = 16
POOL_HALO = max(POOL_WINDOWS) // 2
assert POOL_HALO == SUBLANES
KR_OFF = Q_LORA_RANK + KV_LORA_RANK
U_OFF = KR_OFF + LANES
PROJ_COLS = U_OFF + POOL_WIDTH
VMEM_LIMIT = 56 * 1024 * 1024
S1_TILE = 1024
S1_SUB = 256
LAT_Q_TILE = 512
POST_TILE = 1024
POST_SUB = 256

F32 = jnp.float32
BF16 = jnp.bfloat16


def _dot(a, b):
    return jnp.dot(a, b, preferred_element_type=F32)


def _rms(x, g):
    return x * lax.rsqrt(jnp.mean(x * x, axis=-1, keepdims=True) + EPS) * g


def _round_robin(phased):
    active = list(phased)
    while active:
        for gen in list(active):
            if next(gen, "done") == "done":
                active.remove(gen)


def _const_spec(shape):
    nd = len(shape)
    return pl.BlockSpec(shape, lambda i: (0,) * nd)


def _resident_spec(shape):
    nd = len(shape)
    return pl.BlockSpec(shape, lambda i: (0,) * nd, pipeline_mode=pl.Buffered(1))


def _mod_kernel(c_ref, w_ref, b_ref, o_ref):
    c = c_ref[...]
    s = (c * jax.nn.sigmoid(c)).astype(BF16)
    o_ref[...] = _dot(s, w_ref[...].astype(BF16)) + b_ref[...]


def _mod_call(c_all, w_ada, b_ada):
    tn = 1024
    n = w_ada.shape[1]
    return pl.pallas_call(
        _mod_kernel,
        grid=(n // tn,),
        in_specs=[
            pl.BlockSpec((MOD_ROWS, D_MODEL), lambda j: (0, 0)),
            pl.BlockSpec((D_MODEL, tn), lambda j: (0, j)),
            pl.BlockSpec((1, tn), lambda j: (0, j)),
        ],
        out_specs=pl.BlockSpec((MOD_ROWS, tn), lambda j: (0, j)),
        out_shape=jax.ShapeDtypeStruct((MOD_ROWS, n), F32),
        compiler_params=pltpu.CompilerParams(dimension_semantics=("arbitrary",)),
        name="mod",
    )(c_all, w_ada, b_ada)


def _rope_slab(slab, c, s_up, s_dn):
    return (slab * c + pltpu.roll(slab, N_FREQ, 1) * s_up
            + pltpu.roll(slab, LANES - N_FREQ, 1) * s_dn)


def _inv_counts(w, tm, at_seq_start, at_seq_end):
    half = w // 2
    r = lax.broadcasted_iota(jnp.int32, (SUBLANES, LANES), 0)
    first = jnp.where(at_seq_start, (r + half) - jnp.maximum(r - half, 0), w)
    last = jnp.where(at_seq_end, jnp.minimum(half, SUBLANES - r) + half, w)
    mid = jnp.full((tm - 2 * SUBLANES, LANES), 1.0 / w, F32)
    return jnp.concatenate([1.0 / first.astype(F32), mid, 1.0 / last.astype(F32)], axis=0)


def _pool_windows(e_ref, c2_ref, c4_ref, tm, at_seq_start, at_seq_end):
    rows = tm + 2 * POOL_HALO
    g = [slice(i * POOL_GROUP, (i + 1) * POOL_GROUP) for i in range(len(POOL_WINDOWS))]
    wide = slice(POOL_GROUP, POOL_WIDTH)
    wider = slice(2 * POOL_GROUP, POOL_WIDTH)
    zeros = jnp.zeros((SUBLANES, POOL_WIDTH), F32)
    c2_ref[rows:, :] = zeros
    c4_ref[rows:, :] = zeros
    c2_ref[0:rows, wide] = e_ref[0:rows, wide] + e_ref[1:rows + 1, wide]
    c4_ref[0:rows, wider] = c2_ref[0:rows, wider] + c2_ref[2:rows + 2, wider]
    c8 = c4_ref[0:rows, g[3]] + c4_ref[4:rows + 4, g[3]]
    sums = [
        e_ref[7:7 + tm, g[0]] + e_ref[8:8 + tm, g[0]],
        c2_ref[6:6 + tm, g[1]] + c2_ref[8:8 + tm, g[1]],
        c4_ref[4:4 + tm, g[2]] + c4_ref[8:8 + tm, g[2]],
        c8[0:tm] + c8[8:8 + tm],
    ]
    return [(sums[i] * _inv_counts(w, tm, at_seq_start, at_seq_end)
             - e_ref[POOL_HALO:POOL_HALO + tm, g[i]]).astype(BF16)
            for i, w in enumerate(POOL_WINDOWS)]


def _stage1_kernel(*refs, use_rope, halo, emit_cache, self_attend, tm, sub, seq):
    it = iter(refs)
    x_ref = next(it)
    if halo:
        xp_ref, xn_ref = next(it), next(it)
    (mod_ref, norm1_ref, w_in_ref, qn_ref, wq_ref, kvn_ref, wk_ref, wv_ref, wp_ref,
     ps_ref) = (next(it) for _ in range(10))
    if use_rope:
        q_tabs = [next(it) for _ in range(2)]
        k_tabs = [next(it) for _ in range(3)]
    if self_attend:
        a_ref, m_ref = next(it), next(it)
    else:
        q_ref, k_ref, v_ref, m_ref = (next(it) for _ in range(4))
    if emit_cache:
        ckv_ref, kr_ref = next(it), next(it)
    e_ref, c2_ref, c4_ref = next(it), next(it), next(it)

    n_sub = tm // sub

    def sub_tile(j):
        r0 = j * sub
        rows = slice(r0, r0 + sub)
        pos0 = (pl.program_id(0) * tm + r0) % seq
        at_seq_start = pos0 == 0
        at_seq_end = pos0 + sub == seq
        e, c2, c4 = e_ref.at[j], c2_ref.at[j], c4_ref.at[j]

        if halo:
            before = xp_ref[...] if j == 0 else x_ref[r0 - POOL_HALO:r0, :]
            after = xn_ref[...] if j == n_sub - 1 else x_ref[r0 + sub:r0 + sub + POOL_HALO, :]
            x = jnp.concatenate([before, x_ref[rows, :], after], axis=0)
        else:
            x = x_ref[rows, :]
        gain = norm1_ref[...] * (1.0 + mod_ref[0, 1:2, :])
        h = (_rms(x, gain) + mod_ref[0, 0:1, :]).astype(BF16)
        yield
        u = _dot(h, w_in_ref[:, U_OFF:])
        proj = _dot(h, w_in_ref[:, :U_OFF])
        yield
        zeros = jnp.zeros((POOL_HALO, POOL_WIDTH), F32)
        if halo:
            e[0:POOL_HALO, :] = jnp.where(at_seq_start, 0.0, u[0:POOL_HALO])
            e[POOL_HALO:POOL_HALO + sub, :] = u[POOL_HALO:POOL_HALO + sub]
            e[POOL_HALO + sub:2 * POOL_HALO + sub, :] = jnp.where(at_seq_end, 0.0, u[POOL_HALO + sub:])
            proj = proj[POOL_HALO:POOL_HALO + sub]
        else:
            e[0:POOL_HALO, :] = zeros
            e[POOL_HALO:POOL_HALO + sub, :] = u
            e[POOL_HALO + sub:2 * POOL_HALO + sub, :] = zeros
        e[2 * POOL_HALO + sub:, :] = zeros
        pooled = _pool_windows(e, c2, c4, sub, at_seq_start, at_seq_end)
        qa = _rms(proj[:, :Q_LORA_RANK], qn_ref[...]).astype(BF16)
        if use_rope:
            pos = pl.multiple_of(pos0, sub)
        ckv = _rms(proj[:, Q_LORA_RANK:KR_OFF], kvn_ref[...])
        kr = proj[:, KR_OFF:U_OFF]
        if emit_cache:
            ckv_ref[rows, :] = ckv
            kr_ref[rows, :] = kr[:, :QK_ROPE_DIM]
        if use_rope:
            kr = _rope_slab(kr, *(r[pl.ds(pos, sub), :] for r in k_tabs))
        ckv16 = ckv.astype(BF16)
        kin = jnp.concatenate([ckv16, kr.astype(BF16)], axis=1)
        yield
        q = _dot(qa, wq_ref[...])
        keys = _dot(kin, wk_ref[...]).astype(BF16)
        vals = _dot(ckv16, wv_ref[...]).astype(BF16)
        yield
        if use_rope:
            cq, sq = (r[pl.ds(pos, sub), :] for r in q_tabs)
        q_heads = []
        for hd in range(N_HEADS):
            slab = q[:, hd * HEAD_SLAB:(hd + 1) * HEAD_SLAB]
            if use_rope:
                swapped = q[:, QK_WIDTH + hd * HEAD_SLAB:QK_WIDTH + (hd + 1) * HEAD_SLAB]
                slab = slab * cq + swapped * sq
            else:
                slab = slab * Q_SCALE
            if self_attend:
                q_heads.append(slab.astype(BF16))
            else:
                q_ref[rows, hd * HEAD_SLAB:(hd + 1) * HEAD_SLAB] = slab.astype(BF16)
        yield
        for g in range(len(POOL_WINDOWS)):
            cols = slice(g * POOL_GROUP, (g + 1) * POOL_GROUP)
            m_ref[rows, cols] = (_dot(pooled[g], wp_ref[g]) * ps_ref[:, cols]).astype(BF16)
        if not self_attend:
            k_ref[rows, :] = keys
            v_ref[rows, :] = vals
            return
        yield

        def write(unit, out):
            a_ref[rows, (unit[0] - 1) * V_HEAD_DIM:(unit[0] + 1) * V_HEAD_DIM] = out

        _attention(
            [(hd,) for hd in range(N_HEADS)],
            q_of=lambda u: q_heads[u[0]],
            keys_of=[lambda u: keys[:, u[0] * HEAD_SLAB:(u[0] + 1) * HEAD_SLAB]],
            value_pairs_of=[lambda u: vals[:, (u[0] // 2) * HEAD_SLAB:(u[0] // 2 + 1) * HEAD_SLAB]],
            write_pair=write)

    _round_robin([sub_tile(j) for j in range(n_sub)])


def _stage1_call(x2d, mod3, mod_row_fn, weights, rope_tabs, seq, tm, sub, emit_cache, self_attend):
    n_tok = x2d.shape[0]
    assert not self_attend or sub == seq
    assert tm % sub == 0 and seq % sub == 0 and sub >= 2 * SUBLANES
    assert seq % tm == 0 or tm % seq == 0
    use_rope = rope_tabs is not None
    halo = seq > sub
    assert halo or sub == seq
    halo_blocks = n_tok // POOL_HALO
    per_tile = tm // POOL_HALO
    in_specs = [pl.BlockSpec((tm, D_MODEL), lambda i: (i, 0))]
    args = [x2d]
    if halo:
        in_specs += [
            pl.BlockSpec((POOL_HALO, D_MODEL), lambda i: (jnp.maximum(i * per_tile - 1, 0), 0)),
            pl.BlockSpec((POOL_HALO, D_MODEL),
                         lambda i: (jnp.minimum((i + 1) * per_tile, halo_blocks - 1), 0)),
        ]
        args += [x2d, x2d]
    in_specs += [pl.BlockSpec((1, 6, D_MODEL), lambda i: (mod_row_fn(i), 0, 0))]
    in_specs += [_const_spec(w.shape) for w in weights]
    args += [mod3] + list(weights)
    if use_rope:
        in_specs += [_resident_spec((seq, LANES))] * len(rope_tabs)
        args += list(rope_tabs)
    widths = [MLA_WIDTH] if self_attend else [QK_WIDTH, QK_WIDTH, MLA_WIDTH]
    widths.append(POOL_WIDTH)
    out_specs = [pl.BlockSpec((tm, w), lambda i: (i, 0)) for w in widths]
    out_shape = [jax.ShapeDtypeStruct((n_tok, w), BF16) for w in widths]
    if emit_cache:
        out_specs += [pl.BlockSpec((tm, KV_LORA_RANK), lambda i: (i, 0)),
                      pl.BlockSpec((tm, QK_ROPE_DIM), lambda i: (i, 0))]
        out_shape += [jax.ShapeDtypeStruct((n_tok, KV_LORA_RANK), F32),
                      jax.ShapeDtypeStruct((n_tok, QK_ROPE_DIM), F32)]
    stencil_rows = sub + 3 * POOL_HALO
    return pl.pallas_call(
        functools.partial(_stage1_kernel, use_rope=use_rope, halo=halo, emit_cache=emit_cache,
                          self_attend=self_attend, tm=tm, sub=sub, seq=seq),
        grid=(n_tok // tm,),
        in_specs=in_specs,
        out_specs=out_specs,
        out_shape=out_shape,
        scratch_shapes=[pltpu.VMEM((tm // sub, stencil_rows, POOL_WIDTH), F32)] * 3,
        compiler_params=pltpu.CompilerParams(dimension_semantics=("parallel",),
                                             vmem_limit_bytes=VMEM_LIMIT),
        name="stage1_rope" if use_rope else "stage1_ctx",
    )(*args)


def _qk(qh, kh):
    return lax.dot_general(qh, kh, (((1,), (1,)), ((), ())), preferred_element_type=F32)


def _attention(units, q_of, keys_of, value_pairs_of, write_pair):
    def scores(unit):
        qh = q_of(unit)
        return [_qk(qh, k(unit)) for k in keys_of]

    def values(get, unit):
        vp = get(unit)
        own = (lax.broadcasted_iota(jnp.int32, vp.shape, 1) < V_HEAD_DIM) == (unit[-1] % 2 == 0)
        return jnp.where(own, vp, jnp.ones_like(vp))

    nxt = scores(units[0])
    for idx, unit in enumerate(units):
        blocks = nxt
        if idx + 1 < len(units):
            nxt = scores(units[idx + 1])
        m = functools.reduce(jnp.maximum, [jnp.max(s, axis=-1, keepdims=True) for s in blocks])
        o = functools.reduce(jnp.add, [_dot(jnp.exp2(s - m).astype(BF16), values(v, unit))
                                       for s, v in zip(blocks, value_pairs_of)])
        if unit[-1] % 2 == 0:
            o_even = o
        else:
            low = lax.broadcasted_iota(jnp.int32, o.shape, 1) < V_HEAD_DIM
            num = jnp.where(low, o_even, o)
            den = pltpu.roll(jnp.where(low, o, o_even), V_HEAD_DIM, 1)
            write_pair(unit, (num * (1.0 / den)).astype(BF16))


def _latent_attn_kernel(*refs, n_casts):
    q_ref, k_ref, v_ref, cckv_ref, ckr_ref, wk_ref, wv_ref = refs[:7]
    o_ref = refs[7 + n_casts]
    kc_ref, vc_ref = refs[-2:]
    for src, dst in zip(refs[7:7 + n_casts], refs[8 + n_casts:-2]):
        dst[...] = src[...].astype(BF16)

    @pl.when(pl.program_id(1) == 0)
    def _():
        ckv16 = cckv_ref[0].astype(BF16)
        kr = ckr_ref[0].astype(BF16)
        pad = jnp.zeros((kr.shape[0], LANES - QK_ROPE_DIM), BF16)
        kc_ref[...] = _dot(jnp.concatenate([ckv16, kr, pad], axis=1), wk_ref[...]).astype(BF16)
        vc_ref[...] = _dot(ckv16, wv_ref[...]).astype(BF16)

    slab = lambda hd: slice(hd * HEAD_SLAB, (hd + 1) * HEAD_SLAB)
    pair = lambda hd: slice((hd // 2) * HEAD_SLAB, (hd // 2 + 1) * HEAD_SLAB)

    def write(unit, out):
        o_ref[0, :, (unit[0] - 1) * V_HEAD_DIM:(unit[0] + 1) * V_HEAD_DIM] = out

    _attention([(hd,) for hd in range(N_HEADS)],
               q_of=lambda u: q_ref[0, :, slab(u[0])],
               keys_of=[lambda u: k_ref[0, :, slab(u[0])], lambda u: kc_ref[:, slab(u[0])]],
               value_pairs_of=[lambda u: v_ref[0, :, pair(u[0])], lambda u: vc_ref[:, pair(u[0])]],
               write_pair=write)


def _latent_attn_call(q3, k3, v3, cache_ckv3, cache_kr3, wk, wv, tq, casts):
    b, s, _ = q3.shape
    nk, nc = k3.shape[1], cache_ckv3.shape[1]
    steps_i = s // tq
    n_steps = b * steps_i
    in_specs = [
        pl.BlockSpec((1, tq, QK_WIDTH), lambda bi, i: (bi, i, 0)),
        pl.BlockSpec((1, nk, QK_WIDTH), lambda bi, i: (bi, 0, 0)),
        pl.BlockSpec((1, nk, MLA_WIDTH), lambda bi, i: (bi, 0, 0)),
        pl.BlockSpec((1, nc, KV_LORA_RANK), lambda bi, i: (bi, 0, 0)),
        pl.BlockSpec((1, nc, QK_ROPE_DIM), lambda bi, i: (bi, 0, 0)),
        pl.BlockSpec(wk.shape, lambda bi, i: (0, 0)),
        pl.BlockSpec(wv.shape, lambda bi, i: (0, 0)),
    ]
    args = [q3, k3, v3, cache_ckv3, cache_kr3, wk, wv]
    out_specs = [pl.BlockSpec((1, tq, MLA_WIDTH), lambda bi, i: (bi, i, 0))]
    out_shape = [jax.ShapeDtypeStruct((b, s, MLA_WIDTH), BF16)]
    for w in casts:
        assert w.shape[0] % (n_steps * 2 * SUBLANES) == 0
        blk = (w.shape[0] // n_steps, w.shape[1])
        spec = pl.BlockSpec(blk, lambda bi, i: (bi * steps_i + i, 0))
        in_specs.append(spec)
        out_specs.append(spec)
        args.append(w)
        out_shape.append(jax.ShapeDtypeStruct(w.shape, BF16))
    return pl.pallas_call(
        functools.partial(_latent_attn_kernel, n_casts=len(casts)),
        grid=(b, steps_i),
        in_specs=in_specs,
        out_specs=out_specs,
        out_shape=out_shape,
        scratch_shapes=[pltpu.VMEM((nc, QK_WIDTH), BF16), pltpu.VMEM((nc, MLA_WIDTH), BF16)],
        compiler_params=pltpu.CompilerParams(
            dimension_semantics=("parallel", "arbitrary"),
            vmem_limit_bytes=VMEM_LIMIT),
        name="attn_lat",
    )(*args)


def _post_kernel(x_ref, a_ref, m_ref, mod_ref, wo_ref, norm2_ref, w1_ref, w2_ref, fn_ref, o_ref,
                 *, tm, sub, ff_chunk):
    def sub_tile(j):
        rows = slice(j * sub, (j + 1) * sub)
        cat = jnp.concatenate([a_ref[rows, :], m_ref[rows, :]], axis=1)
        y = x_ref[rows, :] + mod_ref[0, 2:3, :] * _dot(cat, wo_ref[...])
        gain = norm2_ref[...] * (1.0 + mod_ref[0, 4:5, :])
        h2 = (_rms(y, gain) + mod_ref[0, 3:4, :]).astype(BF16)
        yield
        acc = jnp.zeros((sub, D_MODEL), F32)
        for c in range(D_FF // ff_chunk):
            a = _dot(h2, w1_ref[:, c * ff_chunk:(c + 1) * ff_chunk])
            a = jnp.square(jnp.maximum(a, 0.0)).astype(BF16)
            acc = acc + _dot(a, w2_ref[c * ff_chunk:(c + 1) * ff_chunk, :])
            yield
        x2 = y + mod_ref[0, 5:6, :] * acc
        o_ref[rows, :] = _rms(x2, fn_ref[...])

    _round_robin([sub_tile(j) for j in range(tm // sub)])


def _post_call(x2d, attn2d, mixed2d, mod3, mod_row_fn, weights, tm, sub):
    n_tok = x2d.shape[0]
    assert tm % sub == 0
    w_o, norm2, w1, w2, final_norm = weights
    in_specs = [
        pl.BlockSpec((tm, D_MODEL), lambda i: (i, 0)),
        pl.BlockSpec((tm, MLA_WIDTH), lambda i: (i, 0)),
        pl.BlockSpec((tm, POOL_WIDTH), lambda i: (i, 0)),
        pl.BlockSpec((1, 6, D_MODEL), lambda i: (mod_row_fn(i), 0, 0)),
        _resident_spec(w_o.shape), _resident_spec(norm2.shape), _resident_spec(w1.shape),
        _resident_spec(w2.shape), _resident_spec(final_norm.shape),
    ]
    return pl.pallas_call(
        functools.partial(_post_kernel, tm=tm, sub=sub, ff_chunk=1024),
        grid=(n_tok // tm,),
        in_specs=in_specs,
        out_specs=pl.BlockSpec((tm, D_MODEL), lambda i: (i, 0)),
        out_shape=jax.ShapeDtypeStruct((n_tok, D_MODEL), F32),
        compiler_params=pltpu.CompilerParams(dimension_semantics=("parallel",),
                                             vmem_limit_bytes=VMEM_LIMIT),
        name="post",
    )(x2d, attn2d, mixed2d, mod3, w_o, norm2, w1, w2, final_norm)


def _pack_w_in(w_in):
    qa = w_in[:, :Q_LORA_RANK + KV_LORA_RANK]
    kr = w_in[:, KR_OFF:KR_OFF + QK_ROPE_DIM]
    u = w_in[:, KR_OFF + QK_ROPE_DIM:]
    pad = jnp.zeros((D_MODEL, LANES - QK_ROPE_DIM), w_in.dtype)
    return jnp.concatenate([qa, kr, pad, u], axis=1).astype(BF16)


def _pack_wq(w_qb, with_swapped):
    w = w_qb.reshape(Q_LORA_RANK, N_HEADS, QK_DIM)
    tail = ((0, 0), (0, 0), (0, HEAD_SLAB - QK_DIM))
    plain = jnp.pad(w, tail).reshape(Q_LORA_RANK, QK_WIDTH)
    if not with_swapped:
        return plain.astype(BF16)
    r = w[:, :, QK_NOPE_DIM:].reshape(Q_LORA_RANK, N_HEADS, 2, 2, N_FREQ)
    sw = jnp.stack([-r[:, :, :, 1, :], r[:, :, :, 0, :]], axis=3).reshape(Q_LORA_RANK, N_HEADS, QK_ROPE_DIM)
    sw = jnp.concatenate([jnp.zeros((Q_LORA_RANK, N_HEADS, QK_NOPE_DIM), w.dtype), sw], axis=2)
    swapped = jnp.pad(sw, tail).reshape(Q_LORA_RANK, QK_WIDTH)
    return jnp.concatenate([plain, swapped], axis=1).astype(BF16)


def _pack_wkv(w_kvb):
    w = w_kvb.reshape(KV_LORA_RANK, N_HEADS, QK_NOPE_DIM + V_HEAD_DIM)
    wk_nope = jnp.pad(w[:, :, :QK_NOPE_DIM], ((0, 0), (0, 0), (0, HEAD_SLAB - QK_NOPE_DIM)))
    sel = jnp.pad(jnp.eye(QK_ROPE_DIM, dtype=w_kvb.dtype),
                  ((0, 0), (QK_NOPE_DIM, HEAD_SLAB - QK_DIM)))
    sel = jnp.broadcast_to(sel[:, None, :], (QK_ROPE_DIM, N_HEADS, HEAD_SLAB))
    zero = jnp.zeros((LANES - QK_ROPE_DIM, N_HEADS, HEAD_SLAB), w_kvb.dtype)
    wk = jnp.concatenate([wk_nope, sel, zero], axis=0).reshape(KV_LORA_RANK + LANES, QK_WIDTH)
    wv = w[:, :, QK_NOPE_DIM:].reshape(KV_LORA_RANK, MLA_WIDTH)
    return wk.astype(BF16), wv.astype(BF16)


def _rope_tables(n_tokens):
    f = np.float32
    rows = n_tokens // GRID_W
    row = np.repeat(np.arange(rows, dtype=f), GRID_W)
    col = np.tile(np.arange(GRID_W, dtype=f), rows)
    inv_freq = (1.0 / (f(ROPE_THETA) ** (np.arange(N_FREQ, dtype=f) * f(2.0) / f(AXIS_ROPE)))).astype(f)
    cos_r, sin_r = np.cos(row[:, None] * inv_freq), np.sin(row[:, None] * inv_freq)
    cos_c, sin_c = np.cos(col[:, None] * inv_freq), np.sin(col[:, None] * inv_freq)
    z = np.zeros_like(cos_r)
    c32 = np.concatenate([cos_r, cos_r, cos_c, cos_c], axis=1)
    s32 = np.concatenate([sin_r, sin_r, sin_c, sin_c], axis=1)
    up32 = np.concatenate([z, sin_r, z, sin_c], axis=1)
    dn32 = np.concatenate([-sin_r, z, -sin_c, z], axis=1)

    def place(t32, off, fill):
        left = np.full((n_tokens, off), fill, f)
        right = np.zeros((n_tokens, LANES - off - QK_ROPE_DIM), f)
        return np.concatenate([left, t32.astype(f), right], axis=1)

    q_tabs = (place(c32 * f(Q_SCALE), QK_NOPE_DIM, Q_SCALE), place(s32 * f(Q_SCALE), QK_NOPE_DIM, 0.0))
    k_tabs = (place(c32, 0, 0.0), place(up32, 0, 0.0), place(dn32, 0, 0.0))
    return q_tabs + k_tabs


def kernel(x_prompt, x_sample, cache_ckv, cache_krope, c, c_ctx, w_ada, b_ada, norm1, w_in, q_norm,
           w_qb, kv_norm, w_kvb, w_pool, pool_scale, w_o, norm2, w1, w2, final_norm):
    batch, seq, _ = x_prompt.shape
    dec_batch, dec_seq, _ = x_sample.shape
    depth = w_in.shape[0]
    past = cache_ckv.shape[2]
    assert depth == 1, "single trunk layer"
    assert dec_batch + 1 <= MOD_ROWS

    c_all = jnp.concatenate(
        [c, c_ctx[None, :], jnp.zeros((MOD_ROWS - dec_batch - 1, D_MODEL), F32)], axis=0)
    mod3 = _mod_call(c_all, w_ada[0], b_ada[0][None, :]).reshape(MOD_ROWS, 6, D_MODEL)

    wk, wv = _pack_wkv(w_kvb[0])

    def s1_weights(with_swapped):
        return (norm1[0][None, :], _pack_w_in(w_in[0]), q_norm[0][None, :],
                _pack_wq(w_qb[0], with_swapped), kv_norm[0][None, :], wk, wv,
                w_pool[0].astype(BF16), pool_scale[0][None, :])

    xc = x_prompt.reshape(batch * seq, D_MODEL)
    ctx_row = lambda i: dec_batch
    attn_c, mc, ckv_new, kr_new = _stage1_call(xc, mod3, ctx_row, s1_weights(False), None, seq,
                                               S1_TILE, seq, True, True)

    xl = x_sample.reshape(dec_batch * dec_seq, D_MODEL)
    lat_row = lambda i: i // (dec_seq // S1_TILE)
    ql, kl, vl, ml = _stage1_call(xl, mod3, lat_row, s1_weights(True), _rope_tables(dec_seq), dec_seq,
                                  S1_TILE, S1_SUB, False, False)
    attn_l, w1_16, w2_16 = _latent_attn_call(
        ql.reshape(dec_batch, dec_seq, QK_WIDTH), kl.reshape(dec_batch, dec_seq, QK_WIDTH),
        vl.reshape(dec_batch, dec_seq, MLA_WIDTH),
        cache_ckv.reshape(dec_batch, past, KV_LORA_RANK), cache_krope.reshape(dec_batch, past, QK_ROPE_DIM),
        wk, wv, LAT_Q_TILE, casts=(w1[0], w2[0]))

    post_weights = (w_o[0].astype(BF16), norm2[0][None, :], w1_16, w2_16, final_norm[None, :])
    y_prompt = _post_call(xc, attn_c, mc, mod3, ctx_row,
                          post_weights, POST_TILE, POST_SUB)
    post_row = lambda i: i // (dec_seq // POST_TILE)
    y_sample = _post_call(xl, attn_l.reshape(dec_batch * dec_seq, MLA_WIDTH), ml, mod3, post_row,
                          post_weights, POST_TILE, POST_SUB)

    return (y_prompt.reshape(batch, seq, D_MODEL), y_sample.reshape(dec_batch, dec_seq, D_MODEL),
            ckv_new.reshape(batch, 1, seq, KV_LORA_RANK), kr_new.reshape(batch, 1, seq, QK_ROPE_DIM))
```

```python
import functools
import math

import jax
import jax.numpy as jnp
import numpy as np
from jax import lax
from jax.experimental import pallas as pl
from jax.experimental.pallas import tpu as pltpu

D_MODEL = 1024
N_HEADS = 8
QK_NOPE_DIM = 64
QK_ROPE_DIM = 32
V_HEAD_DIM = 64
QK_DIM = QK_NOPE_DIM + QK_ROPE_DIM
Q_LORA_RANK = 256
KV_LORA_RANK = 128
MLA_WIDTH = N_HEADS * V_HEAD_DIM
POOL_WIDTH = D_MODEL - MLA_WIDTH
POOL_WINDOWS = (2, 4, 8, 16)
POOL_GROUP = POOL_WIDTH // len(POOL_WINDOWS)
D_FF = 4 * D_MODEL
GRID_W = 64
ROPE_THETA = 10000.0
AXIS_ROPE = QK_ROPE_DIM // 2
N_FREQ = AXIS_ROPE // 2
EPS = 1e-6
ATTN_SCALE = 1.0 / math.sqrt(QK_DIM)
Q_SCALE = ATTN_SCALE * math.log2(math.e)

LANES = 128
SUBLANES = 8
HEAD_SLAB = LANES
QK_WIDTH = N_HEADS * HEAD_SLAB
MOD_ROWS = 16
POOL_HALO = max(POOL_WINDOWS) // 2
assert POOL_HALO == SUBLANES
KR_OFF = Q_LORA_RANK + KV_LORA_RANK
U_OFF = KR_OFF + LANES
PROJ_COLS = U_OFF + POOL_WIDTH
VMEM_LIMIT = 56 * 1024 * 1024
S1_TILE = 1024
S1_SUB = 256
LAT_Q_TILE = 512
POST_TILE = 1024
POST_SUB = 256

F32 = jnp.float32
BF16 = jnp.bfloat16


def _dot(a, b):
    return jnp.dot(a, b, preferred_element_type=F32)


def _rms(x, g):
    return x * lax.rsqrt(jnp.mean(x * x, axis=-1, keepdims=True) + EPS) * g


def _round_robin(phased):
    active = list(phased)
    while active:
        for gen in list(active):
            if next(gen, "done") == "done":
                active.remove(gen)


def _const_spec(shape):
    nd = len(shape)
    return pl.BlockSpec(shape, lambda i: (0,) * nd)


def _resident_spec(shape):
    nd = len(shape)
    return pl.BlockSpec(shape, lambda i: (0,) * nd, pipeline_mode=pl.Buffered(1))


MOD_CHUNK = 128
MOD_BUFS = 3


def _mod_kernel(c_ref, b_ref, w_hbm, o_ref, buf, sem):
    n_chunks = D_MODEL // MOD_CHUNK

    def chunk_copy(k):
        return pltpu.make_async_copy(w_hbm.at[pl.ds(k * MOD_CHUNK, MOD_CHUNK), :],
                                     buf.at[k % MOD_BUFS], sem.at[k % MOD_BUFS])

    for k in range(MOD_BUFS):
        chunk_copy(k).start()
    c = c_ref[...]
    s = (c * jax.nn.sigmoid(c)).astype(BF16)
    acc = jnp.zeros(o_ref.shape, F32)
    for k in range(n_chunks):
        chunk_copy(k).wait()
        w = buf[k % MOD_BUFS].astype(BF16)
        acc = acc + _dot(s[:, k * MOD_CHUNK:(k + 1) * MOD_CHUNK], w)
        if k + MOD_BUFS < n_chunks:
            chunk_copy(k + MOD_BUFS).start()
    o_ref[...] = acc + b_ref[...]


def _mod_call(c_all, w_ada, b_ada):
    n = w_ada.shape[1]
    assert D_MODEL % MOD_CHUNK == 0 and D_MODEL // MOD_CHUNK >= MOD_BUFS
    vmem = pl.BlockSpec(memory_space=pltpu.VMEM)
    return pl.pallas_call(
        _mod_kernel,
        in_specs=[vmem, vmem, pl.BlockSpec(memory_space=pl.ANY)],
        out_specs=vmem,
        out_shape=jax.ShapeDtypeStruct((MOD_ROWS, n), F32),
        scratch_shapes=[pltpu.VMEM((MOD_BUFS, MOD_CHUNK, n), F32),
                        pltpu.SemaphoreType.DMA((MOD_BUFS,))],
        compiler_params=pltpu.CompilerParams(vmem_limit_bytes=VMEM_LIMIT),
        name="mod",
    )(c_all, b_ada, w_ada)


def _rope_slab(slab, c, s_up, s_dn):
    return (slab * c + pltpu.roll(slab, N_FREQ, 1) * s_up
            + pltpu.roll(slab, LANES - N_FREQ, 1) * s_dn)


def _inv_counts(w, tm, at_seq_start, at_seq_end):
    half = w // 2
    r = lax.broadcasted_iota(jnp.int32, (SUBLANES, LANES), 0)
    first = jnp.where(at_seq_start, (r + half) - jnp.maximum(r - half, 0), w)
    last = jnp.where(at_seq_end, jnp.minimum(half, SUBLANES - r) + half, w)
    mid = jnp.full((tm - 2 * SUBLANES, LANES), 1.0 / w, F32)
    return jnp.concatenate([1.0 / first.astype(F32), mid, 1.0 / last.astype(F32)], axis=0)


def _pool_windows(e_ref, c2_ref, c4_ref, tm, at_seq_start, at_seq_end):
    rows = tm + 2 * POOL_HALO
    g = [slice(i * POOL_GROUP, (i + 1) * POOL_GROUP) for i in range(len(POOL_WINDOWS))]
    wide = slice(POOL_GROUP, POOL_WIDTH)
    wider = slice(2 * POOL_GROUP, POOL_WIDTH)
    zeros = jnp.zeros((SUBLANES, POOL_WIDTH), F32)
    c2_ref[rows:, :] = zeros
    c4_ref[rows:, :] = zeros
    c2_ref[0:rows, wide] = e_ref[0:rows, wide] + e_ref[1:rows + 1, wide]
    c4_ref[0:rows, wider] = c2_ref[0:rows, wider] + c2_ref[2:rows + 2, wider]
    c8 = c4_ref[0:rows, g[3]] + c4_ref[4:rows + 4, g[3]]
    sums = [
        e_ref[7:7 + tm, g[0]] + e_ref[8:8 + tm, g[0]],
        c2_ref[6:6 + tm, g[1]] + c2_ref[8:8 + tm, g[1]],
        c4_ref[4:4 + tm, g[2]] + c4_ref[8:8 + tm, g[2]],
        c8[0:tm] + c8[8:8 + tm],
    ]
    return [(sums[i] * _inv_counts(w, tm, at_seq_start, at_seq_end)
             - e_ref[POOL_HALO:POOL_HALO + tm, g[i]]).astype(BF16)
            for i, w in enumerate(POOL_WINDOWS)]


def _stage1_kernel(*refs, use_rope, halo, emit_cache, self_attend, tm, sub, seq):
    it = iter(refs)
    x_ref = next(it)
    if halo:
        xp_ref, xn_ref = next(it), next(it)
    (mod_ref, norm1_ref, w_in_ref, qn_ref, wq_ref, kvn_ref, wk_ref, wv_ref, wp_ref,
     ps_ref) = (next(it) for _ in range(10))
    if use_rope:
        q_tabs = [next(it) for _ in range(2)]
        k_tabs = [next(it) for _ in range(3)]
    if self_attend:
        a_ref, m_ref = next(it), next(it)
    else:
        q_ref, k_ref, v_ref, m_ref = (next(it) for _ in range(4))
    if emit_cache:
        ckv_ref, kr_ref = next(it), next(it)
    e_ref, c2_ref, c4_ref = next(it), next(it), next(it)

    n_sub = tm // sub

    def sub_tile(j):
        r0 = j * sub
        rows = slice(r0, r0 + sub)
        pos0 = (pl.program_id(0) * tm + r0) % seq
        at_seq_start = pos0 == 0
        at_seq_end = pos0 + sub == seq
        e, c2, c4 = e_ref.at[j], c2_ref.at[j], c4_ref.at[j]

        if halo:
            before = xp_ref[...] if j == 0 else x_ref[r0 - POOL_HALO:r0, :]
            after = xn_ref[...] if j == n_sub - 1 else x_ref[r0 + sub:r0 + sub + POOL_HALO, :]
            x = jnp.concatenate([before, x_ref[rows, :], after], axis=0)
        else:
            x = x_ref[rows, :]
        gain = norm1_ref[...] * (1.0 + mod_ref[0, 1:2, :])
        h = (_rms(x, gain) + mod_ref[0, 0:1, :]).astype(BF16)
        yield
        u = _dot(h, w_in_ref[:, U_OFF:])
        proj = _dot(h, w_in_ref[:, :U_OFF])
        yield
        zeros = jnp.zeros((POOL_HALO, POOL_WIDTH), F32)
        if halo:
            e[0:POOL_HALO, :] = jnp.where(at_seq_start, 0.0, u[0:POOL_HALO])
            e[POOL_HALO:POOL_HALO + sub, :] = u[POOL_HALO:POOL_HALO + sub]
            e[POOL_HALO + sub:2 * POOL_HALO + sub, :] = jnp.where(at_seq_end, 0.0, u[POOL_HALO + sub:])
            proj = proj[POOL_HALO:POOL_HALO + sub]
        else:
            e[0:POOL_HALO, :] = zeros
            e[POOL_HALO:POOL_HALO + sub, :] = u
            e[POOL_HALO + sub:2 * POOL_HALO + sub, :] = zeros
        e[2 * POOL_HALO + sub:, :] = zeros
        pooled = _pool_windows(e, c2, c4, sub, at_seq_start, at_seq_end)
        qa = _rms(proj[:, :Q_LORA_RANK], qn_ref[...]).astype(BF16)
        if use_rope:
            pos = pl.multiple_of(pos0, sub)
        ckv = _rms(proj[:, Q_LORA_RANK:KR_OFF], kvn_ref[...])
        kr = proj[:, KR_OFF:U_OFF]
        if emit_cache:
            ckv_ref[rows, :] = ckv
            kr_ref[rows, :] = kr[:, :QK_ROPE_DIM]
        if use_rope:
            kr = _rope_slab(kr, *(r[pl.ds(pos, sub), :] for r in k_tabs))
        ckv16 = ckv.astype(BF16)
        kin = jnp.concatenate([ckv16, kr.astype(BF16)], axis=1)
        yield
        q = _dot(qa, wq_ref[...])
        keys = _dot(kin, wk_ref[...]).astype(BF16)
        vals = _dot(ckv16, wv_ref[...]).astype(BF16)
        yield
        if use_rope:
            cq, sq = (r[pl.ds(pos, sub), :] for r in q_tabs)
        q_heads = []
        for hd in range(N_HEADS):
            slab = q[:, hd * HEAD_SLAB:(hd + 1) * HEAD_SLAB]
            if use_rope:
                swapped = q[:, QK_WIDTH + hd * HEAD_SLAB:QK_WIDTH + (hd + 1) * HEAD_SLAB]
                slab = slab * cq + swapped * sq
            else:
                slab = slab * Q_SCALE
            if self_attend:
                q_heads.append(slab.astype(BF16))
            else:
                q_ref[rows, hd * HEAD_SLAB:(hd + 1) * HEAD_SLAB] = slab.astype(BF16)
        yield
        for g in range(len(POOL_WINDOWS)):
            cols = slice(g * POOL_GROUP, (g + 1) * POOL_GROUP)
            m_ref[rows, cols] = (_dot(pooled[g], wp_ref[g]) * ps_ref[:, cols]).astype(BF16)
        if not self_attend:
            k_ref[rows, :] = keys
            v_ref[rows, :] = vals
            return
        yield

        def write(unit, out):
            a_ref[rows, (unit[0] - 1) * V_HEAD_DIM:(unit[0] + 1) * V_HEAD_DIM] = out

        _attention(
            [(hd,) for hd in range(N_HEADS)],
            q_of=lambda u: q_heads[u[0]],
            keys_of=[lambda u: keys[:, u[0] * HEAD_SLAB:(u[0] + 1) * HEAD_SLAB]],
            value_pairs_of=[lambda u: vals[:, (u[0] // 2) * HEAD_SLAB:(u[0] // 2 + 1) * HEAD_SLAB]],
            write_pair=write)

    _round_robin([sub_tile(j) for j in range(n_sub)])


def _stage1_call(x2d, mod3, mod_row_fn, weights, rope_tabs, seq, tm, sub, emit_cache, self_attend):
    n_tok = x2d.shape[0]
    assert not self_attend or sub == seq
    assert tm % sub == 0 and seq % sub == 0 and sub >= 2 * SUBLANES
    assert seq % tm == 0 or tm % seq == 0
    use_rope = rope_tabs is not None
    halo = seq > sub
    assert halo or sub == seq
    halo_blocks = n_tok // POOL_HALO
    per_tile = tm // POOL_HALO
    in_specs = [pl.BlockSpec((tm, D_MODEL), lambda i: (i, 0))]
    args = [x2d]
    if halo:
        in_specs += [
            pl.BlockSpec((POOL_HALO, D_MODEL), lambda i: (jnp.maximum(i * per_tile - 1, 0), 0)),
            pl.BlockSpec((POOL_HALO, D_MODEL),
                         lambda i: (jnp.minimum((i + 1) * per_tile, halo_blocks - 1), 0)),
        ]
        args += [x2d, x2d]
    in_specs += [pl.BlockSpec((1, 6, D_MODEL), lambda i: (mod_row_fn(i), 0, 0))]
    in_specs += [_const_spec(w.shape) for w in weights]
    args += [mod3] + list(weights)
    if use_rope:
        in_specs += [_resident_spec((seq, LANES))] * len(rope_tabs)
        args += list(rope_tabs)
    widths = [MLA_WIDTH] if self_attend else [QK_WIDTH, QK_WIDTH, MLA_WIDTH]
    widths.append(POOL_WIDTH)
    out_specs = [pl.BlockSpec((tm, w), lambda i: (i, 0)) for w in widths]
    out_shape = [jax.ShapeDtypeStruct((n_tok, w), BF16) for w in widths]
    if emit_cache:
        out_specs += [pl.BlockSpec((tm, KV_LORA_RANK), lambda i: (i, 0)),
                      pl.BlockSpec((tm, QK_ROPE_DIM), lambda i: (i, 0))]
        out_shape += [jax.ShapeDtypeStruct((n_tok, KV_LORA_RANK), F32),
                      jax.ShapeDtypeStruct((n_tok, QK_ROPE_DIM), F32)]
    stencil_rows = sub + 3 * POOL_HALO
    return pl.pallas_call(
        functools.partial(_stage1_kernel, use_rope=use_rope, halo=halo, emit_cache=emit_cache,
                          self_attend=self_attend, tm=tm, sub=sub, seq=seq),
        grid=(n_tok // tm,),
        in_specs=in_specs,
        out_specs=out_specs,
        out_shape=out_shape,
        scratch_shapes=[pltpu.VMEM((tm // sub, stencil_rows, POOL_WIDTH), F32)] * 3,
        compiler_params=pltpu.CompilerParams(dimension_semantics=("parallel",),
                                             vmem_limit_bytes=VMEM_LIMIT),
        name="stage1_rope" if use_rope else "stage1_ctx",
    )(*args)


def _qk(qh, kh):
    return lax.dot_general(qh, kh, (((1,), (1,)), ((), ())), preferred_element_type=F32)


def _attention(units, q_of, keys_of, value_pairs_of, write_pair):
    def scores(unit):
        qh = q_of(unit)
        return [_qk(qh, k(unit)) for k in keys_of]

    def values(get, unit):
        vp = get(unit)
        own = (lax.broadcasted_iota(jnp.int32, vp.shape, 1) < V_HEAD_DIM) == (unit[-1] % 2 == 0)
        return jnp.where(own, vp, jnp.ones_like(vp))

    nxt = scores(units[0])
    for idx, unit in enumerate(units):
        blocks = nxt
        if idx + 1 < len(units):
            nxt = scores(units[idx + 1])
        m = functools.reduce(jnp.maximum, [jnp.max(s, axis=-1, keepdims=True) for s in blocks])
        o = functools.reduce(jnp.add, [_dot(jnp.exp2(s - m).astype(BF16), values(v, unit))
                                       for s, v in zip(blocks, value_pairs_of)])
        if unit[-1] % 2 == 0:
            o_even = o
        else:
            low = lax.broadcasted_iota(jnp.int32, o.shape, 1) < V_HEAD_DIM
            num = jnp.where(low, o_even, o)
            den = pltpu.roll(jnp.where(low, o, o_even), V_HEAD_DIM, 1)
            write_pair(unit, (num * (1.0 / den)).astype(BF16))


def _latent_attn_kernel(*refs, n_casts):
    q_ref, k_ref, v_ref, cckv_ref, ckr_ref, wk_ref, wv_ref = refs[:7]
    o_ref = refs[7 + n_casts]
    kc_ref, vc_ref = refs[-2:]
    for src, dst in zip(refs[7:7 + n_casts], refs[8 + n_casts:-2]):
        dst[...] = src[...].astype(BF16)

    @pl.when(pl.program_id(1) == 0)
    def _():
        ckv16 = cckv_ref[0].astype(BF16)
        kr = ckr_ref[0].astype(BF16)
        pad = jnp.zeros((kr.shape[0], LANES - QK_ROPE_DIM), BF16)
        kc_ref[...] = _dot(jnp.concatenate([ckv16, kr, pad], axis=1), wk_ref[...]).astype(BF16)
        vc_ref[...] = _dot(ckv16, wv_ref[...]).astype(BF16)

    slab = lambda hd: slice(hd * HEAD_SLAB, (hd + 1) * HEAD_SLAB)
    pair = lambda hd: slice((hd // 2) * HEAD_SLAB, (hd // 2 + 1) * HEAD_SLAB)

    def write(unit, out):
        o_ref[0, :, (unit[0] - 1) * V_HEAD_DIM:(unit[0] + 1) * V_HEAD_DIM] = out

    _attention([(hd,) for hd in range(N_HEADS)],
               q_of=lambda u: q_ref[0, :, slab(u[0])],
               keys_of=[lambda u: k_ref[0, :, slab(u[0])], lambda u: kc_ref[:, slab(u[0])]],
               value_pairs_of=[lambda u: v_ref[0, :, pair(u[0])], lambda u: vc_ref[:, pair(u[0])]],
               write_pair=write)


def _latent_attn_call(q3, k3, v3, cache_ckv3, cache_kr3, wk, wv, tq, casts):
    b, s, _ = q3.shape
    nk, nc = k3.shape[1], cache_ckv3.shape[1]
    steps_i = s // tq
    n_steps = b * steps_i
    in_specs = [
        pl.BlockSpec((1, tq, QK_WIDTH), lambda bi, i: (bi, i, 0)),
        pl.BlockSpec((1, nk, QK_WIDTH), lambda bi, i: (bi, 0, 0)),
        pl.BlockSpec((1, nk, MLA_WIDTH), lambda bi, i: (bi, 0, 0)),
        pl.BlockSpec((1, nc, KV_LORA_RANK), lambda bi, i: (bi, 0, 0)),
        pl.BlockSpec((1, nc, QK_ROPE_DIM), lambda bi, i: (bi, 0, 0)),
        pl.BlockSpec(wk.shape, lambda bi, i: (0, 0)),
        pl.BlockSpec(wv.shape, lambda bi, i: (0, 0)),
    ]
    args = [q3, k3, v3, cache_ckv3, cache_kr3, wk, wv]
    out_specs = [pl.BlockSpec((1, tq, MLA_WIDTH), lambda bi, i: (bi, i, 0))]
    out_shape = [jax.ShapeDtypeStruct((b, s, MLA_WIDTH), BF16)]
    for w in casts:
        assert w.shape[0] % (n_steps * 2 * SUBLANES) == 0
        blk = (w.shape[0] // n_steps, w.shape[1])
        spec = pl.BlockSpec(blk, lambda bi, i: (bi * steps_i + i, 0))
        in_specs.append(spec)
        out_specs.append(spec)
        args.append(w)
        out_shape.append(jax.ShapeDtypeStruct(w.shape, BF16))
    return pl.pallas_call(
        functools.partial(_latent_attn_kernel, n_casts=len(casts)),
        grid=(b, steps_i),
        in_specs=in_specs,
        out_specs=out_specs,
        out_shape=out_shape,
        scratch_shapes=[pltpu.VMEM((nc, QK_WIDTH), BF16), pltpu.VMEM((nc, MLA_WIDTH), BF16)],
        compiler_params=pltpu.CompilerParams(
            dimension_semantics=("parallel", "arbitrary"),
            vmem_limit_bytes=VMEM_LIMIT),
        name="attn_lat",
    )(*args)


def _post_kernel(x_ref, a_ref, m_ref, mod_ref, wo_ref, norm2_ref, w1_ref, w2_ref, fn_ref, o_ref,
                 *, tm, sub, ff_chunk):
    def sub_tile(j):
        rows = slice(j * sub, (j + 1) * sub)
        cat = jnp.concatenate([a_ref[rows, :], m_ref[rows, :]], axis=1)
        y = x_ref[rows, :] + mod_ref[0, 2:3, :] * _dot(cat, wo_ref[...])
        gain = norm2_ref[...] * (1.0 + mod_ref[0, 4:5, :])
        h2 = (_rms(y, gain) + mod_ref[0, 3:4, :]).astype(BF16)
        yield
        acc = jnp.zeros((sub, D_MODEL), F32)
        for c in range(D_FF // ff_chunk):
            a = _dot(h2, w1_ref[:, c * ff_chunk:(c + 1) * ff_chunk])
            a = jnp.square(jnp.maximum(a, 0.0)).astype(BF16)
            acc = acc + _dot(a, w2_ref[c * ff_chunk:(c + 1) * ff_chunk, :])
            yield
        x2 = y + mod_ref[0, 5:6, :] * acc
        o_ref[rows, :] = _rms(x2, fn_ref[...])

    _round_robin([sub_tile(j) for j in range(tm // sub)])


def _post_call(x2d, attn2d, mixed2d, mod3, mod_row_fn, weights, tm, sub):
    n_tok = x2d.shape[0]
    assert tm % sub == 0
    w_o, norm2, w1, w2, final_norm = weights
    in_specs = [
        pl.BlockSpec((tm, D_MODEL), lambda i: (i, 0)),
        pl.BlockSpec((tm, MLA_WIDTH), lambda i: (i, 0)),
        pl.BlockSpec((tm, POOL_WIDTH), lambda i: (i, 0)),
        pl.BlockSpec((1, 6, D_MODEL), lambda i: (mod_row_fn(i), 0, 0)),
        _resident_spec(w_o.shape), _resident_spec(norm2.shape), _resident_spec(w1.shape),
        _resident_spec(w2.shape), _resident_spec(final_norm.shape),
    ]
    return pl.pallas_call(
        functools.partial(_post_kernel, tm=tm, sub=sub, ff_chunk=1024),
        grid=(n_tok // tm,),
        in_specs=in_specs,
        out_specs=pl.BlockSpec((tm, D_MODEL), lambda i: (i, 0)),
        out_shape=jax.ShapeDtypeStruct((n_tok, D_MODEL), F32),
        compiler_params=pltpu.CompilerParams(dimension_semantics=("parallel",),
                                             vmem_limit_bytes=VMEM_LIMIT),
        name="post",
    )(x2d, attn2d, mixed2d, mod3, w_o, norm2, w1, w2, final_norm)


def _pack_w_in(w_in):
    qa = w_in[:, :Q_LORA_RANK + KV_LORA_RANK]
    kr = w_in[:, KR_OFF:KR_OFF + QK_ROPE_DIM]
    u = w_in[:, KR_OFF + QK_ROPE_DIM:]
    pad = jnp.zeros((D_MODEL, LANES - QK_ROPE_DIM), w_in.dtype)
    return jnp.concatenate([qa, kr, pad, u], axis=1).astype(BF16)


def _pack_wq(w_qb, with_swapped):
    w = w_qb.reshape(Q_LORA_RANK, N_HEADS, QK_DIM)
    tail = ((0, 0), (0, 0), (0, HEAD_SLAB - QK_DIM))
    plain = jnp.pad(w, tail).reshape(Q_LORA_RANK, QK_WIDTH)
    if not with_swapped:
        return plain.astype(BF16)
    r = w[:, :, QK_NOPE_DIM:].reshape(Q_LORA_RANK, N_HEADS, 2, 2, N_FREQ)
    sw = jnp.stack([-r[:, :, :, 1, :], r[:, :, :, 0, :]], axis=3).reshape(Q_LORA_RANK, N_HEADS, QK_ROPE_DIM)
    sw = jnp.concatenate([jnp.zeros((Q_LORA_RANK, N_HEADS, QK_NOPE_DIM), w.dtype), sw], axis=2)
    swapped = jnp.pad(sw, tail).reshape(Q_LORA_RANK, QK_WIDTH)
    return jnp.concatenate([plain, swapped], axis=1).astype(BF16)


def _pack_wkv(w_kvb):
    w = w_kvb.reshape(KV_LORA_RANK, N_HEADS, QK_NOPE_DIM + V_HEAD_DIM)
    wk_nope = jnp.pad(w[:, :, :QK_NOPE_DIM], ((0, 0), (0, 0), (0, HEAD_SLAB - QK_NOPE_DIM)))
    sel = jnp.pad(jnp.eye(QK_ROPE_DIM, dtype=w_kvb.dtype),
                  ((0, 0), (QK_NOPE_DIM, HEAD_SLAB - QK_DIM)))
    sel = jnp.broadcast_to(sel[:, None, :], (QK_ROPE_DIM, N_HEADS, HEAD_SLAB))
    zero = jnp.zeros((LANES - QK_ROPE_DIM, N_HEADS, HEAD_SLAB), w_kvb.dtype)
    wk = jnp.concatenate([wk_nope, sel, zero], axis=0).reshape(KV_LORA_RANK + LANES, QK_WIDTH)
    wv = w[:, :, QK_NOPE_DIM:].reshape(KV_LORA_RANK, MLA_WIDTH)
    return wk.astype(BF16), wv.astype(BF16)


def _rope_tables(n_tokens):
    f = np.float32
    rows = n_tokens // GRID_W
    row = np.repeat(np.arange(rows, dtype=f), GRID_W)
    col = np.tile(np.arange(GRID_W, dtype=f), rows)
    inv_freq = (1.0 / (f(ROPE_THETA) ** (np.arange(N_FREQ, dtype=f) * f(2.0) / f(AXIS_ROPE)))).astype(f)
    cos_r, sin_r = np.cos(row[:, None] * inv_freq), np.sin(row[:, None] * inv_freq)
    cos_c, sin_c = np.cos(col[:, None] * inv_freq), np.sin(col[:, None] * inv_freq)
    z = np.zeros_like(cos_r)
    c32 = np.concatenate([cos_r, cos_r, cos_c, cos_c], axis=1)
    s32 = np.concatenate([sin_r, sin_r, sin_c, sin_c], axis=1)
    up32 = np.concatenate([z, sin_r, z, sin_c], axis=1)
    dn32 = np.concatenate([-sin_r, z, -sin_c, z], axis=1)

    def place(t32, off, fill):
        left = np.full((n_tokens, off), fill, f)
        right = np.zeros((n_tokens, LANES - off - QK_ROPE_DIM), f)
        return np.concatenate([left, t32.astype(f), right], axis=1)

    q_tabs = (place(c32 * f(Q_SCALE), QK_NOPE_DIM, Q_SCALE), place(s32 * f(Q_SCALE), QK_NOPE_DIM, 0.0))
    k_tabs = (place(c32, 0, 0.0), place(up32, 0, 0.0), place(dn32, 0, 0.0))
    return q_tabs + k_tabs


def kernel(x_prompt, x_sample, cache_ckv, cache_krope, c, c_ctx, w_ada, b_ada, norm1, w_in, q_norm,
           w_qb, kv_norm, w_kvb, w_pool, pool_scale, w_o, norm2, w1, w2, final_norm):
    batch, seq, _ = x_prompt.shape
    dec_batch, dec_seq, _ = x_sample.shape
    depth = w_in.shape[0]
    past = cache_ckv.shape[2]
    assert depth == 1, "single trunk layer"
    assert dec_batch + 1 <= MOD_ROWS

    c_all = jnp.concatenate(
        [c, c_ctx[None, :], jnp.zeros((MOD_ROWS - dec_batch - 1, D_MODEL), F32)], axis=0)
    mod3 = _mod_call(c_all, w_ada[0], b_ada[0][None, :]).reshape(MOD_ROWS, 6, D_MODEL)

    wk, wv = _pack_wkv(w_kvb[0])

    def s1_weights(with_swapped):
        return (norm1[0][None, :], _pack_w_in(w_in[0]), q_norm[0][None, :],
                _pack_wq(w_qb[0], with_swapped), kv_norm[0][None, :], wk, wv,
                w_pool[0].astype(BF16), pool_scale[0][None, :])

    xc = x_prompt.reshape(batch * seq, D_MODEL)
    ctx_row = lambda i: dec_batch
    attn_c, mc, ckv_new, kr_new = _stage1_call(xc, mod3, ctx_row, s1_weights(False), None, seq,
                                               S1_TILE, seq, True, True)

    xl = x_sample.reshape(dec_batch * dec_seq, D_MODEL)
    lat_row = lambda i: i // (dec_seq // S1_TILE)
    ql, kl, vl, ml = _stage1_call(xl, mod3, lat_row, s1_weights(True), _rope_tables(dec_seq), dec_seq,
                                  S1_TILE, S1_SUB, False, False)
    attn_l, w1_16, w2_16 = _latent_attn_call(
        ql.reshape(dec_batch, dec_seq, QK_WIDTH), kl.reshape(dec_batch, dec_seq, QK_WIDTH),
        vl.reshape(dec_batch, dec_seq, MLA_WIDTH),
        cache_ckv.reshape(dec_batch, past, KV_LORA_RANK), cache_krope.reshape(dec_batch, past, QK_ROPE_DIM),
        wk, wv, LAT_Q_TILE, casts=(w1[0], w2[0]))

    post_weights = (w_o[0].astype(BF16), norm2[0][None, :], w1_16, w2_16, final_norm[None, :])
    y_prompt = _post_call(xc, attn_c, mc, mod3, ctx_row,
                          post_weights, POST_TILE, POST_SUB)
    post_row = lambda i: i // (dec_seq // POST_TILE)
    y_sample = _post_call(xl, attn_l.reshape(dec_batch * dec_seq, MLA_WIDTH), ml, mod3, post_row,
                          post_weights, POST_TILE, POST_SUB)

    return (y_prompt.reshape(batch, seq, D_MODEL), y_sample.reshape(dec_batch, dec_seq, D_MODEL),
            ckv_new.reshape(batch, 1, seq, KV_LORA_RANK), kr_new.reshape(batch, 1, seq, QK_ROPE_DIM))
```
